```python
import jax, jax.numpy as jnp
from jax import lax
import numpy as np

D_MODEL = 1024
BATCH = 2
SEQ = 8192
DEPTH = 2
DEC_BATCH = 128
DEC_SEQ = 4
PAST_LEN = 8192
PAGE_SIZE = 128

N_A = DEPTH // 2
N_B = DEPTH - N_A
N_META = 16
D_FF = 2816
CONV_W = 3
HEAD_DIM = 64
N_HEADS = D_MODEL // HEAD_DIM
N_KV = 4
GROUP = N_HEADS // N_KV
WINDOW = 128
BLOCK = WINDOW
ROPE_THETA = 10000.0
EPS = 1e-6
NEG = -1e30

kernel_name = "yoco_shortconv_swa_sink_decoder_step"


def rmsnorm(x, g):
    x32 = x.astype(jnp.float32)
    y = x32 * lax.rsqrt(jnp.mean(x32 * x32, axis=-1, keepdims=True) + EPS) * g.astype(jnp.float32)
    return y.astype(x.dtype)


def swiglu(h, w_in, w_out):
    gate, up = jnp.split(h @ w_in, 2, axis=-1)
    return (jax.nn.silu(gate) * up) @ w_out


def short_conv_mixer(h, prev_u, w_in, kern, w_out):
    T = h.shape[1]
    b, c, z = jnp.split(h @ w_in, 3, axis=-1)
    u = c * z
    u_full = jnp.concatenate([prev_u.astype(u.dtype), u], axis=1)
    conv = kern[0] * u_full[:, 0:T]
    for j in range(1, CONV_W):
        conv = conv + kern[j] * u_full[:, j:j + T]
    y = (b * conv) @ w_out
    return y, u_full[:, -(CONV_W - 1):]


def rope(x, pos):
    inv = 1.0 / (ROPE_THETA ** (jnp.arange(0, HEAD_DIM, 2, dtype=jnp.float32) / HEAD_DIM))
    ang = pos.astype(jnp.float32)[:, None] * inv[None, :]
    cos = jnp.cos(ang)[None, :, None, :]
    sin = jnp.sin(ang)[None, :, None, :]
    x32 = x.astype(jnp.float32)
    x1, x2 = jnp.split(x32, 2, axis=-1)
    out = jnp.concatenate([x1 * cos - x2 * sin, x2 * cos + x1 * sin], axis=-1)
    return out.astype(x.dtype)


def shared_kv(x, pos, g, w_kv):
    B, T, _ = x.shape
    k, v = jnp.split(rmsnorm(x, g) @ w_kv, 2, axis=-1)
    k = rope(k.reshape(B, T, N_KV, HEAD_DIM), pos)
    return k, v.reshape(B, T, N_KV, HEAD_DIM)


def query(h, w_q, pos):
    B, T, _ = h.shape
    return rope((h @ w_q).reshape(B, T, N_HEADS, HEAD_DIM), pos)


def sink_softmax(scores, mask, sinks):
    s = jnp.where(mask, scores, NEG)
    sk = sinks.astype(jnp.float32).reshape(N_KV, GROUP)[:, :, None, None]
    m = jnp.maximum(jnp.max(s, axis=-1, keepdims=True), sk)
    p = jnp.exp(s - m)
    denom = jnp.sum(p, axis=-1, keepdims=True) + jnp.exp(sk - m)
    return p / denom


def window_attn_prompt(q, k, v, sinks):
    B, L = q.shape[0], q.shape[1]
    P = (-L) % BLOCK
    Lp = L + P
    NB = Lp // BLOCK
    padw = ((0, 0), (P, 0), (0, 0), (0, 0))
    qb = jnp.pad(q, padw).reshape(B, NB, BLOCK, N_KV, GROUP, HEAD_DIM)
    kb = jnp.pad(k, padw).reshape(B, NB, BLOCK, N_KV, HEAD_DIM)
    vb = jnp.pad(v, padw).reshape(B, NB, BLOCK, N_KV, HEAD_DIM)
    kband = jnp.concatenate([jnp.concatenate([jnp.zeros_like(kb[:, :1]), kb[:, :-1]], axis=1), kb], axis=2)
    vband = jnp.concatenate([jnp.concatenate([jnp.zeros_like(vb[:, :1]), vb[:, :-1]], axis=1), vb], axis=2)
    blk = jnp.arange(NB, dtype=jnp.int32)[:, None]
    qpos = blk * BLOCK + jnp.arange(BLOCK, dtype=jnp.int32)[None, :] - P
    kpos = (blk - 1) * BLOCK + jnp.arange(2 * BLOCK, dtype=jnp.int32)[None, :] - P
    dist = qpos[:, :, None] - kpos[:, None, :]
    mask = (kpos[:, None, :] >= 0) & (dist >= 0) & (dist <= WINDOW)
    scores = jnp.einsum('bnqkgd,bnskd->bnkgqs', qb, kband).astype(jnp.float32) * (HEAD_DIM ** -0.5)
    p = sink_softmax(scores, mask[None, :, None, None], sinks)
    out = jnp.einsum('bnkgqs,bnskd->bnqkgd', p.astype(v.dtype), vband)
    return out.reshape(B, Lp, N_HEADS * HEAD_DIM)[:, P:]


def window_attn_sample(q, k_all, v_all, qpos, kpos, sinks):
    B, T = q.shape[0], q.shape[1]
    qg = q.reshape(B, T, N_KV, GROUP, HEAD_DIM)
    dist = qpos[:, None] - kpos[None, :]
    mask = (kpos[None, :] >= 0) & (dist >= 0) & (dist <= WINDOW)
    scores = jnp.einsum('btkgd,bskd->bkgts', qg, k_all).astype(jnp.float32) * (HEAD_DIM ** -0.5)
    p = sink_softmax(scores, mask, sinks)
    out = jnp.einsum('bkgts,bskd->btkgd', p.astype(v_all.dtype), v_all)
    return out.reshape(B, T, N_HEADS * HEAD_DIM)


def setup_inputs(seed: int = 0) -> dict:
    key = jax.random.key(seed)
    ks = jax.random.split(key, 20)
    f32 = jnp.float32
    w_keep = min(WINDOW, PAST_LEN)
    nrm = lambda k, shape, s: jax.random.normal(k, shape, f32) * s
    return {
        "x_prompt": nrm(ks[0], (BATCH, SEQ, D_MODEL), 1.0),
        "x_sample": nrm(ks[1], (DEC_BATCH, DEC_SEQ, D_MODEL), 1.0),
        "state_conv": nrm(ks[2], (N_A, DEC_BATCH, CONV_W - 1, D_MODEL), 1.0),
        "cache_k": nrm(ks[3], (DEC_BATCH, w_keep, N_KV, HEAD_DIM), 1.0),
        "cache_v": nrm(ks[4], (DEC_BATCH, w_keep, N_KV, HEAD_DIM), 1.0),
        "meta_tokens": nrm(ks[5], (N_META, D_MODEL), 1.0),
        "norm_g": 1.0 + nrm(ks[6], (DEPTH, 3, D_MODEL), 0.02),
        "ffn_w_in": nrm(ks[7], (DEPTH, 2, D_MODEL, 2 * D_FF), D_MODEL ** -0.5),
        "ffn_w_out": nrm(ks[8], (DEPTH, 2, D_FF, D_MODEL), D_FF ** -0.5),
        "conv_w_in": nrm(ks[9], (N_A, D_MODEL, 3 * D_MODEL), D_MODEL ** -0.5),
        "conv_kernel": nrm(ks[10], (N_A, CONV_W, D_MODEL), CONV_W ** -0.5),
        "conv_w_out": nrm(ks[11], (N_A, D_MODEL, D_MODEL), D_MODEL ** -0.5),
        "kv_norm_g": 1.0 + nrm(ks[12], (D_MODEL,), 0.02),
        "w_kv": nrm(ks[13], (D_MODEL, 2 * N_KV * HEAD_DIM), D_MODEL ** -0.5),
        "w_q": nrm(ks[14], (N_B, D_MODEL, N_HEADS * HEAD_DIM), D_MODEL ** -0.5),
        "w_o": nrm(ks[15], (N_B, N_HEADS * HEAD_DIM, D_MODEL), (N_HEADS * HEAD_DIM) ** -0.5),
        "sinks": nrm(ks[16], (N_B, N_HEADS), 0.5),
        "final_norm_g": 1.0 + nrm(ks[17], (D_MODEL,), 0.02),
    }


def reference(x_prompt, x_sample, state_conv, cache_k, cache_v, meta_tokens, norm_g,
              ffn_w_in, ffn_w_out, conv_w_in, conv_kernel, conv_w_out, kv_norm_g, w_kv,
              w_q, w_o, sinks, final_norm_g):
    B, Bd = x_prompt.shape[0], x_sample.shape[0]
    L = x_prompt.shape[1] + N_META
    T = x_sample.shape[1]
    w_keep_s = cache_k.shape[1]
    w_keep_p = min(WINDOW, L)

    meta = jnp.broadcast_to(meta_tokens[None].astype(x_prompt.dtype), (B, N_META, D_MODEL))
    xp = jnp.concatenate([meta, x_prompt], axis=1)
    xs = x_sample
    pos_p = jnp.arange(L, dtype=jnp.int32)
    pos_s = PAST_LEN + jnp.arange(T, dtype=jnp.int32)
    kpos_s = jnp.concatenate([PAST_LEN - w_keep_s + jnp.arange(w_keep_s, dtype=jnp.int32), pos_s])

    conv_p, conv_s = [], []
    for layer in range(DEPTH):
        if layer == N_A:
            kp, vp = shared_kv(xp, pos_p, kv_norm_g, w_kv)
            ks_new, vs_new = shared_kv(xs, pos_s, kv_norm_g, w_kv)
            ks_all = jnp.concatenate([cache_k.astype(ks_new.dtype), ks_new], axis=1)
            vs_all = jnp.concatenate([cache_v.astype(vs_new.dtype), vs_new], axis=1)
            new_k_p, new_v_p = kp[:, L - w_keep_p:], vp[:, L - w_keep_p:]
            new_k_s, new_v_s = ks_all[:, -w_keep_s:], vs_all[:, -w_keep_s:]
        g = norm_g[layer]
        xp = xp + 0.5 * swiglu(rmsnorm(xp, g[0]), ffn_w_in[layer, 0], ffn_w_out[layer, 0])
        xs = xs + 0.5 * swiglu(rmsnorm(xs, g[0]), ffn_w_in[layer, 0], ffn_w_out[layer, 0])
        hp, hs = rmsnorm(xp, g[1]), rmsnorm(xs, g[1])
        if layer < N_A:
            a = layer
            zero_prev = jnp.zeros((B, CONV_W - 1, D_MODEL), hp.dtype)
            yp, sp = short_conv_mixer(hp, zero_prev, conv_w_in[a], conv_kernel[a], conv_w_out[a])
            ys, ss = short_conv_mixer(hs, state_conv[a], conv_w_in[a], conv_kernel[a], conv_w_out[a])
            conv_p.append(sp)
            conv_s.append(ss)
        else:
            bi = layer - N_A
            qp = query(hp, w_q[bi], pos_p)
            yp = window_attn_prompt(qp, kp, vp, sinks[bi]) @ w_o[bi]
            qs = query(hs, w_q[bi], pos_s)
            ys = window_attn_sample(qs, ks_all, vs_all, pos_s, kpos_s, sinks[bi]) @ w_o[bi]
        xp = xp + yp
        xs = xs + ys
        xp = xp + 0.5 * swiglu(rmsnorm(xp, g[2]), ffn_w_in[layer, 1], ffn_w_out[layer, 1])
        xs = xs + 0.5 * swiglu(rmsnorm(xs, g[2]), ffn_w_in[layer, 1], ffn_w_out[layer, 1])

    y_prompt = rmsnorm(xp, final_norm_g)[:, N_META:]
    y_sample = rmsnorm(xs, final_norm_g)
    new_state_conv_p = jnp.stack(conv_p, axis=0)
    new_state_conv_s = jnp.stack(conv_s, axis=0)
    return (y_prompt, y_sample, new_state_conv_p, new_state_conv_s, new_k_p, new_v_p, new_k_s, new_v_s)
```

```python
import functools

import jax
import jax.numpy as jnp
from jax import lax
from jax.experimental import pallas as pl
from jax.experimental.pallas import tpu as pltpu

D_MODEL = 1024
BATCH = 2
SEQ = 8192
DEC_BATCH = 128
DEC_SEQ = 4
PAST_LEN = 8192
N_META = 16
D_FF = 2816
CONV_W = 3
HEAD_DIM = 64
N_HEADS = 16
N_KV = 4
GROUP = 4
WINDOW = 128
ROPE_THETA = 10000.0
EPS = 1e-6
NEG = -1e30

KV_W = N_KV * HEAD_DIM
N_MAIN = BATCH * SEQ
N_SAMPLE = DEC_BATCH * DEC_SEQ
N_TAIL = N_SAMPLE + N_META
TM = 512
N_TILES = N_MAIN // TM
TILES_PER_SEQ = SEQ // TM
FF_CHUNK = 256
BLK = WINDOW
BLKS_PER_TILE = TM // BLK
SAMPLE_GROUP = 8
N_GROUPS = DEC_BATCH // SAMPLE_GROUP
GROUP_ROWS = SAMPLE_GROUP * DEC_SEQ
LANES = 128
V7X_VMEM_LIMIT = 56 * 1024 * 1024

F32 = jnp.float32
BF16 = jnp.bfloat16


def _rms(x, g):
    return x * lax.rsqrt(jnp.mean(x * x, axis=-1, keepdims=True) + EPS) * g


def _rope(x, cos, sin_signed):
    lane = lax.broadcasted_iota(jnp.int32, (1, LANES), 1)
    first_half = (lane & (HEAD_DIM - 1)) < (HEAD_DIM // 2)
    outs = []
    for c in range(x.shape[1] // LANES):
        xc = x[:, c * LANES:(c + 1) * LANES]
        partner = jnp.where(first_half,
                            pltpu.roll(xc, LANES - HEAD_DIM // 2, 1),
                            pltpu.roll(xc, HEAD_DIM // 2, 1))
        outs.append(xc * cos + partner * sin_signed)
    return jnp.concatenate(outs, axis=1)


def _dot(a, b):
    return jnp.dot(a, b, preferred_element_type=F32)


def _dot_nt(a, b):
    return lax.dot_general(a, b, (((1,), (1,)), ((), ())), preferred_element_type=F32)


def _resident(shape):
    nd = len(shape)
    return pl.BlockSpec(shape, lambda i: (0,) * nd, pipeline_mode=pl.Buffered(1))


def _main_tile(i):
    return jnp.maximum(i - 1, 0)


def _main_spec(width):
    return pl.BlockSpec((TM, width), lambda i: (_main_tile(i), 0))


def _main_pos_spec():
    return pl.BlockSpec((TM, LANES), lambda i: (_main_tile(i) % TILES_PER_SEQ, 0))


def _params():
    return pltpu.CompilerParams(dimension_semantics=("arbitrary",),
                                vmem_limit_bytes=V7X_VMEM_LIMIT)


def _ffn_rows(x, g, w_in_ref, w_out_ref, a_ref):
    m = x.shape[0]
    hn = _rms(x, g).astype(BF16)
    for c in range(D_FF // FF_CHUNK):
        lo = c * FF_CHUNK
        gate = _dot(hn, w_in_ref[:, lo:lo + FF_CHUNK])
        up = _dot(hn, w_in_ref[:, D_FF + lo:D_FF + lo + FF_CHUNK])
        a_ref[0:m, lo:lo + FF_CHUNK] = (gate * jax.nn.sigmoid(gate) * up).astype(BF16)
    y = _dot(a_ref[0:m, :], w_out_ref[...])
    return x + 0.5 * y


def _ffn_kernel(xm_ref, xt_ref, g_ref, w_in_ref, w_out_ref, om_ref, ot_ref, a_ref):
    i = pl.program_id(0)

    @pl.when(i == 0)
    def _():
        ot_ref[...] = _ffn_rows(xt_ref[...], g_ref[...], w_in_ref, w_out_ref, a_ref)

    @pl.when(i > 0)
    def _():
        om_ref[...] = _ffn_rows(xm_ref[...], g_ref[...], w_in_ref, w_out_ref, a_ref)


def _ffn_final_kernel(xm_ref, xt_ref, g_ref, w_in_ref, w_out_ref, gf_ref,
                      om_ref, ot_ref, a_ref):
    i = pl.program_id(0)

    @pl.when(i == 0)
    def _():
        y = _ffn_rows(xt_ref[...], g_ref[...], w_in_ref, w_out_ref, a_ref)
        ot_ref[...] = _rms(y, gf_ref[...])

    @pl.when(i > 0)
    def _():
        y = _ffn_rows(xm_ref[...], g_ref[...], w_in_ref, w_out_ref, a_ref)
        om_ref[...] = _rms(y, gf_ref[...])


def _ffn_kv_kernel(xm_ref, xt_ref, g_ref, w_in_ref, w_out_ref, gkv_ref, wkv_ref,
                   cm_ref, sm_ref, ct_ref, st_ref,
                   om_ref, ot_ref, km_ref, vm_ref, kt_ref, vt_ref, a_ref):
    i = pl.program_id(0)

    def rows(x_ref, cos_ref, sin_ref, o_ref, k_ref, v_ref):
        y = _ffn_rows(x_ref[...], g_ref[...], w_in_ref, w_out_ref, a_ref)
        o_ref[...] = y
        kv = _dot(_rms(y, gkv_ref[...]).astype(BF16), wkv_ref[...])
        k_ref[...] = _rope(kv[:, :KV_W], cos_ref[...], sin_ref[...])
        v_ref[...] = kv[:, KV_W:]

    @pl.when(i == 0)
    def _():
        rows(xt_ref, ct_ref, st_ref, ot_ref, kt_ref, vt_ref)

    @pl.when(i > 0)
    def _():
        rows(xm_ref, cm_ref, sm_ref, om_ref, km_ref, vm_ref)


def _ffn_call(xm, xt, g, w_in, w_out, *, name, final_g=None, kv=None):
    in_specs = [_main_spec(D_MODEL), _resident((N_TAIL, D_MODEL)), _resident((1, D_MODEL)),
                _resident((D_MODEL, 2 * D_FF)), _resident((D_FF, D_MODEL))]
    args = [xm, xt, g.reshape(1, D_MODEL), w_in, w_out]
    out_shape = [jax.ShapeDtypeStruct((N_MAIN, D_MODEL), F32),
                 jax.ShapeDtypeStruct((N_TAIL, D_MODEL), F32)]
    out_specs = [_main_spec(D_MODEL), pl.BlockSpec((N_TAIL, D_MODEL), lambda i: (0, 0))]
    body = _ffn_kernel
    if final_g is not None:
        body = _ffn_final_kernel
        in_specs.append(_resident((1, D_MODEL)))
        args.append(final_g.reshape(1, D_MODEL))
    if kv is not None:
        body = _ffn_kv_kernel
        g_kv, w_kv, cos_m, sin_m, cos_t, sin_t = kv
        in_specs += [_resident((1, D_MODEL)), _resident((D_MODEL, 2 * KV_W)),
                     _main_pos_spec(), _main_pos_spec(),
                     _resident((N_TAIL, LANES)), _resident((N_TAIL, LANES))]
        args += [g_kv.reshape(1, D_MODEL), w_kv, cos_m, sin_m, cos_t, sin_t]
        out_shape += [jax.ShapeDtypeStruct((N_MAIN, KV_W), F32)] * 2
        out_shape += [jax.ShapeDtypeStruct((N_TAIL, KV_W), F32)] * 2
        out_specs += [_main_spec(KV_W), _main_spec(KV_W),
                      pl.BlockSpec((N_TAIL, KV_W), lambda i: (0, 0)),
                      pl.BlockSpec((N_TAIL, KV_W), lambda i: (0, 0))]
    return pl.pallas_call(
        body,
        grid=(1 + N_TILES,),
        in_specs=in_specs,
        out_specs=out_specs,
        out_shape=out_shape,
        scratch_shapes=[pltpu.VMEM((N_TAIL, D_FF), BF16)],
        compiler_params=_params(),
        name=name,
    )(*args)


CARRY = 8


def _conv_rows(x, g, w_in_ref, kern_ref, w_out_ref, ubuf_ref, fix):
    m = x.shape[0]
    bcz = _dot(_rms(x, g).astype(BF16), w_in_ref[...])
    b = bcz[:, :D_MODEL]
    u = bcz[:, D_MODEL:2 * D_MODEL] * bcz[:, 2 * D_MODEL:]
    ubuf_ref[CARRY:CARRY + m, :] = u
    prev1 = ubuf_ref[CARRY - 1:CARRY - 1 + m, :]
    prev2 = ubuf_ref[CARRY - 2:CARRY - 2 + m, :]
    if fix is not None:
        prev1, prev2 = fix(prev1, prev2)
    kern = kern_ref[...]
    conv = kern[0:1, :] * prev2 + kern[1:2, :] * prev1 + kern[2:3, :] * u
    return x + _dot((b * conv).astype(BF16), w_out_ref[...])


def _conv_kernel(xm_ref, xt_ref, g_ref, w_in_ref, kern_ref, w_out_ref, s1_ref, s2_ref,
                 om_ref, ot_ref, ut_ref, sp_ref, ubuf_ref, meta_ref):
    i = pl.program_id(0)

    @pl.when(i == 0)
    def _():
        ubuf_ref[0:CARRY, :] = jnp.zeros((CARRY, D_MODEL), F32)

        def fix(prev1, prev2):
            r = lax.broadcasted_iota(jnp.int32, (N_TAIL, 1), 0)
            step = jnp.where(r < N_SAMPLE, r & (DEC_SEQ - 1), r - N_SAMPLE)
            return (jnp.where(step >= 1, prev1, s1_ref[...]),
                    jnp.where(step >= 2, prev2, s2_ref[...]))

        ot_ref[...] = _conv_rows(xt_ref[...], g_ref[...], w_in_ref, kern_ref, w_out_ref,
                                 ubuf_ref, fix)
        ut_ref[...] = ubuf_ref[CARRY:CARRY + N_TAIL, :]
        meta_ref[...] = ubuf_ref[N_TAIL:N_TAIL + CARRY, :]

    @pl.when(i > 0)
    def _():
        t = i - 1

        @pl.when(t % TILES_PER_SEQ == 0)
        def _():
            ubuf_ref[0:CARRY, :] = meta_ref[...]

        om_ref[...] = _conv_rows(xm_ref[...], g_ref[...], w_in_ref, kern_ref, w_out_ref,
                                 ubuf_ref, None)
        last = ubuf_ref[TM:TM + CARRY, :]
        ubuf_ref[0:CARRY, :] = last

        @pl.when(t % TILES_PER_SEQ == TILES_PER_SEQ - 1)
        def _():
            sp_ref[t // TILES_PER_SEQ] = last


def _conv_call(xm, xt, g, w_in, kern, w_out, s1, s2):
    return pl.pallas_call(
        _conv_kernel,
        grid=(1 + N_TILES,),
        in_specs=[_main_spec(D_MODEL), _resident((N_TAIL, D_MODEL)), _resident((1, D_MODEL)),
                  _resident((D_MODEL, 3 * D_MODEL)), _resident((CONV_W, D_MODEL)),
                  _resident((D_MODEL, D_MODEL)),
                  _resident((N_TAIL, D_MODEL)), _resident((N_TAIL, D_MODEL))],
        out_specs=[_main_spec(D_MODEL),
                   pl.BlockSpec((N_TAIL, D_MODEL), lambda i: (0, 0)),
                   pl.BlockSpec((N_TAIL, D_MODEL), lambda i: (0, 0)),
                   pl.BlockSpec((BATCH, CARRY, D_MODEL), lambda i: (0, 0, 0))],
        out_shape=[jax.ShapeDtypeStruct((N_MAIN, D_MODEL), F32),
                   jax.ShapeDtypeStruct((N_TAIL, D_MODEL), F32),
                   jax.ShapeDtypeStruct((N_TAIL, D_MODEL), F32),
                   jax.ShapeDtypeStruct((BATCH, CARRY, D_MODEL), F32)],
        scratch_shapes=[pltpu.VMEM((CARRY + N_TAIL, D_MODEL), F32),
                        pltpu.VMEM((CARRY, D_MODEL), F32)],
        compiler_params=_params(),
        name="conv_mixer",
    )(xm, xt, g.reshape(1, D_MODEL), w_in, kern, w_out, s1, s2)


def _attend(qp, segs, sinks_ref):
    mq = qp.shape[0]
    lane_head = lax.broadcasted_iota(jnp.int32, (1, KV_W), 1) // HEAD_DIM
    row_group = lax.broadcasted_iota(jnp.int32, (GROUP * mq, 1), 0) // mq
    masks = [jnp.concatenate([mask] * GROUP, axis=0) for (_, _, mask) in segs]
    probs, inv = [], []
    for k in range(N_KV):
        head = lane_head == k
        qs = jnp.concatenate(
            [jnp.where(head, qp[:, g * KV_W:(g + 1) * KV_W], jnp.zeros((), BF16))
             for g in range(GROUP)], axis=0)
        sink = jnp.full((GROUP * mq, 1), sinks_ref[k * GROUP], F32)
        for g in range(1, GROUP):
            sink = jnp.where(row_group == g, sinks_ref[k * GROUP + g], sink)
        scores = [jnp.where(msk, _dot_nt(qs, kb), NEG) for (kb, _, _), msk in zip(segs, masks)]
        mx = sink
        for s in scores:
            mx = jnp.maximum(mx, jnp.max(s, axis=-1, keepdims=True))
        denom = jnp.exp(sink - mx)
        pk = []
        for s in scores:
            p = jnp.exp(s - mx)
            denom = denom + jnp.sum(p, axis=-1, keepdims=True)
            pk.append(p.astype(BF16))
        probs.append(pk)
        inv.append(1.0 / denom)
    out = jnp.zeros((GROUP * mq, KV_W), F32)
    for j, (_, vb, _) in enumerate(segs):
        vheads = [jnp.where(lane_head == k, vb, jnp.zeros((), BF16)) for k in range(N_KV)]
        if vb.shape[0] % LANES == 0:
            out = out + _dot(jnp.concatenate([probs[k][j] for k in range(N_KV)], axis=1),
                             jnp.concatenate(vheads, axis=0))
        else:
            for k in range(N_KV):
                out = out + _dot(probs[k][j], vheads[k])
    scale = inv[0]
    for k in range(1, N_KV):
        scale = jnp.where(lane_head == k, inv[k], scale)
    out = out * scale
    return jnp.concatenate([out[g * mq:(g + 1) * mq, :] for g in range(GROUP)], axis=1)


def _queries(x, g, wq_ref, cos, sin):
    q = _dot(_rms(x, g).astype(BF16), wq_ref[...])
    return (_rope(q, cos, sin) * (HEAD_DIM ** -0.5)).astype(BF16)


def _attn_main_kernel(sinks_ref, xm_ref, g_ref, wq_ref, wo_ref, cos_ref, sin_ref,
                      km_ref, vm_ref, kp_ref, vp_ref, kmeta_ref, vmeta_ref,
                      om_ref, q_ref, ao_ref, kprev_ref, vprev_ref):
    i = pl.program_id(0)
    start = i % TILES_PER_SEQ == 0
    x = xm_ref[...]
    q_ref[...] = _queries(x, g_ref[...], wq_ref, cos_ref[...], sin_ref[...])

    @pl.when(start)
    def _():
        kprev_ref[...] = kmeta_ref[...].astype(BF16)
        vprev_ref[...] = vmeta_ref[...].astype(BF16)

    @pl.when(jnp.logical_not(start))
    def _():
        kprev_ref[...] = kp_ref[...].astype(BF16)
        vprev_ref[...] = vp_ref[...].astype(BF16)

    qi = lax.broadcasted_iota(jnp.int32, (BLK, BLK), 0)
    kj = lax.broadcasted_iota(jnp.int32, (BLK, BLK), 1)
    first_key = jnp.where(start, BLK - N_META, 0)
    for blk in range(BLKS_PER_TILE):
        lo = blk * BLK
        kc = km_ref[lo:lo + BLK, :].astype(BF16)
        vc = vm_ref[lo:lo + BLK, :].astype(BF16)
        if blk == 0:
            kp, vp = kprev_ref[...], vprev_ref[...]
            mask_prev = (kj >= qi) & (kj >= first_key)
        else:
            kp = km_ref[lo - BLK:lo, :].astype(BF16)
            vp = vm_ref[lo - BLK:lo, :].astype(BF16)
            mask_prev = kj >= qi
        att = _attend(q_ref[lo:lo + BLK, :], [(kp, vp, mask_prev), (kc, vc, kj <= qi)],
                      sinks_ref)
        ao_ref[lo:lo + BLK, :] = att.astype(BF16)
    om_ref[...] = x + _dot(ao_ref[...], wo_ref[...])


def _attn_main_call(sinks, xm, g, wq, wo, cos_m, sin_m, km, vm, kmeta, vmeta):
    tile = lambda w: pl.BlockSpec((TM, w), lambda i: (i, 0))
    prev = pl.BlockSpec((BLK, KV_W), lambda i: (jnp.maximum(i * BLKS_PER_TILE - 1, 0), 0))
    pos = pl.BlockSpec((TM, LANES), lambda i: (i % TILES_PER_SEQ, 0))
    return pl.pallas_call(
        _attn_main_kernel,
        grid=(N_TILES,),
        in_specs=[pl.BlockSpec(memory_space=pltpu.SMEM),
                  tile(D_MODEL), _resident((1, D_MODEL)),
                  _resident((D_MODEL, D_MODEL)), _resident((D_MODEL, D_MODEL)),
                  pos, pos, tile(KV_W), tile(KV_W), prev, prev,
                  _resident((BLK, KV_W)), _resident((BLK, KV_W))],
        out_specs=tile(D_MODEL),
        out_shape=jax.ShapeDtypeStruct((N_MAIN, D_MODEL), F32),
        scratch_shapes=[pltpu.VMEM((TM, D_MODEL), BF16), pltpu.VMEM((TM, D_MODEL), BF16),
                        pltpu.VMEM((BLK, KV_W), BF16), pltpu.VMEM((BLK, KV_W), BF16)],
        compiler_params=_params(),
        name="attn_main",
    )(sinks, xm, g.reshape(1, D_MODEL), wq, wo, cos_m, sin_m, km, vm, km, vm, kmeta, vmeta)


def _attn_tail_kernel(sinks_ref, xt_ref, g_ref, wq_ref, wo_ref, cos_ref, sin_ref,
                      kt_ref, vt_ref, ck_ref, cv_ref, ot_ref, q_ref, ao_ref):
    s = pl.program_id(0)

    @pl.when(s == 0)
    def _():
        q_ref[...] = _queries(xt_ref[...], g_ref[...], wq_ref, cos_ref[...], sin_ref[...])
        qi = lax.broadcasted_iota(jnp.int32, (N_META, N_META), 0)
        kj = lax.broadcasted_iota(jnp.int32, (N_META, N_META), 1)
        km = kt_ref[N_SAMPLE:N_TAIL, :].astype(BF16)
        vm = vt_ref[N_SAMPLE:N_TAIL, :].astype(BF16)
        att = _attend(q_ref[N_SAMPLE:N_TAIL, :], [(km, vm, kj <= qi)], sinks_ref)
        ao_ref[N_SAMPLE:N_TAIL, :] = att.astype(BF16)

    rows = pl.ds(pl.multiple_of(s * GROUP_ROWS, GROUP_ROWS), GROUP_ROWS)
    n_cached = SAMPLE_GROUP * WINDOW
    qr = lax.broadcasted_iota(jnp.int32, (GROUP_ROWS, n_cached), 0)
    kc = lax.broadcasted_iota(jnp.int32, (GROUP_ROWS, n_cached), 1)
    mask_cached = ((kc // WINDOW) == (qr // DEC_SEQ)) & ((kc % WINDOW) >= (qr % DEC_SEQ))
    qn = lax.broadcasted_iota(jnp.int32, (GROUP_ROWS, GROUP_ROWS), 0)
    kn = lax.broadcasted_iota(jnp.int32, (GROUP_ROWS, GROUP_ROWS), 1)
    mask_new = ((kn // DEC_SEQ) == (qn // DEC_SEQ)) & ((kn % DEC_SEQ) <= (qn % DEC_SEQ))
    att = _attend(q_ref[rows, :],
                  [(ck_ref[...].astype(BF16), cv_ref[...].astype(BF16), mask_cached),
                   (kt_ref[rows, :].astype(BF16), vt_ref[rows, :].astype(BF16), mask_new)],
                  sinks_ref)
    ao_ref[rows, :] = att.astype(BF16)

    @pl.when(s == N_GROUPS - 1)
    def _():
        ot_ref[...] = xt_ref[...] + _dot(ao_ref[...], wo_ref[...])


def _attn_tail_call(sinks, xt, g, wq, wo, cos_t, sin_t, kt, vt, ck, cv):
    cached = pl.BlockSpec((SAMPLE_GROUP * WINDOW, KV_W), lambda s: (s, 0))
    return pl.pallas_call(
        _attn_tail_kernel,
        grid=(N_GROUPS,),
        in_specs=[pl.BlockSpec(memory_space=pltpu.SMEM),
                  _resident((N_TAIL, D_MODEL)), _resident((1, D_MODEL)),
                  _resident((D_MODEL, D_MODEL)), _resident((D_MODEL, D_MODEL)),
                  _resident((N_TAIL, LANES)), _resident((N_TAIL, LANES)),
                  _resident((N_TAIL, KV_W)), _resident((N_TAIL, KV_W)),
                  cached, cached],
        out_specs=pl.BlockSpec((N_TAIL, D_MODEL), lambda s: (0, 0)),
        out_shape=jax.ShapeDtypeStruct((N_TAIL, D_MODEL), F32),
        scratch_shapes=[pltpu.VMEM((N_TAIL, D_MODEL), BF16), pltpu.VMEM((N_TAIL, D_MODEL), BF16)],
        compiler_params=_params(),
        name="attn_tail",
    )(sinks, xt, g.reshape(1, D_MODEL), wq, wo, cos_t, sin_t, kt, vt, ck, cv)


def _rope_tables(pos):
    inv = 1.0 / (ROPE_THETA ** (jnp.arange(0, HEAD_DIM, 2, dtype=F32) / HEAD_DIM))
    ang = pos.astype(F32)[:, None] * inv[None, :]
    cos, sin = jnp.cos(ang), jnp.sin(ang)
    reps = LANES // HEAD_DIM
    return (jnp.tile(jnp.concatenate([cos, cos], axis=1), (1, reps)),
            jnp.tile(jnp.concatenate([-sin, sin], axis=1), (1, reps)))


def kernel(x_prompt, x_sample, state_conv, cache_k, cache_v, meta_tokens, norm_g, ffn_w_in,
           ffn_w_out, conv_w_in, conv_kernel, conv_w_out, kv_norm_g, w_kv, w_q, w_o, sinks,
           final_norm_g):
    assert x_prompt.shape == (BATCH, SEQ, D_MODEL) and x_sample.shape == (DEC_BATCH, DEC_SEQ, D_MODEL)
    assert cache_k.shape == (DEC_BATCH, WINDOW, N_KV, HEAD_DIM)
    assert norm_g.shape[0] == 2 and conv_w_in.shape[0] == 1 and w_q.shape[0] == 1

    xm = x_prompt.reshape(N_MAIN, D_MODEL)
    xt = jnp.concatenate([x_sample.reshape(N_SAMPLE, D_MODEL), meta_tokens], axis=0)

    w_in = ffn_w_in.astype(BF16)
    w_out = ffn_w_out.astype(BF16)
    wq = w_q[0].reshape(D_MODEL, N_KV, GROUP, HEAD_DIM).transpose(0, 2, 1, 3)
    wq = wq.reshape(D_MODEL, D_MODEL).astype(BF16)
    wo = w_o[0].reshape(N_KV, GROUP, HEAD_DIM, D_MODEL).transpose(1, 0, 2, 3)
    wo = wo.reshape(D_MODEL, D_MODEL).astype(BF16)

    pos_main = N_META + jnp.arange(SEQ, dtype=jnp.int32)
    pos_tail = jnp.concatenate([
        jnp.tile(PAST_LEN + jnp.arange(DEC_SEQ, dtype=jnp.int32), DEC_BATCH),
        jnp.arange(N_META, dtype=jnp.int32)])
    cos_m, sin_m = _rope_tables(pos_main)
    cos_t, sin_t = _rope_tables(pos_tail)

    sc = state_conv[0]
    zeros = jnp.zeros((DEC_BATCH, DEC_SEQ, D_MODEL), F32)
    s1 = zeros.at[:, 0].set(sc[:, 1])
    s2 = zeros.at[:, 0].set(sc[:, 0]).at[:, 1].set(sc[:, 1])
    pad_meta = lambda a: jnp.pad(a.reshape(N_SAMPLE, D_MODEL), ((0, N_META), (0, 0)))
    s1, s2 = pad_meta(s1), pad_meta(s2)

    xm, xt = _ffn_call(xm, xt, norm_g[0, 0], w_in[0, 0], w_out[0, 0], name="ffn_l0a")
    xm, xt, u_tail, sp = _conv_call(xm, xt, norm_g[0, 1], conv_w_in[0].astype(BF16),
                                    conv_kernel[0], conv_w_out[0].astype(BF16), s1, s2)
    xm, xt, km, vm, kt, vt = _ffn_call(
        xm, xt, norm_g[0, 2], w_in[0, 1], w_out[0, 1], name="ffn_l0b_kv",
        kv=(kv_norm_g, w_kv.astype(BF16), cos_m, sin_m, cos_t, sin_t))

    xm, xt = _ffn_call(xm, xt, norm_g[1, 0], w_in[1, 0], w_out[1, 0], name="ffn_l1a")
    ck = cache_k.reshape(DEC_BATCH * WINDOW, KV_W)
    cv = cache_v.reshape(DEC_BATCH * WINDOW, KV_W)
    xt = _attn_tail_call(sinks[0], xt, norm_g[1, 1], wq, wo, cos_t, sin_t, kt, vt, ck, cv)
    front = ((BLK - N_META, 0), (0, 0))
    xm = _attn_main_call(sinks[0], xm, norm_g[1, 1], wq, wo, cos_m, sin_m, km, vm,
                         jnp.pad(kt[N_SAMPLE:], front), jnp.pad(vt[N_SAMPLE:], front))
    ym, yt = _ffn_call(xm, xt, norm_g[1, 2], w_in[1, 1], w_out[1, 1], name="ffn_l1b_final",
                       final_g=final_norm_g)

    y_prompt = ym.reshape(BATCH, SEQ, D_MODEL)
    y_sample = yt[:N_SAMPLE].reshape(DEC_BATCH, DEC_SEQ, D_MODEL)
    new_state_conv_p = sp[:, CARRY - (CONV_W - 1):][None]
    new_state_conv_s = u_tail[:N_SAMPLE].reshape(DEC_BATCH, DEC_SEQ, D_MODEL)[:, DEC_SEQ - (CONV_W - 1):][None]
    heads = lambda a, b: a.reshape(b, -1, N_KV, HEAD_DIM)
    new_k_p = heads(km, BATCH)[:, SEQ - WINDOW:]
    new_v_p = heads(vm, BATCH)[:, SEQ - WINDOW:]
    new_k_s = jnp.concatenate([cache_k, heads(kt[:N_SAMPLE], DEC_BATCH)], axis=1)[:, DEC_SEQ:]
    new_v_s = jnp.concatenate([cache_v, heads(vt[:N_SAMPLE], DEC_BATCH)], axis=1)[:, DEC_SEQ:]
    return (y_prompt, y_sample, new_state_conv_p, new_state_conv_s,
            new_k_p, new_v_p, new_k_s, new_v_s)
```

```python
import jax
import jax.numpy as jnp
from jax import lax
from jax.experimental import pallas as pl
from jax.experimental.pallas import tpu as pltpu

D_MODEL = 1024
BATCH = 2
SEQ = 8192
DEC_BATCH = 128
DEC_SEQ = 4
PAST_LEN = 8192
N_META = 16
D_FF = 2816
CONV_W = 3
HEAD_DIM = 64
N_HEADS = 16
N_KV = 4
GROUP = 4
WINDOW = 128
ROPE_THETA = 10000.0
EPS = 1e-6
NEG = -1e30

KV_W = N_KV * HEAD_DIM
N_MAIN = BATCH * SEQ
N_SAMPLE = DEC_BATCH * DEC_SEQ
N_TAIL = N_SAMPLE + N_META
TM = 512
N_TILES = N_MAIN // TM
TILES_PER_SEQ = SEQ // TM
FF_CHUNK = 256
BLK = WINDOW
BLKS_PER_TILE = TM // BLK
SAMPLE_GROUP = 8
N_GROUPS = DEC_BATCH // SAMPLE_GROUP
GROUP_ROWS = SAMPLE_GROUP * DEC_SEQ
LANES = 128
BF16_ROWS = 16
V7X_VMEM_LIMIT = 56 * 1024 * 1024

F32 = jnp.float32
BF16 = jnp.bfloat16


def _rms(x, g):
    return x * lax.rsqrt(jnp.mean(x * x, axis=-1, keepdims=True) + EPS) * g


def _rope(x, cos, sin_signed):
    lane = lax.broadcasted_iota(jnp.int32, (1, LANES), 1)
    first_half = (lane & (HEAD_DIM - 1)) < (HEAD_DIM // 2)
    outs = []
    for c in range(x.shape[1] // LANES):
        xc = x[:, c * LANES:(c + 1) * LANES]
        partner = jnp.where(first_half,
                            pltpu.roll(xc, LANES - HEAD_DIM // 2, 1),
                            pltpu.roll(xc, HEAD_DIM // 2, 1))
        outs.append(xc * cos + partner * sin_signed)
    return jnp.concatenate(outs, axis=1)


def _dot(a, b):
    return jnp.dot(a, b, preferred_element_type=F32)


def _dot_nt(a, b):
    return lax.dot_general(a, b, (((1,), (1,)), ((), ())), preferred_element_type=F32)


def _resident(shape):
    nd = len(shape)
    return pl.BlockSpec(shape, lambda i: (0,) * nd, pipeline_mode=pl.Buffered(1))


def _main_tile(i):
    return jnp.maximum(i - 1, 0)


def _main_spec(width):
    return pl.BlockSpec((TM, width), lambda i: (_main_tile(i), 0))


def _main_pos_spec():
    return pl.BlockSpec((TM, LANES), lambda i: (_main_tile(i) % TILES_PER_SEQ, 0))


def _params():
    return pltpu.CompilerParams(dimension_semantics=("arbitrary",),
                                vmem_limit_bytes=V7X_VMEM_LIMIT)


def _ffn_rows(x, g, w_in_ref, w_out_ref, a_ref):
    m = x.shape[0]
    hn = _rms(x, g).astype(BF16)
    for c in range(D_FF // FF_CHUNK):
        lo = c * FF_CHUNK
        gate = _dot(hn, w_in_ref[:, lo:lo + FF_CHUNK])
        up = _dot(hn, w_in_ref[:, D_FF + lo:D_FF + lo + FF_CHUNK])
        a_ref[0:m, lo:lo + FF_CHUNK] = (gate * jax.nn.sigmoid(gate) * up).astype(BF16)
    y = _dot(a_ref[0:m, :], w_out_ref[...])
    return x + 0.5 * y


def _make_ffn_kernel(has_final, has_kv, n_cast):
    def body(*refs):
        refs = list(refs)
        xm_ref, xt_ref, g_ref, w_in_ref, w_out_ref = refs[:5]
        pos = 5
        if has_final:
            gf_ref = refs[pos]
            pos += 1
        if has_kv:
            gkv_ref, wkv_ref, cm_ref, sm_ref, ct_ref, st_ref = refs[pos:pos + 6]
            pos += 6
        cast_src = refs[pos:pos + n_cast]
        pos += n_cast
        om_ref, ot_ref = refs[pos:pos + 2]
        pos += 2
        if has_kv:
            km_ref, vm_ref, kt_ref, vt_ref, kl_ref, vl_ref = refs[pos:pos + 6]
            pos += 6
        cast_dst = refs[pos:pos + n_cast]
        pos += n_cast
        a_ref = refs[pos]
        i = pl.program_id(0)

        for src, dst in zip(cast_src, cast_dst):
            dst[...] = src[...].astype(BF16)

        def rows(x_ref, o_ref, cos_ref, sin_ref, k_ref, v_ref):
            y = _ffn_rows(x_ref[...], g_ref[...], w_in_ref, w_out_ref, a_ref)
            if has_final:
                y = _rms(y, gf_ref[...])
            o_ref[...] = y
            if not has_kv:
                return None, None
            kv = _dot(_rms(y, gkv_ref[...]).astype(BF16), wkv_ref[...])
            k = _rope(kv[:, :KV_W], cos_ref[...], sin_ref[...])
            v = kv[:, KV_W:]
            k_ref[...] = k
            v_ref[...] = v
            return k, v

        @pl.when(i == 0)
        def _():
            if has_kv:
                rows(xt_ref, ot_ref, ct_ref, st_ref, kt_ref, vt_ref)
            else:
                rows(xt_ref, ot_ref, None, None, None, None)

        @pl.when(i > 0)
        def _():
            if has_kv:
                k, v = rows(xm_ref, om_ref, cm_ref, sm_ref, km_ref, vm_ref)

                @pl.when((i - 1) % TILES_PER_SEQ == TILES_PER_SEQ - 1)
                def _():
                    kl_ref[...] = k[TM - WINDOW:, :].T
                    vl_ref[...] = v[TM - WINDOW:, :].T
            else:
                rows(xm_ref, om_ref, None, None, None, None)

    return body


def _cast_specs(casts):
    in_specs, out_specs, out_shape, args = [], [], [], []
    for arr, lead, steps in casts:
        n_rows, n_cols = arr.shape[-2:]
        rows = n_rows // steps
        assert rows * steps == n_rows and rows % BF16_ROWS == 0
        blk = lambda i, steps=steps: jnp.minimum(_main_tile(i), steps - 1)
        in_specs.append(pl.BlockSpec((None,) * len(lead) + (rows, n_cols),
                                     lambda i, lead=lead, blk=blk: lead + (blk(i), 0)))
        out_specs.append(pl.BlockSpec((rows, n_cols), lambda i, blk=blk: (blk(i), 0)))
        out_shape.append(jax.ShapeDtypeStruct((n_rows, n_cols), BF16))
        args.append(arr)
    return in_specs, out_specs, out_shape, args


def _ffn_call(xm, xt, g, w_in, w_out, *, name, final_g=None, kv=None, casts=()):
    in_specs = [_main_spec(D_MODEL), _resident((N_TAIL, D_MODEL)), _resident((1, D_MODEL)),
                _resident((D_MODEL, 2 * D_FF)), _resident((D_FF, D_MODEL))]
    args = [xm, xt, g.reshape(1, D_MODEL), w_in, w_out]
    out_shape = [jax.ShapeDtypeStruct((N_MAIN, D_MODEL), F32),
                 jax.ShapeDtypeStruct((N_TAIL, D_MODEL), F32)]
    out_specs = [_main_spec(D_MODEL), pl.BlockSpec((N_TAIL, D_MODEL), lambda i: (0, 0))]
    if final_g is not None:
        in_specs.append(_resident((1, D_MODEL)))
        args.append(final_g.reshape(1, D_MODEL))
    if kv is not None:
        g_kv, w_kv, cos_m, sin_m, cos_t, sin_t = kv
        in_specs += [_resident((1, D_MODEL)), _resident((D_MODEL, 2 * KV_W)),
                     _main_pos_spec(), _main_pos_spec(),
                     _resident((N_TAIL, LANES)), _resident((N_TAIL, LANES))]
        args += [g_kv.reshape(1, D_MODEL), w_kv, cos_m, sin_m, cos_t, sin_t]
        out_shape += [jax.ShapeDtypeStruct((N_MAIN, KV_W), F32)] * 2
        out_shape += [jax.ShapeDtypeStruct((N_TAIL, KV_W), F32)] * 2
        out_shape += [jax.ShapeDtypeStruct((BATCH, KV_W, WINDOW), F32)] * 2
        last = pl.BlockSpec((None, KV_W, WINDOW), lambda i: (_main_tile(i) // TILES_PER_SEQ, 0, 0))
        out_specs += [_main_spec(KV_W), _main_spec(KV_W),
                      pl.BlockSpec((N_TAIL, KV_W), lambda i: (0, 0)),
                      pl.BlockSpec((N_TAIL, KV_W), lambda i: (0, 0)), last, last]
    c_in, c_out, c_shape, c_args = _cast_specs(casts)
    return pl.pallas_call(
        _make_ffn_kernel(final_g is not None, kv is not None, len(casts)),
        grid=(1 + N_TILES,),
        in_specs=in_specs + c_in,
        out_specs=out_specs + c_out,
        out_shape=out_shape + c_shape,
        scratch_shapes=[pltpu.VMEM((N_TAIL, D_FF), BF16)],
        compiler_params=_params(),
        name=name,
    )(*args, *c_args)


CARRY = 8


def _conv_rows(x, g, w_in_ref, kern_ref, w_out_ref, ubuf_ref, fix):
    m = x.shape[0]
    bcz = _dot(_rms(x, g).astype(BF16), w_in_ref[...])
    b = bcz[:, :D_MODEL]
    u = bcz[:, D_MODEL:2 * D_MODEL] * bcz[:, 2 * D_MODEL:]
    ubuf_ref[CARRY:CARRY + m, :] = u
    prev1 = ubuf_ref[CARRY - 1:CARRY - 1 + m, :]
    prev2 = ubuf_ref[CARRY - 2:CARRY - 2 + m, :]
    if fix is not None:
        prev1, prev2 = fix(prev1, prev2)
    kern = kern_ref[...]
    conv = kern[0:1, :] * prev2 + kern[1:2, :] * prev1 + kern[2:3, :] * u
    return x + _dot((b * conv).astype(BF16), w_out_ref[...])


def _conv_kernel(xm_ref, xt_ref, g_ref, w_in_ref, kern_ref, w_out_ref, s1_ref, s2_ref,
                 om_ref, ot_ref, ut_ref, sp_ref, ubuf_ref, meta_ref):
    i = pl.program_id(0)

    @pl.when(i == 0)
    def _():
        ubuf_ref[0:CARRY, :] = jnp.zeros((CARRY, D_MODEL), F32)

        def fix(prev1, prev2):
            r = lax.broadcasted_iota(jnp.int32, (N_TAIL, 1), 0)
            step = jnp.where(r < N_SAMPLE, r & (DEC_SEQ - 1), r - N_SAMPLE)
            return (jnp.where(step >= 1, prev1, s1_ref[...]),
                    jnp.where(step >= 2, prev2, s2_ref[...]))

        ot_ref[...] = _conv_rows(xt_ref[...], g_ref[...], w_in_ref, kern_ref, w_out_ref,
                                 ubuf_ref, fix)
        ut_ref[...] = ubuf_ref[CARRY:CARRY + N_TAIL, :]
        meta_ref[...] = ubuf_ref[N_TAIL:N_TAIL + CARRY, :]

    @pl.when(i > 0)
    def _():
        t = i - 1

        @pl.when(t % TILES_PER_SEQ == 0)
        def _():
            ubuf_ref[0:CARRY, :] = meta_ref[...]

        om_ref[...] = _conv_rows(xm_ref[...], g_ref[...], w_in_ref, kern_ref, w_out_ref,
                                 ubuf_ref, None)
        last = ubuf_ref[TM:TM + CARRY, :]
        ubuf_ref[0:CARRY, :] = last

        @pl.when(t % TILES_PER_SEQ == TILES_PER_SEQ - 1)
        def _():
            sp_ref[t // TILES_PER_SEQ] = last


def _conv_call(xm, xt, g, w_in, kern, w_out, s1, s2):
    return pl.pallas_call(
        _conv_kernel,
        grid=(1 + N_TILES,),
        in_specs=[_main_spec(D_MODEL), _resident((N_TAIL, D_MODEL)), _resident((1, D_MODEL)),
                  _resident((D_MODEL, 3 * D_MODEL)), _resident((CONV_W, D_MODEL)),
                  _resident((D_MODEL, D_MODEL)),
                  _resident((N_TAIL, D_MODEL)), _resident((N_TAIL, D_MODEL))],
        out_specs=[_main_spec(D_MODEL),
                   pl.BlockSpec((N_TAIL, D_MODEL), lambda i: (0, 0)),
                   pl.BlockSpec((N_TAIL, D_MODEL), lambda i: (0, 0)),
                   pl.BlockSpec((BATCH, CARRY, D_MODEL), lambda i: (0, 0, 0))],
        out_shape=[jax.ShapeDtypeStruct((N_MAIN, D_MODEL), F32),
                   jax.ShapeDtypeStruct((N_TAIL, D_MODEL), F32),
                   jax.ShapeDtypeStruct((N_TAIL, D_MODEL), F32),
                   jax.ShapeDtypeStruct((BATCH, CARRY, D_MODEL), F32)],
        scratch_shapes=[pltpu.VMEM((CARRY + N_TAIL, D_MODEL), F32),
                        pltpu.VMEM((CARRY, D_MODEL), F32)],
        compiler_params=_params(),
        name="conv_mixer",
    )(xm, xt, g.reshape(1, D_MODEL), w_in, kern, w_out, s1, s2)


def _attend(qp, segs, sinks_ref):
    mq = qp.shape[0]
    lane_head = lax.broadcasted_iota(jnp.int32, (1, KV_W), 1) // HEAD_DIM
    row_group = lax.broadcasted_iota(jnp.int32, (GROUP * mq, 1), 0) // mq
    masks = [jnp.concatenate([seg[2]] * GROUP, axis=0) for seg in segs]
    probs, inv = [], []
    for k in range(N_KV):
        head = lane_head == k
        qs = jnp.concatenate(
            [jnp.where(head, qp[:, g * KV_W:(g + 1) * KV_W], jnp.zeros((), BF16))
             for g in range(GROUP)], axis=0)
        sink = jnp.full((GROUP * mq, 1), sinks_ref[k * GROUP], F32)
        for g in range(1, GROUP):
            sink = jnp.where(row_group == g, sinks_ref[k * GROUP + g], sink)
        scores = []
        for (kb, _, _, k_transposed), msk in zip(segs, masks):
            s = _dot(qs, kb) if k_transposed else _dot_nt(qs, kb)
            scores.append(jnp.where(msk, s, NEG))
        mx = sink
        for s in scores:
            mx = jnp.maximum(mx, jnp.max(s, axis=-1, keepdims=True))
        denom = jnp.exp(sink - mx)
        pk = []
        for s in scores:
            p = jnp.exp(s - mx)
            denom = denom + jnp.sum(p, axis=-1, keepdims=True)
            pk.append(p.astype(BF16))
        probs.append(pk)
        inv.append(1.0 / denom)
    out = jnp.zeros((GROUP * mq, KV_W), F32)
    for j, seg in enumerate(segs):
        vb = seg[1]
        vheads = [jnp.where(lane_head == k, vb, jnp.zeros((), BF16)) for k in range(N_KV)]
        if vb.shape[0] % LANES == 0:
            out = out + _dot(jnp.concatenate([probs[k][j] for k in range(N_KV)], axis=1),
                             jnp.concatenate(vheads, axis=0))
        else:
            for k in range(N_KV):
                out = out + _dot(probs[k][j], vheads[k])
    scale = inv[0]
    for k in range(1, N_KV):
        scale = jnp.where(lane_head == k, inv[k], scale)
    out = out * scale
    return jnp.concatenate([out[g * mq:(g + 1) * mq, :] for g in range(GROUP)], axis=1)


def _queries(x, g, wq_ref, cos, sin):
    q = _dot(_rms(x, g).astype(BF16), wq_ref[...])
    return (_rope(q, cos, sin) * (HEAD_DIM ** -0.5)).astype(BF16)


def _attn_main_kernel(sinks_ref, xm_ref, g_ref, wq_ref, wo_ref, cos_ref, sin_ref,
                      km_ref, vm_ref, kt_ref, vt_ref,
                      om_ref, q_ref, ao_ref, kbuf_ref, vbuf_ref):
    i = pl.program_id(0)
    start = i % TILES_PER_SEQ == 0
    x = xm_ref[...]
    q_ref[...] = _queries(x, g_ref[...], wq_ref, cos_ref[...], sin_ref[...])

    @pl.when(start)
    def _():
        pad = jnp.zeros((BLK - N_META, KV_W), BF16)
        kbuf_ref[0:BLK - N_META, :] = pad
        vbuf_ref[0:BLK - N_META, :] = pad
        kbuf_ref[BLK - N_META:BLK, :] = kt_ref[N_SAMPLE:N_TAIL, :].astype(BF16)
        vbuf_ref[BLK - N_META:BLK, :] = vt_ref[N_SAMPLE:N_TAIL, :].astype(BF16)

    @pl.when(jnp.logical_not(start))
    def _():
        kbuf_ref[0:BLK, :] = kbuf_ref[TM:TM + BLK, :]
        vbuf_ref[0:BLK, :] = vbuf_ref[TM:TM + BLK, :]

    kbuf_ref[BLK:BLK + TM, :] = km_ref[...].astype(BF16)
    vbuf_ref[BLK:BLK + TM, :] = vm_ref[...].astype(BF16)

    qi = lax.broadcasted_iota(jnp.int32, (BLK, 2 * BLK), 0)
    kj = lax.broadcasted_iota(jnp.int32, (BLK, 2 * BLK), 1)
    band = (kj >= qi) & (kj <= qi + WINDOW)
    first_key = jnp.where(start, BLK - N_META, 0)
    for blk in range(BLKS_PER_TILE):
        lo = blk * BLK
        mask = band & (kj >= first_key) if blk == 0 else band
        att = _attend(q_ref[lo:lo + BLK, :],
                      [(kbuf_ref[lo:lo + 2 * BLK, :], vbuf_ref[lo:lo + 2 * BLK, :], mask, False)],
                      sinks_ref)
        ao_ref[lo:lo + BLK, :] = att.astype(BF16)
    om_ref[...] = x + _dot(ao_ref[...], wo_ref[...])


def _attn_main_call(sinks, xm, g, wq, wo, cos_m, sin_m, km, vm, kt, vt):
    tile = lambda w: pl.BlockSpec((TM, w), lambda i: (i, 0))
    pos = pl.BlockSpec((TM, LANES), lambda i: (i % TILES_PER_SEQ, 0))
    return pl.pallas_call(
        _attn_main_kernel,
        grid=(N_TILES,),
        in_specs=[pl.BlockSpec(memory_space=pltpu.SMEM),
                  tile(D_MODEL), _resident((1, D_MODEL)),
                  _resident((D_MODEL, D_MODEL)), _resident((D_MODEL, D_MODEL)),
                  pos, pos, tile(KV_W), tile(KV_W),
                  _resident((N_TAIL, KV_W)), _resident((N_TAIL, KV_W))],
        out_specs=tile(D_MODEL),
        out_shape=jax.ShapeDtypeStruct((N_MAIN, D_MODEL), F32),
        scratch_shapes=[pltpu.VMEM((TM, D_MODEL), BF16), pltpu.VMEM((TM, D_MODEL), BF16),
                        pltpu.VMEM((BLK + TM, KV_W), BF16), pltpu.VMEM((BLK + TM, KV_W), BF16)],
        compiler_params=_params(),
        name="attn_main",
    )(sinks, xm, g.reshape(1, D_MODEL), wq, wo, cos_m, sin_m, km, vm, kt, vt)


def _attn_tail_kernel(sinks_ref, xt_ref, g_ref, wq_ref, wo_ref, cos_ref, sin_ref,
                      kt_ref, vt_ref, ck_ref, cv_ref, ot_ref, nk_ref, nv_ref, q_ref, ao_ref):
    s = pl.program_id(0)

    @pl.when(s == 0)
    def _():
        q_ref[...] = _queries(xt_ref[...], g_ref[...], wq_ref, cos_ref[...], sin_ref[...])
        qi = lax.broadcasted_iota(jnp.int32, (N_META, N_META), 0)
        kj = lax.broadcasted_iota(jnp.int32, (N_META, N_META), 1)
        km = kt_ref[N_SAMPLE:N_TAIL, :].astype(BF16)
        vm = vt_ref[N_SAMPLE:N_TAIL, :].astype(BF16)
        att = _attend(q_ref[N_SAMPLE:N_TAIL, :], [(km, vm, kj <= qi, False)], sinks_ref)
        ao_ref[N_SAMPLE:N_TAIL, :] = att.astype(BF16)

    rows = pl.ds(pl.multiple_of(s * GROUP_ROWS, GROUP_ROWS), GROUP_ROWS)
    k_new = kt_ref[rows, :]
    v_new = vt_ref[rows, :]
    ck = [ck_ref[b].reshape(KV_W, WINDOW) for b in range(SAMPLE_GROUP)]
    cv = [cv_ref[b].reshape(KV_W, WINDOW) for b in range(SAMPLE_GROUP)]
    k_cached_t = jnp.concatenate(ck, axis=1).astype(BF16)
    v_cached = jnp.concatenate([c.T for c in cv], axis=0).astype(BF16)

    n_cached = SAMPLE_GROUP * WINDOW
    qr = lax.broadcasted_iota(jnp.int32, (GROUP_ROWS, n_cached), 0)
    kc = lax.broadcasted_iota(jnp.int32, (GROUP_ROWS, n_cached), 1)
    mask_cached = ((kc // WINDOW) == (qr // DEC_SEQ)) & ((kc % WINDOW) >= (qr % DEC_SEQ))
    qn = lax.broadcasted_iota(jnp.int32, (GROUP_ROWS, GROUP_ROWS), 0)
    kn = lax.broadcasted_iota(jnp.int32, (GROUP_ROWS, GROUP_ROWS), 1)
    mask_new = ((kn // DEC_SEQ) == (qn // DEC_SEQ)) & ((kn % DEC_SEQ) <= (qn % DEC_SEQ))
    att = _attend(q_ref[rows, :],
                  [(k_cached_t, v_cached, mask_cached, True),
                   (k_new.astype(BF16), v_new.astype(BF16), mask_new, False)],
                  sinks_ref)
    ao_ref[rows, :] = att.astype(BF16)

    lane = lax.broadcasted_iota(jnp.int32, (1, WINDOW), 1)
    is_new = lane >= WINDOW - DEC_SEQ
    fill = jnp.zeros((LANES - GROUP_ROWS, KV_W), F32)
    for new, cached, out_ref in ((k_new, ck, nk_ref), (v_new, cv, nv_ref)):
        new_t = jnp.concatenate([new, fill], axis=0).T
        for b in range(SAMPLE_GROUP):
            shifted = pltpu.roll(cached[b], WINDOW - DEC_SEQ, 1)
            placed = pltpu.roll(new_t, (WINDOW - DEC_SEQ - b * DEC_SEQ) % LANES, 1)
            out_ref[b] = jnp.where(is_new, placed, shifted).reshape(N_KV, HEAD_DIM, WINDOW)

    @pl.when(s == N_GROUPS - 1)
    def _():
        ot_ref[...] = xt_ref[...] + _dot(ao_ref[...], wo_ref[...])


def _attn_tail_call(sinks, xt, g, wq, wo, cos_t, sin_t, kt, vt, ck, cv):
    cached = pl.BlockSpec((SAMPLE_GROUP, N_KV, HEAD_DIM, WINDOW), lambda s: (s, 0, 0, 0))
    cache_shape = jax.ShapeDtypeStruct((DEC_BATCH, N_KV, HEAD_DIM, WINDOW), F32)
    return pl.pallas_call(
        _attn_tail_kernel,
        grid=(N_GROUPS,),
        in_specs=[pl.BlockSpec(memory_space=pltpu.SMEM),
                  _resident((N_TAIL, D_MODEL)), _resident((1, D_MODEL)),
                  _resident((D_MODEL, D_MODEL)), _resident((D_MODEL, D_MODEL)),
                  _resident((N_TAIL, LANES)), _resident((N_TAIL, LANES)),
                  _resident((N_TAIL, KV_W)), _resident((N_TAIL, KV_W)),
                  cached, cached],
        out_specs=[pl.BlockSpec((N_TAIL, D_MODEL), lambda s: (0, 0)), cached, cached],
        out_shape=[jax.ShapeDtypeStruct((N_TAIL, D_MODEL), F32), cache_shape, cache_shape],
        scratch_shapes=[pltpu.VMEM((N_TAIL, D_MODEL), BF16), pltpu.VMEM((N_TAIL, D_MODEL), BF16)],
        compiler_params=_params(),
        name="attn_tail",
    )(sinks, xt, g.reshape(1, D_MODEL), wq, wo, cos_t, sin_t, kt, vt, ck, cv)


def _rope_tables(pos):
    inv = 1.0 / (ROPE_THETA ** (jnp.arange(0, HEAD_DIM, 2, dtype=F32) / HEAD_DIM))
    ang = pos.astype(F32)[:, None] * inv[None, :]
    cos, sin = jnp.cos(ang), jnp.sin(ang)
    reps = LANES // HEAD_DIM
    return (jnp.tile(jnp.concatenate([cos, cos], axis=1), (1, reps)),
            jnp.tile(jnp.concatenate([-sin, sin], axis=1), (1, reps)))


def kernel(x_prompt, x_sample, state_conv, cache_k, cache_v, meta_tokens, norm_g, ffn_w_in,
           ffn_w_out, conv_w_in, conv_kernel, conv_w_out, kv_norm_g, w_kv, w_q, w_o, sinks,
           final_norm_g):
    assert x_prompt.shape == (BATCH, SEQ, D_MODEL) and x_sample.shape == (DEC_BATCH, DEC_SEQ, D_MODEL)
    assert cache_k.shape == (DEC_BATCH, WINDOW, N_KV, HEAD_DIM)
    assert norm_g.shape[0] == 2 and conv_w_in.shape[0] == 1 and w_q.shape[0] == 1

    xm = x_prompt.reshape(N_MAIN, D_MODEL)
    xt = jnp.concatenate([x_sample.reshape(N_SAMPLE, D_MODEL), meta_tokens], axis=0)

    wq = w_q[0].reshape(D_MODEL, N_KV, GROUP, HEAD_DIM).transpose(0, 2, 1, 3)
    wq = wq.reshape(D_MODEL, D_MODEL).astype(BF16)
    wo = w_o[0].reshape(N_KV, GROUP, HEAD_DIM, D_MODEL).transpose(1, 0, 2, 3)
    wo = wo.reshape(D_MODEL, D_MODEL).astype(BF16)

    pos_main = N_META + jnp.arange(SEQ, dtype=jnp.int32)
    pos_tail = jnp.concatenate([
        jnp.tile(PAST_LEN + jnp.arange(DEC_SEQ, dtype=jnp.int32), DEC_BATCH),
        jnp.arange(N_META, dtype=jnp.int32)])
    cos_m, sin_m = _rope_tables(pos_main)
    cos_t, sin_t = _rope_tables(pos_tail)

    sc = state_conv[0]
    zeros = jnp.zeros((DEC_BATCH, DEC_SEQ, D_MODEL), F32)
    s1 = zeros.at[:, 0].set(sc[:, 1])
    s2 = zeros.at[:, 0].set(sc[:, 0]).at[:, 1].set(sc[:, 1])
    pad_meta = lambda a: jnp.pad(a.reshape(N_SAMPLE, D_MODEL), ((0, N_META), (0, 0)))
    s1, s2 = pad_meta(s1), pad_meta(s2)

    ffn_casts = lambda l, j: [(ffn_w_in, (l, j), N_TILES), (ffn_w_out, (l, j), N_TILES // 2)]

    xm, xt, cw_in, cw_out, wkv, w_in, w_out = _ffn_call(
        xm, xt, norm_g[0, 0], ffn_w_in[0, 0].astype(BF16), ffn_w_out[0, 0].astype(BF16),
        name="ffn_l0a",
        casts=[(conv_w_in, (0,), N_TILES), (conv_w_out, (0,), N_TILES), (w_kv, (), N_TILES)]
        + ffn_casts(0, 1))
    xm, xt, u_tail, sp = _conv_call(xm, xt, norm_g[0, 1], cw_in, conv_kernel[0], cw_out, s1, s2)
    xm, xt, km, vm, kt, vt, k_last, v_last, w_in, w_out = _ffn_call(
        xm, xt, norm_g[0, 2], w_in, w_out, name="ffn_l0b_kv",
        kv=(kv_norm_g, wkv, cos_m, sin_m, cos_t, sin_t), casts=ffn_casts(1, 0))

    xm, xt, w_in, w_out = _ffn_call(xm, xt, norm_g[1, 0], w_in, w_out, name="ffn_l1a",
                                    casts=ffn_casts(1, 1))
    slot_minor = lambda a: a.transpose(0, 2, 3, 1)
    xt, nk, nv = _attn_tail_call(sinks[0], xt, norm_g[1, 1], wq, wo, cos_t, sin_t, kt, vt,
                                 slot_minor(cache_k), slot_minor(cache_v))
    xm = _attn_main_call(sinks[0], xm, norm_g[1, 1], wq, wo, cos_m, sin_m, km, vm, kt, vt)
    ym, yt = _ffn_call(xm, xt, norm_g[1, 2], w_in, w_out, name="ffn_l1b_final",
                       final_g=final_norm_g)

    y_prompt = ym.reshape(BATCH, SEQ, D_MODEL)
    y_sample = yt[:N_SAMPLE].reshape(DEC_BATCH, DEC_SEQ, D_MODEL)
    new_state_conv_p = sp[:, CARRY - (CONV_W - 1):][None]
    new_state_conv_s = u_tail[:N_SAMPLE].reshape(DEC_BATCH, DEC_SEQ, D_MODEL)[:, DEC_SEQ - (CONV_W - 1):][None]
    slot_major = lambda a: a.transpose(0, 3, 1, 2)
    new_k_p = slot_major(k_last.reshape(BATCH, N_KV, HEAD_DIM, WINDOW))
    new_v_p = slot_major(v_last.reshape(BATCH, N_KV, HEAD_DIM, WINDOW))
    return (y_prompt, y_sample, new_state_conv_p, new_state_conv_s,
            new_k_p, new_v_p, slot_major(nk), slot_major(nv))
```

```python
import jax
import jax.numpy as jnp
from jax import lax
from jax.experimental import pallas as pl
from jax.experimental.pallas import tpu as pltpu

D_MODEL = 1024
BATCH = 2
SEQ = 8192
DEC_BATCH = 128
DEC_SEQ = 4
PAST_LEN = 8192
N_META = 16
D_FF = 2816
CONV_W = 3
HEAD_DIM = 64
N_HEADS = 16
N_KV = 4
GROUP = 4
WINDOW = 128
ROPE_THETA = 10000.0
EPS = 1e-6
NEG = -1e30

KV_W = N_KV * HEAD_DIM
N_MAIN = BATCH * SEQ
N_SAMPLE = DEC_BATCH * DEC_SEQ
N_TAIL = N_SAMPLE + N_META
TM = 512
N_TILES = N_MAIN // TM
TILES_PER_SEQ = SEQ // TM
FF_CHUNK = 256
BLK = WINDOW
BLKS_PER_TILE = TM // BLK
SAMPLE_GROUP = 8
N_GROUPS = DEC_BATCH // SAMPLE_GROUP
CACHE_SEQS = DEC_BATCH // N_TILES
GROUP_ROWS = SAMPLE_GROUP * DEC_SEQ
LANES = 128
BF16_ROWS = 16
V7X_VMEM_LIMIT = 56 * 1024 * 1024

F32 = jnp.float32
BF16 = jnp.bfloat16


def _rms(x, g):
    return x * lax.rsqrt(jnp.mean(x * x, axis=-1, keepdims=True) + EPS) * g


def _rope(x, cos, sin_signed):
    lane = lax.broadcasted_iota(jnp.int32, (1, LANES), 1)
    first_half = (lane & (HEAD_DIM - 1)) < (HEAD_DIM // 2)
    outs = []
    for c in range(x.shape[1] // LANES):
        xc = x[:, c * LANES:(c + 1) * LANES]
        partner = jnp.where(first_half,
                            pltpu.roll(xc, LANES - HEAD_DIM // 2, 1),
                            pltpu.roll(xc, HEAD_DIM // 2, 1))
        outs.append(xc * cos + partner * sin_signed)
    return jnp.concatenate(outs, axis=1)


def _dot(a, b):
    return jnp.dot(a, b, preferred_element_type=F32)


def _dot_nt(a, b):
    return lax.dot_general(a, b, (((1,), (1,)), ((), ())), preferred_element_type=F32)


def _resident(shape):
    nd = len(shape)
    return pl.BlockSpec(shape, lambda i: (0,) * nd, pipeline_mode=pl.Buffered(1))


def _main_tile(i):
    return jnp.maximum(i - 1, 0)


def _main_spec(width):
    return pl.BlockSpec((TM, width), lambda i: (_main_tile(i), 0))


def _main_pos_spec():
    return pl.BlockSpec((TM, LANES), lambda i: (_main_tile(i) % TILES_PER_SEQ, 0))


def _params():
    return pltpu.CompilerParams(dimension_semantics=("arbitrary",),
                                vmem_limit_bytes=V7X_VMEM_LIMIT)


def _ffn_rows(x, g, w_in_ref, w_out_ref, a_ref):
    m = x.shape[0]
    hn = _rms(x, g).astype(BF16)
    for c in range(D_FF // FF_CHUNK):
        lo = c * FF_CHUNK
        gate = _dot(hn, w_in_ref[:, lo:lo + FF_CHUNK])
        up = _dot(hn, w_in_ref[:, D_FF + lo:D_FF + lo + FF_CHUNK])
        a_ref[0:m, lo:lo + FF_CHUNK] = (gate * jax.nn.sigmoid(gate) * up).astype(BF16)
    y = _dot(a_ref[0:m, :], w_out_ref[...])
    return x + 0.5 * y


def _make_ffn_kernel(has_final, has_kv, n_cast):
    def body(*refs):
        refs = list(refs)
        xm_ref, xt_ref, g_ref, w_in_ref, w_out_ref = refs[:5]
        pos = 5
        if has_final:
            gf_ref = refs[pos]
            pos += 1
        if has_kv:
            gkv_ref, wkv_ref, cm_ref, sm_ref, ct_ref, st_ref, ck_ref, cv_ref = refs[pos:pos + 8]
            pos += 8
        cast_src = refs[pos:pos + n_cast]
        pos += n_cast
        om_ref, ot_ref = refs[pos:pos + 2]
        pos += 2
        if has_kv:
            km_ref, vm_ref, kt_ref, vt_ref, kl_ref, vl_ref, nk_ref, nv_ref = refs[pos:pos + 8]
            pos += 8
        cast_dst = refs[pos:pos + n_cast]
        pos += n_cast
        a_ref = refs[pos]
        if has_kv:
            newk_ref, newv_ref = refs[pos + 1:pos + 3]
        i = pl.program_id(0)

        for src, dst in zip(cast_src, cast_dst):
            dst[...] = src[...].astype(BF16)

        def rows(x_ref, o_ref):
            y = _ffn_rows(x_ref[...], g_ref[...], w_in_ref, w_out_ref, a_ref)
            o_ref[...] = _rms(y, gf_ref[...]) if has_final else y

        def project_kv(x_ref, cos_ref, sin_ref, k_ref, v_ref):
            kv = _dot(_rms(x_ref[...], gkv_ref[...]).astype(BF16), wkv_ref[...])
            k = _rope(kv[:, :KV_W], cos_ref[...], sin_ref[...])
            v = kv[:, KV_W:]
            k_ref[...] = k
            v_ref[...] = v
            return k, v

        @pl.when(i == 0)
        def _():
            rows(xt_ref, ot_ref)
            if has_kv:
                k, v = project_kv(xt_ref, ct_ref, st_ref, kt_ref, vt_ref)
                for c in range(N_SAMPLE // LANES):
                    newk_ref[c] = k[c * LANES:(c + 1) * LANES, :].T
                    newv_ref[c] = v[c * LANES:(c + 1) * LANES, :].T

        @pl.when(i > 0)
        def _():
            rows(xm_ref, om_ref)
            if has_kv:
                t = i - 1
                k, v = project_kv(xm_ref, cm_ref, sm_ref, km_ref, vm_ref)

                @pl.when(t % TILES_PER_SEQ == TILES_PER_SEQ - 1)
                def _():
                    kl_ref[...] = k[TM - WINDOW:, :].T
                    vl_ref[...] = v[TM - WINDOW:, :].T

                lane = lax.broadcasted_iota(jnp.int32, (1, WINDOW), 1)
                is_new = lane >= WINDOW - DEC_SEQ
                first_lane = (t * CACHE_SEQS * DEC_SEQ) % LANES
                for new_ref, c_ref, n_ref in ((newk_ref, ck_ref, nk_ref), (newv_ref, cv_ref, nv_ref)):
                    new_t = new_ref[(t * CACHE_SEQS * DEC_SEQ) // LANES]
                    for b in range(CACHE_SEQS):
                        shifted = pltpu.roll(c_ref[b].reshape(KV_W, WINDOW), WINDOW - DEC_SEQ, 1)
                        amount = (WINDOW - DEC_SEQ - b * DEC_SEQ - first_lane) & (LANES - 1)
                        placed = pltpu.roll(new_t, amount, 1)
                        n_ref[b] = jnp.where(is_new, placed, shifted).reshape(N_KV, HEAD_DIM, WINDOW)

    return body


def _cast_specs(casts):
    in_specs, out_specs, out_shape, args = [], [], [], []
    for arr, lead, steps in casts:
        n_rows, n_cols = arr.shape[-2:]
        rows = n_rows // steps
        assert rows * steps == n_rows and rows % BF16_ROWS == 0
        blk = lambda i, steps=steps: jnp.minimum(_main_tile(i), steps - 1)
        in_specs.append(pl.BlockSpec((None,) * len(lead) + (rows, n_cols),
                                     lambda i, lead=lead, blk=blk: lead + (blk(i), 0)))
        out_specs.append(pl.BlockSpec((rows, n_cols), lambda i, blk=blk: (blk(i), 0)))
        out_shape.append(jax.ShapeDtypeStruct((n_rows, n_cols), BF16))
        args.append(arr)
    return in_specs, out_specs, out_shape, args


def _ffn_call(xm, xt, g, w_in, w_out, *, name, final_g=None, kv=None, casts=()):
    in_specs = [_main_spec(D_MODEL), _resident((N_TAIL, D_MODEL)), _resident((1, D_MODEL)),
                _resident((D_MODEL, 2 * D_FF)), _resident((D_FF, D_MODEL))]
    args = [xm, xt, g.reshape(1, D_MODEL), w_in, w_out]
    out_shape = [jax.ShapeDtypeStruct((N_MAIN, D_MODEL), F32),
                 jax.ShapeDtypeStruct((N_TAIL, D_MODEL), F32)]
    out_specs = [_main_spec(D_MODEL), pl.BlockSpec((N_TAIL, D_MODEL), lambda i: (0, 0))]
    if final_g is not None:
        in_specs.append(_resident((1, D_MODEL)))
        args.append(final_g.reshape(1, D_MODEL))
    scratch = [pltpu.VMEM((N_TAIL, D_FF), BF16)]
    if kv is not None:
        g_kv, w_kv, cos_m, sin_m, cos_t, sin_t, ck, cv = kv
        cached = pl.BlockSpec((CACHE_SEQS, N_KV, HEAD_DIM, WINDOW), lambda i: (_main_tile(i), 0, 0, 0))
        in_specs += [_resident((1, D_MODEL)), _resident((D_MODEL, 2 * KV_W)),
                     _main_pos_spec(), _main_pos_spec(),
                     _resident((N_TAIL, LANES)), _resident((N_TAIL, LANES)), cached, cached]
        args += [g_kv.reshape(1, D_MODEL), w_kv, cos_m, sin_m, cos_t, sin_t, ck, cv]
        out_shape += [jax.ShapeDtypeStruct((N_MAIN, KV_W), F32)] * 2
        out_shape += [jax.ShapeDtypeStruct((N_TAIL, KV_W), F32)] * 2
        out_shape += [jax.ShapeDtypeStruct((BATCH, KV_W, WINDOW), F32)] * 2
        out_shape += [jax.ShapeDtypeStruct((DEC_BATCH, N_KV, HEAD_DIM, WINDOW), F32)] * 2
        last = pl.BlockSpec((None, KV_W, WINDOW), lambda i: (_main_tile(i) // TILES_PER_SEQ, 0, 0))
        out_specs += [_main_spec(KV_W), _main_spec(KV_W),
                      pl.BlockSpec((N_TAIL, KV_W), lambda i: (0, 0)),
                      pl.BlockSpec((N_TAIL, KV_W), lambda i: (0, 0)), last, last, cached, cached]
        scratch += [pltpu.VMEM((N_SAMPLE // LANES, KV_W, LANES), F32)] * 2
    c_in, c_out, c_shape, c_args = _cast_specs(casts)
    return pl.pallas_call(
        _make_ffn_kernel(final_g is not None, kv is not None, len(casts)),
        grid=(1 + N_TILES,),
        in_specs=in_specs + c_in,
        out_specs=out_specs + c_out,
        out_shape=out_shape + c_shape,
        scratch_shapes=scratch,
        compiler_params=_params(),
        name=name,
    )(*args, *c_args)


CARRY = 8


def _conv_rows(x, g, w_in_ref, kern_ref, w_out_ref, ubuf_ref, fix):
    m = x.shape[0]
    bcz = _dot(_rms(x, g).astype(BF16), w_in_ref[...])
    b = bcz[:, :D_MODEL]
    u = bcz[:, D_MODEL:2 * D_MODEL] * bcz[:, 2 * D_MODEL:]
    ubuf_ref[CARRY:CARRY + m, :] = u
    prev1 = ubuf_ref[CARRY - 1:CARRY - 1 + m, :]
    prev2 = ubuf_ref[CARRY - 2:CARRY - 2 + m, :]
    if fix is not None:
        prev1, prev2 = fix(prev1, prev2)
    kern = kern_ref[...]
    conv = kern[0:1, :] * prev2 + kern[1:2, :] * prev1 + kern[2:3, :] * u
    return x + _dot((b * conv).astype(BF16), w_out_ref[...])


def _conv_kernel(xm_ref, xt_ref, g_ref, w_in_ref, kern_ref, w_out_ref, s1_ref, s2_ref,
                 om_ref, ot_ref, ut_ref, sp_ref, ubuf_ref, meta_ref):
    i = pl.program_id(0)

    @pl.when(i == 0)
    def _():
        ubuf_ref[0:CARRY, :] = jnp.zeros((CARRY, D_MODEL), F32)

        def fix(prev1, prev2):
            step = lax.broadcasted_iota(jnp.int32, (N_SAMPLE, 1), 0) & (DEC_SEQ - 1)
            meta = lax.broadcasted_iota(jnp.int32, (N_META, 1), 0)
            p1 = [jnp.where(step >= 1, prev1[:N_SAMPLE], s1_ref[...]),
                  jnp.where(meta >= 1, prev1[N_SAMPLE:], 0.0)]
            p2 = [jnp.where(step >= 2, prev2[:N_SAMPLE], s2_ref[...]),
                  jnp.where(meta >= 2, prev2[N_SAMPLE:], 0.0)]
            return jnp.concatenate(p1, axis=0), jnp.concatenate(p2, axis=0)

        ot_ref[...] = _conv_rows(xt_ref[...], g_ref[...], w_in_ref, kern_ref, w_out_ref,
                                 ubuf_ref, fix)
        ut_ref[...] = ubuf_ref[CARRY:CARRY + N_TAIL, :]
        meta_ref[...] = ubuf_ref[N_TAIL:N_TAIL + CARRY, :]

    @pl.when(i > 0)
    def _():
        t = i - 1

        @pl.when(t % TILES_PER_SEQ == 0)
        def _():
            ubuf_ref[0:CARRY, :] = meta_ref[...]

        om_ref[...] = _conv_rows(xm_ref[...], g_ref[...], w_in_ref, kern_ref, w_out_ref,
                                 ubuf_ref, None)
        last = ubuf_ref[TM:TM + CARRY, :]
        ubuf_ref[0:CARRY, :] = last

        @pl.when(t % TILES_PER_SEQ == TILES_PER_SEQ - 1)
        def _():
            sp_ref[t // TILES_PER_SEQ] = last


def _conv_call(xm, xt, g, w_in, kern, w_out, s1, s2):
    return pl.pallas_call(
        _conv_kernel,
        grid=(1 + N_TILES,),
        in_specs=[_main_spec(D_MODEL), _resident((N_TAIL, D_MODEL)), _resident((1, D_MODEL)),
                  _resident((D_MODEL, 3 * D_MODEL)), _resident((CONV_W, D_MODEL)),
                  _resident((D_MODEL, D_MODEL)),
                  _resident((N_SAMPLE, D_MODEL)), _resident((N_SAMPLE, D_MODEL))],
        out_specs=[_main_spec(D_MODEL),
                   pl.BlockSpec((N_TAIL, D_MODEL), lambda i: (0, 0)),
                   pl.BlockSpec((N_TAIL, D_MODEL), lambda i: (0, 0)),
                   pl.BlockSpec((BATCH, CARRY, D_MODEL), lambda i: (0, 0, 0))],
        out_shape=[jax.ShapeDtypeStruct((N_MAIN, D_MODEL), F32),
                   jax.ShapeDtypeStruct((N_TAIL, D_MODEL), F32),
                   jax.ShapeDtypeStruct((N_TAIL, D_MODEL), F32),
                   jax.ShapeDtypeStruct((BATCH, CARRY, D_MODEL), F32)],
        scratch_shapes=[pltpu.VMEM((CARRY + N_TAIL, D_MODEL), F32),
                        pltpu.VMEM((CARRY, D_MODEL), F32)],
        compiler_params=_params(),
        name="conv_mixer",
    )(xm, xt, g.reshape(1, D_MODEL), w_in, kern, w_out, s1, s2)


def _attend(qp, segs, sinks_ref):
    mq = qp.shape[0]
    lane_head = lax.broadcasted_iota(jnp.int32, (1, KV_W), 1) // HEAD_DIM
    row_group = lax.broadcasted_iota(jnp.int32, (GROUP * mq, 1), 0) // mq
    masks = [jnp.concatenate([seg[2]] * GROUP, axis=0) for seg in segs]
    out = None
    for k in range(N_KV):
        head = lane_head == k
        qs = jnp.concatenate(
            [jnp.where(head, qp[:, g * KV_W:(g + 1) * KV_W], jnp.zeros((), BF16))
             for g in range(GROUP)], axis=0)
        sink = jnp.full((GROUP * mq, 1), sinks_ref[k * GROUP], F32)
        for g in range(1, GROUP):
            sink = jnp.where(row_group == g, sinks_ref[k * GROUP + g], sink)
        scores = []
        for (kb, _, _, k_transposed), msk in zip(segs, masks):
            s = _dot(qs, kb) if k_transposed else _dot_nt(qs, kb)
            scores.append(jnp.where(msk, s, NEG))
        mx = sink
        for s in scores:
            mx = jnp.maximum(mx, jnp.max(s, axis=-1, keepdims=True))
        denom = jnp.exp(sink - mx)
        acc = None
        for s, (_, vb, _, _) in zip(scores, segs):
            p = jnp.exp(s - mx)
            denom = denom + jnp.sum(p, axis=-1, keepdims=True)
            pv = _dot(p.astype(BF16), jnp.where(head, vb, jnp.zeros((), BF16)))
            acc = pv if acc is None else acc + pv
        acc = acc * (1.0 / denom)
        out = acc if out is None else out + acc
    return jnp.concatenate([out[g * mq:(g + 1) * mq, :] for g in range(GROUP)], axis=1)


def _queries(x, g, wq_ref, cos, sin):
    q = _dot(_rms(x, g).astype(BF16), wq_ref[...])
    return (_rope(q, cos, sin) * (HEAD_DIM ** -0.5)).astype(BF16)


def _attn_main_kernel(sinks_ref, xm_ref, g_ref, wq_ref, wo_ref, cos_ref, sin_ref,
                      km_ref, vm_ref, kt_ref, vt_ref,
                      om_ref, q_ref, ao_ref, kbuf_ref, vbuf_ref):
    i = pl.program_id(0)
    start = i % TILES_PER_SEQ == 0
    x = xm_ref[...]
    q_ref[...] = _queries(x, g_ref[...], wq_ref, cos_ref[...], sin_ref[...])

    @pl.when(start)
    def _():
        pad = jnp.zeros((BLK - N_META, KV_W), BF16)
        kbuf_ref[0:BLK - N_META, :] = pad
        vbuf_ref[0:BLK - N_META, :] = pad
        kbuf_ref[BLK - N_META:BLK, :] = kt_ref[N_SAMPLE:N_TAIL, :].astype(BF16)
        vbuf_ref[BLK - N_META:BLK, :] = vt_ref[N_SAMPLE:N_TAIL, :].astype(BF16)

    @pl.when(jnp.logical_not(start))
    def _():
        kbuf_ref[0:BLK, :] = kbuf_ref[TM:TM + BLK, :]
        vbuf_ref[0:BLK, :] = vbuf_ref[TM:TM + BLK, :]

    kbuf_ref[BLK:BLK + TM, :] = km_ref[...].astype(BF16)
    vbuf_ref[BLK:BLK + TM, :] = vm_ref[...].astype(BF16)

    qi = lax.broadcasted_iota(jnp.int32, (BLK, 2 * BLK), 0)
    kj = lax.broadcasted_iota(jnp.int32, (BLK, 2 * BLK), 1)
    band = (kj >= qi) & (kj <= qi + WINDOW)
    first_key = jnp.where(start, BLK - N_META, 0)
    for blk in range(BLKS_PER_TILE):
        lo = blk * BLK
        mask = band & (kj >= first_key) if blk == 0 else band
        att = _attend(q_ref[lo:lo + BLK, :],
                      [(kbuf_ref[lo:lo + 2 * BLK, :], vbuf_ref[lo:lo + 2 * BLK, :], mask, False)],
                      sinks_ref)
        ao_ref[lo:lo + BLK, :] = att.astype(BF16)
    om_ref[...] = x + _dot(ao_ref[...], wo_ref[...])


def _attn_main_call(sinks, xm, g, wq, wo, cos_m, sin_m, km, vm, kt, vt):
    tile = lambda w: pl.BlockSpec((TM, w), lambda i: (i, 0))
    pos = pl.BlockSpec((TM, LANES), lambda i: (i % TILES_PER_SEQ, 0))
    return pl.pallas_call(
        _attn_main_kernel,
        grid=(N_TILES,),
        in_specs=[pl.BlockSpec(memory_space=pltpu.SMEM),
                  tile(D_MODEL), _resident((1, D_MODEL)),
                  _resident((D_MODEL, D_MODEL)), _resident((D_MODEL, D_MODEL)),
                  pos, pos, tile(KV_W), tile(KV_W),
                  _resident((N_TAIL, KV_W)), _resident((N_TAIL, KV_W))],
        out_specs=tile(D_MODEL),
        out_shape=jax.ShapeDtypeStruct((N_MAIN, D_MODEL), F32),
        scratch_shapes=[pltpu.VMEM((TM, D_MODEL), BF16), pltpu.VMEM((TM, D_MODEL), BF16),
                        pltpu.VMEM((BLK + TM, KV_W), BF16), pltpu.VMEM((BLK + TM, KV_W), BF16)],
        compiler_params=_params(),
        name="attn_main",
    )(sinks, xm, g.reshape(1, D_MODEL), wq, wo, cos_m, sin_m, km, vm, kt, vt)


def _attn_tail_kernel(sinks_ref, xt_ref, g_ref, wq_ref, wo_ref, cos_ref, sin_ref,
                      kt_ref, vt_ref, ck_ref, cv_ref, ot_ref, q_ref, ao_ref):
    s = pl.program_id(0)

    @pl.when(s == 0)
    def _():
        q_ref[...] = _queries(xt_ref[...], g_ref[...], wq_ref, cos_ref[...], sin_ref[...])
        qi = lax.broadcasted_iota(jnp.int32, (N_META, N_META), 0)
        kj = lax.broadcasted_iota(jnp.int32, (N_META, N_META), 1)
        km = kt_ref[N_SAMPLE:N_TAIL, :].astype(BF16)
        vm = vt_ref[N_SAMPLE:N_TAIL, :].astype(BF16)
        att = _attend(q_ref[N_SAMPLE:N_TAIL, :], [(km, vm, kj <= qi, False)], sinks_ref)
        ao_ref[N_SAMPLE:N_TAIL, :] = att.astype(BF16)

    rows = pl.ds(pl.multiple_of(s * GROUP_ROWS, GROUP_ROWS), GROUP_ROWS)
    k_new = kt_ref[rows, :]
    v_new = vt_ref[rows, :]
    ck = [ck_ref[b].reshape(KV_W, WINDOW) for b in range(SAMPLE_GROUP)]
    cv = [cv_ref[b].reshape(KV_W, WINDOW) for b in range(SAMPLE_GROUP)]
    k_cached_t = jnp.concatenate(ck, axis=1).astype(BF16)
    v_cached = jnp.concatenate([c.T for c in cv], axis=0).astype(BF16)

    n_cached = SAMPLE_GROUP * WINDOW
    qr = lax.broadcasted_iota(jnp.int32, (GROUP_ROWS, n_cached), 0)
    kc = lax.broadcasted_iota(jnp.int32, (GROUP_ROWS, n_cached), 1)
    mask_cached = ((kc // WINDOW) == (qr // DEC_SEQ)) & ((kc % WINDOW) >= (qr % DEC_SEQ))
    qn = lax.broadcasted_iota(jnp.int32, (GROUP_ROWS, GROUP_ROWS), 0)
    kn = lax.broadcasted_iota(jnp.int32, (GROUP_ROWS, GROUP_ROWS), 1)
    mask_new = ((kn // DEC_SEQ) == (qn // DEC_SEQ)) & ((kn % DEC_SEQ) <= (qn % DEC_SEQ))
    att = _attend(q_ref[rows, :],
                  [(k_cached_t, v_cached, mask_cached, True),
                   (k_new.astype(BF16), v_new.astype(BF16), mask_new, False)],
                  sinks_ref)
    ao_ref[rows, :] = att.astype(BF16)

    @pl.when(s == N_GROUPS - 1)
    def _():
        ot_ref[...] = xt_ref[...] + _dot(ao_ref[...], wo_ref[...])


def _attn_tail_call(sinks, xt, g, wq, wo, cos_t, sin_t, kt, vt, ck, cv):
    cached = pl.BlockSpec((SAMPLE_GROUP, N_KV, HEAD_DIM, WINDOW), lambda s: (s, 0, 0, 0))
    return pl.pallas_call(
        _attn_tail_kernel,
        grid=(N_GROUPS,),
        in_specs=[pl.BlockSpec(memory_space=pltpu.SMEM),
                  _resident((N_TAIL, D_MODEL)), _resident((1, D_MODEL)),
                  _resident((D_MODEL, D_MODEL)), _resident((D_MODEL, D_MODEL)),
                  _resident((N_TAIL, LANES)), _resident((N_TAIL, LANES)),
                  _resident((N_TAIL, KV_W)), _resident((N_TAIL, KV_W)),
                  cached, cached],
        out_specs=pl.BlockSpec((N_TAIL, D_MODEL), lambda s: (0, 0)),
        out_shape=jax.ShapeDtypeStruct((N_TAIL, D_MODEL), F32),
        scratch_shapes=[pltpu.VMEM((N_TAIL, D_MODEL), BF16), pltpu.VMEM((N_TAIL, D_MODEL), BF16)],
        compiler_params=_params(),
        name="attn_tail",
    )(sinks, xt, g.reshape(1, D_MODEL), wq, wo, cos_t, sin_t, kt, vt, ck, cv)


def _rope_tables(pos):
    inv = 1.0 / (ROPE_THETA ** (jnp.arange(0, HEAD_DIM, 2, dtype=F32) / HEAD_DIM))
    ang = pos.astype(F32)[:, None] * inv[None, :]
    cos, sin = jnp.cos(ang), jnp.sin(ang)
    reps = LANES // HEAD_DIM
    return (jnp.tile(jnp.concatenate([cos, cos], axis=1), (1, reps)),
            jnp.tile(jnp.concatenate([-sin, sin], axis=1), (1, reps)))


def kernel(x_prompt, x_sample, state_conv, cache_k, cache_v, meta_tokens, norm_g, ffn_w_in,
           ffn_w_out, conv_w_in, conv_kernel, conv_w_out, kv_norm_g, w_kv, w_q, w_o, sinks,
           final_norm_g):
    assert x_prompt.shape == (BATCH, SEQ, D_MODEL) and x_sample.shape == (DEC_BATCH, DEC_SEQ, D_MODEL)
    assert cache_k.shape == (DEC_BATCH, WINDOW, N_KV, HEAD_DIM)
    assert norm_g.shape[0] == 2 and conv_w_in.shape[0] == 1 and w_q.shape[0] == 1

    xm = x_prompt.reshape(N_MAIN, D_MODEL)
    xt = jnp.concatenate([x_sample.reshape(N_SAMPLE, D_MODEL), meta_tokens], axis=0)

    wq = w_q[0].reshape(D_MODEL, N_KV, GROUP, HEAD_DIM).transpose(0, 2, 1, 3)
    wq = wq.reshape(D_MODEL, D_MODEL).astype(BF16)
    wo = w_o[0].reshape(N_KV, GROUP, HEAD_DIM, D_MODEL).transpose(1, 0, 2, 3)
    wo = wo.reshape(D_MODEL, D_MODEL).astype(BF16)

    pos_main = N_META + jnp.arange(SEQ, dtype=jnp.int32)
    pos_tail = jnp.concatenate([
        jnp.tile(PAST_LEN + jnp.arange(DEC_SEQ, dtype=jnp.int32), DEC_BATCH),
        jnp.arange(N_META, dtype=jnp.int32)])
    cos_m, sin_m = _rope_tables(pos_main)
    cos_t, sin_t = _rope_tables(pos_tail)

    sc = state_conv[0]
    grow = lambda a: jnp.pad(a, ((0, 0), (0, DEC_SEQ - a.shape[1]), (0, 0))).reshape(N_SAMPLE, D_MODEL)
    s1 = grow(sc[:, 1:])
    s2 = grow(sc)

    ffn_casts = lambda l, j: [(ffn_w_in, (l, j), N_TILES), (ffn_w_out, (l, j), N_TILES // 2)]
    slot_minor = lambda a: a.transpose(0, 2, 3, 1)
    ck, cv = slot_minor(cache_k), slot_minor(cache_v)

    xm, xt, cw_in, cw_out, w_in, w_out = _ffn_call(
        xm, xt, norm_g[0, 0], ffn_w_in[0, 0].astype(BF16), ffn_w_out[0, 0].astype(BF16),
        name="ffn_l0a",
        casts=[(conv_w_in, (0,), N_TILES), (conv_w_out, (0,), N_TILES)] + ffn_casts(0, 1))
    xm, xt, u_tail, sp = _conv_call(xm, xt, norm_g[0, 1], cw_in, conv_kernel[0], cw_out, s1, s2)
    xm, xt, wkv, w_in, w_out = _ffn_call(
        xm, xt, norm_g[0, 2], w_in, w_out, name="ffn_l0b",
        casts=[(w_kv, (), N_TILES)] + ffn_casts(1, 0))

    xm, xt, km, vm, kt, vt, k_last, v_last, nk, nv, w_in, w_out = _ffn_call(
        xm, xt, norm_g[1, 0], w_in, w_out, name="ffn_l1a_kv",
        kv=(kv_norm_g, wkv, cos_m, sin_m, cos_t, sin_t, ck, cv), casts=ffn_casts(1, 1))
    xt = _attn_tail_call(sinks[0], xt, norm_g[1, 1], wq, wo, cos_t, sin_t, kt, vt, ck, cv)
    xm = _attn_main_call(sinks[0], xm, norm_g[1, 1], wq, wo, cos_m, sin_m, km, vm, kt, vt)
    ym, yt = _ffn_call(xm, xt, norm_g[1, 2], w_in, w_out, name="ffn_l1b_final",
                       final_g=final_norm_g)

    y_prompt = ym.reshape(BATCH, SEQ, D_MODEL)
    y_sample = yt[:N_SAMPLE].reshape(DEC_BATCH, DEC_SEQ, D_MODEL)
    new_state_conv_p = sp[:, CARRY - (CONV_W - 1):][None]
    new_state_conv_s = u_tail[:N_SAMPLE].reshape(DEC_BATCH, DEC_SEQ, D_MODEL)[:, DEC_SEQ - (CONV_W - 1):][None]
    slot_major = lambda a: a.transpose(0, 3, 1, 2)
    new_k_p = slot_major(k_last.reshape(BATCH, N_KV, HEAD_DIM, WINDOW))
    new_v_p = slot_major(v_last.reshape(BATCH, N_KV, HEAD_DIM, WINDOW))
    return (y_prompt, y_sample, new_state_conv_p, new_state_conv_s,
            new_k_p, new_v_p, slot_major(nk), slot_major(nv))
```

```python
import jax
import jax.numpy as jnp
from jax import lax
from jax.experimental import pallas as pl
from jax.experimental.pallas import tpu as pltpu

D_MODEL = 1024
BATCH = 2
SEQ = 8192
DEC_BATCH = 128
DEC_SEQ = 4
PAST_LEN = 8192
N_META = 16
D_FF = 2816
CONV_W = 3
HEAD_DIM = 64
N_HEADS = 16
N_KV = 4
GROUP = 4
WINDOW = 128
ROPE_THETA = 10000.0
EPS = 1e-6
NEG = -1e30

KV_W = N_KV * HEAD_DIM
HALF_DIM = HEAD_DIM // 2
N_MAIN = BATCH * SEQ
N_SAMPLE = DEC_BATCH * DEC_SEQ
N_TAIL = N_SAMPLE + N_META
TM = 512
N_TILES = N_MAIN // TM
TILES_PER_SEQ = SEQ // TM
FF_CHUNK = 256
OUT_CHUNK = 256
BLK = WINDOW
BLKS_PER_TILE = TM // BLK
SAMPLE_GROUP = 8
N_GROUPS = DEC_BATCH // SAMPLE_GROUP
CACHE_SEQS = DEC_BATCH // N_TILES
GROUP_ROWS = SAMPLE_GROUP * DEC_SEQ
LANES = 128
BF16_ROWS = 16
V7X_VMEM_LIMIT = 56 * 1024 * 1024

F32 = jnp.float32
BF16 = jnp.bfloat16


def _rms(x, g):
    return x * lax.rsqrt(jnp.mean(x * x, axis=-1, keepdims=True) + EPS) * g


def _rope(x, cos, sin):
    outs = []
    for c in range(x.shape[1] // KV_W):
        x1 = x[:, c * KV_W:c * KV_W + LANES]
        x2 = x[:, c * KV_W + LANES:(c + 1) * KV_W]
        outs += [x1 * cos - x2 * sin, x2 * cos + x1 * sin]
    return jnp.concatenate(outs, axis=1)


def _rows_split_to_heads(t):
    return jnp.concatenate([t[half * LANES + h * HALF_DIM:half * LANES + (h + 1) * HALF_DIM]
                            for h in range(N_KV) for half in range(2)], axis=0)


def _rows_heads_to_split(t):
    return jnp.concatenate([t[h * HEAD_DIM + half * HALF_DIM:h * HEAD_DIM + (half + 1) * HALF_DIM]
                            for half in range(2) for h in range(N_KV)], axis=0)


def _dot(a, b):
    return jnp.dot(a, b, preferred_element_type=F32)


def _dot_nt(a, b):
    return lax.dot_general(a, b, (((1,), (1,)), ((), ())), preferred_element_type=F32)


def _resident(shape):
    nd = len(shape)
    return pl.BlockSpec(shape, lambda i: (0,) * nd, pipeline_mode=pl.Buffered(1))


def _main_tile(i):
    return jnp.maximum(i - 1, 0)


def _main_spec(width):
    return pl.BlockSpec((TM, width), lambda i: (_main_tile(i), 0))


def _main_pos_spec():
    return pl.BlockSpec((TM, LANES), lambda i: (_main_tile(i) % TILES_PER_SEQ, 0))


def _params():
    return pltpu.CompilerParams(dimension_semantics=("arbitrary",),
                                vmem_limit_bytes=V7X_VMEM_LIMIT)


def _ffn_rows(x_ref, o_ref, g, w_in_ref, w_out_ref, a_ref, final_g=None):
    x = x_ref[...]
    m = x.shape[0]
    hn = _rms(x, g).astype(BF16)
    for c in range(D_FF // FF_CHUNK):
        lo = c * FF_CHUNK
        gate = _dot(hn, w_in_ref[:, lo:lo + FF_CHUNK])
        up = _dot(hn, w_in_ref[:, D_FF + lo:D_FF + lo + FF_CHUNK])
        a_ref[0:m, lo:lo + FF_CHUNK] = (gate * jax.nn.sigmoid(gate) * up).astype(BF16)
    sq = None
    for c in range(D_MODEL // OUT_CHUNK):
        cols = slice(c * OUT_CHUNK, (c + 1) * OUT_CHUNK)
        y = x[:, cols] + 0.5 * _dot(a_ref[0:m, :], w_out_ref[:, cols])
        o_ref[:, cols] = y
        if final_g is not None:
            part = jnp.sum(y * y, axis=-1, keepdims=True)
            sq = part if sq is None else sq + part
    if final_g is not None:
        o_ref[...] = o_ref[...] * lax.rsqrt(sq * (1.0 / D_MODEL) + EPS) * final_g


def _make_ffn_kernel(has_final, has_kv, n_cast):
    def body(*refs):
        refs = list(refs)
        xm_ref, xt_ref, g_ref, w_in_ref, w_out_ref = refs[:5]
        pos = 5
        if has_final:
            gf_ref = refs[pos]
            pos += 1
        if has_kv:
            gkv_ref, wkv_ref, cm_ref, sm_ref, ct_ref, st_ref, ck_ref, cv_ref = refs[pos:pos + 8]
            pos += 8
        cast_src = refs[pos:pos + n_cast]
        pos += n_cast
        om_ref, ot_ref = refs[pos:pos + 2]
        pos += 2
        if has_kv:
            km_ref, vm_ref, kt_ref, vt_ref, kl_ref, vl_ref, nk_ref, nv_ref = refs[pos:pos + 8]
            pos += 8
        cast_dst = refs[pos:pos + n_cast]
        pos += n_cast
        a_ref = refs[pos]
        if has_kv:
            newk_ref, newv_ref = refs[pos + 1:pos + 3]
        i = pl.program_id(0)

        for src, dst in zip(cast_src, cast_dst):
            dst[...] = src[...].astype(BF16)

        def rows(x_ref, o_ref):
            _ffn_rows(x_ref, o_ref, g_ref[...], w_in_ref, w_out_ref, a_ref,
                      gf_ref[...] if has_final else None)

        def project_kv(x_ref, cos_ref, sin_ref, k_ref, v_ref):
            kv = _dot(_rms(x_ref[...], gkv_ref[...]).astype(BF16), wkv_ref[...])
            k = _rope(kv[:, :KV_W], cos_ref[...], sin_ref[...])
            v = kv[:, KV_W:]
            k_ref[...] = k
            v_ref[...] = v
            return k, v

        @pl.when(i == 0)
        def _():
            if has_kv:
                k, v = project_kv(xt_ref, ct_ref, st_ref, kt_ref, vt_ref)
                for c in range(N_SAMPLE // LANES):
                    newk_ref[c] = _rows_split_to_heads(k[c * LANES:(c + 1) * LANES, :].T)
                    newv_ref[c] = v[c * LANES:(c + 1) * LANES, :].T
            rows(xt_ref, ot_ref)

        @pl.when(i > 0)
        def _():
            if has_kv:
                t = i - 1
                k, v = project_kv(xm_ref, cm_ref, sm_ref, km_ref, vm_ref)
                kl_ref[...] = _rows_split_to_heads(k[TM - WINDOW:, :].T)
                vl_ref[...] = v[TM - WINDOW:, :].T

                lane = lax.broadcasted_iota(jnp.int32, (1, WINDOW), 1)
                is_new = lane >= WINDOW - DEC_SEQ
                first_lane = (t * CACHE_SEQS * DEC_SEQ) % LANES
                for new_ref, c_ref, n_ref in ((newk_ref, ck_ref, nk_ref), (newv_ref, cv_ref, nv_ref)):
                    new_t = new_ref[(t * CACHE_SEQS * DEC_SEQ) // LANES]
                    for b in range(CACHE_SEQS):
                        shifted = pltpu.roll(c_ref[b].reshape(KV_W, WINDOW), WINDOW - DEC_SEQ, 1)
                        amount = (WINDOW - DEC_SEQ - b * DEC_SEQ - first_lane) & (LANES - 1)
                        placed = pltpu.roll(new_t, amount, 1)
                        n_ref[b] = jnp.where(is_new, placed, shifted).reshape(N_KV, HEAD_DIM, WINDOW)
            rows(xm_ref, om_ref)

    return body


def _cast_specs(casts):
    in_specs, out_specs, out_shape, args = [], [], [], []
    for arr, lead, steps in casts:
        n_rows, n_cols = arr.shape[-2:]
        rows = n_rows // steps
        assert rows * steps == n_rows and rows % BF16_ROWS == 0
        blk = lambda i, steps=steps: jnp.minimum(_main_tile(i), steps - 1)
        in_specs.append(pl.BlockSpec((None,) * len(lead) + (rows, n_cols),
                                     lambda i, lead=lead, blk=blk: lead + (blk(i), 0)))
        out_specs.append(pl.BlockSpec((rows, n_cols), lambda i, blk=blk: (blk(i), 0)))
        out_shape.append(jax.ShapeDtypeStruct((n_rows, n_cols), BF16))
        args.append(arr)
    return in_specs, out_specs, out_shape, args


def _ffn_call(xm, xt, g, w_in, w_out, *, name, final_g=None, kv=None, casts=()):
    in_specs = [_main_spec(D_MODEL), _resident((N_TAIL, D_MODEL)), _resident((1, D_MODEL)),
                _resident((D_MODEL, 2 * D_FF)), _resident((D_FF, D_MODEL))]
    args = [xm, xt, g.reshape(1, D_MODEL), w_in, w_out]
    out_shape = [jax.ShapeDtypeStruct((N_MAIN, D_MODEL), F32),
                 jax.ShapeDtypeStruct((N_TAIL, D_MODEL), F32)]
    out_specs = [_main_spec(D_MODEL), pl.BlockSpec((N_TAIL, D_MODEL), lambda i: (0, 0))]
    if final_g is not None:
        in_specs.append(_resident((1, D_MODEL)))
        args.append(final_g.reshape(1, D_MODEL))
    scratch = [pltpu.VMEM((N_TAIL, D_FF), BF16)]
    if kv is not None:
        g_kv, w_kv, cos_m, sin_m, cos_t, sin_t, ck, cv = kv
        cached = pl.BlockSpec((CACHE_SEQS, N_KV, HEAD_DIM, WINDOW), lambda i: (_main_tile(i), 0, 0, 0))
        in_specs += [_resident((1, D_MODEL)), _resident((D_MODEL, 2 * KV_W)),
                     _main_pos_spec(), _main_pos_spec(),
                     _resident((N_TAIL, LANES)), _resident((N_TAIL, LANES)), cached, cached]
        args += [g_kv.reshape(1, D_MODEL), w_kv, cos_m, sin_m, cos_t, sin_t, ck, cv]
        out_shape += [jax.ShapeDtypeStruct((N_MAIN, KV_W), F32)] * 2
        out_shape += [jax.ShapeDtypeStruct((N_TAIL, KV_W), F32)] * 2
        out_shape += [jax.ShapeDtypeStruct((BATCH, KV_W, WINDOW), F32)] * 2
        out_shape += [jax.ShapeDtypeStruct((DEC_BATCH, N_KV, HEAD_DIM, WINDOW), F32)] * 2
        last = pl.BlockSpec((None, KV_W, WINDOW), lambda i: (_main_tile(i) // TILES_PER_SEQ, 0, 0))
        out_specs += [_main_spec(KV_W), _main_spec(KV_W),
                      pl.BlockSpec((N_TAIL, KV_W), lambda i: (0, 0)),
                      pl.BlockSpec((N_TAIL, KV_W), lambda i: (0, 0)), last, last, cached, cached]
        scratch += [pltpu.VMEM((N_SAMPLE // LANES, KV_W, LANES), F32)] * 2
    c_in, c_out, c_shape, c_args = _cast_specs(casts)
    return pl.pallas_call(
        _make_ffn_kernel(final_g is not None, kv is not None, len(casts)),
        grid=(1 + N_TILES,),
        in_specs=in_specs + c_in,
        out_specs=out_specs + c_out,
        out_shape=out_shape + c_shape,
        scratch_shapes=scratch,
        compiler_params=_params(),
        name=name,
    )(*args, *c_args)


CARRY = 8


def _conv_rows(x_ref, o_ref, g, w_in_ref, kern_ref, w_out_ref, ubuf_ref, mbuf_ref, fix):
    x = x_ref[...]
    m = x.shape[0]
    hn = _rms(x, g).astype(BF16)
    for c in range(D_MODEL // OUT_CHUNK):
        cols = slice(c * OUT_CHUNK, (c + 1) * OUT_CHUNK)
        b, cc, z = (_dot(hn, w_in_ref[:, j * D_MODEL + c * OUT_CHUNK:j * D_MODEL + (c + 1) * OUT_CHUNK])
                    for j in range(3))
        u = cc * z
        ubuf_ref[CARRY:CARRY + m, cols] = u
        prev1 = ubuf_ref[CARRY - 1:CARRY - 1 + m, cols]
        prev2 = ubuf_ref[CARRY - 2:CARRY - 2 + m, cols]
        if fix is not None:
            prev1, prev2 = fix(prev1, prev2, cols)
        conv = kern_ref[0:1, cols] * prev2 + kern_ref[1:2, cols] * prev1 + kern_ref[2:3, cols] * u
        mbuf_ref[0:m, cols] = (b * conv).astype(BF16)
    for c in range(D_MODEL // OUT_CHUNK):
        cols = slice(c * OUT_CHUNK, (c + 1) * OUT_CHUNK)
        o_ref[:, cols] = x[:, cols] + _dot(mbuf_ref[0:m, :], w_out_ref[:, cols])


def _conv_kernel(xm_ref, xt_ref, g_ref, w_in_ref, kern_ref, w_out_ref, s1_ref, s2_ref,
                 om_ref, ot_ref, ut_ref, sp_ref, ubuf_ref, meta_ref, mbuf_ref):
    i = pl.program_id(0)

    @pl.when(i == 0)
    def _():
        ubuf_ref[0:CARRY, :] = jnp.zeros((CARRY, D_MODEL), F32)

        def fix(prev1, prev2, cols):
            step = lax.broadcasted_iota(jnp.int32, (N_SAMPLE, 1), 0) & (DEC_SEQ - 1)
            meta = lax.broadcasted_iota(jnp.int32, (N_META, 1), 0)
            p1 = [jnp.where(step >= 1, prev1[:N_SAMPLE], s1_ref[:, cols]),
                  jnp.where(meta >= 1, prev1[N_SAMPLE:], 0.0)]
            p2 = [jnp.where(step >= 2, prev2[:N_SAMPLE], s2_ref[:, cols]),
                  jnp.where(meta >= 2, prev2[N_SAMPLE:], 0.0)]
            return jnp.concatenate(p1, axis=0), jnp.concatenate(p2, axis=0)

        _conv_rows(xt_ref, ot_ref, g_ref[...], w_in_ref, kern_ref, w_out_ref, ubuf_ref, mbuf_ref, fix)
        ut_ref[...] = ubuf_ref[CARRY:CARRY + N_TAIL, :]
        meta_ref[...] = ubuf_ref[N_TAIL:N_TAIL + CARRY, :]

    @pl.when(i > 0)
    def _():
        t = i - 1

        @pl.when(t % TILES_PER_SEQ == 0)
        def _():
            ubuf_ref[0:CARRY, :] = meta_ref[...]

        _conv_rows(xm_ref, om_ref, g_ref[...], w_in_ref, kern_ref, w_out_ref, ubuf_ref, mbuf_ref, None)
        last = ubuf_ref[TM:TM + CARRY, :]
        ubuf_ref[0:CARRY, :] = last

        @pl.when(t % TILES_PER_SEQ == TILES_PER_SEQ - 1)
        def _():
            sp_ref[t // TILES_PER_SEQ] = last


def _conv_call(xm, xt, g, w_in, kern, w_out, s1, s2):
    return pl.pallas_call(
        _conv_kernel,
        grid=(1 + N_TILES,),
        in_specs=[_main_spec(D_MODEL), _resident((N_TAIL, D_MODEL)), _resident((1, D_MODEL)),
                  _resident((D_MODEL, 3 * D_MODEL)), _resident((CONV_W, D_MODEL)),
                  _resident((D_MODEL, D_MODEL)),
                  _resident((N_SAMPLE, D_MODEL)), _resident((N_SAMPLE, D_MODEL))],
        out_specs=[_main_spec(D_MODEL),
                   pl.BlockSpec((N_TAIL, D_MODEL), lambda i: (0, 0)),
                   pl.BlockSpec((N_TAIL, D_MODEL), lambda i: (0, 0)),
                   pl.BlockSpec((BATCH, CARRY, D_MODEL), lambda i: (0, 0, 0))],
        out_shape=[jax.ShapeDtypeStruct((N_MAIN, D_MODEL), F32),
                   jax.ShapeDtypeStruct((N_TAIL, D_MODEL), F32),
                   jax.ShapeDtypeStruct((N_TAIL, D_MODEL), F32),
                   jax.ShapeDtypeStruct((BATCH, CARRY, D_MODEL), F32)],
        scratch_shapes=[pltpu.VMEM((CARRY + N_TAIL, D_MODEL), F32),
                        pltpu.VMEM((CARRY, D_MODEL), F32),
                        pltpu.VMEM((N_TAIL, D_MODEL), BF16)],
        compiler_params=_params(),
        name="conv_mixer",
    )(xm, xt, g.reshape(1, D_MODEL), w_in, kern, w_out, s1, s2)


def _attend(qp, segs, sinks_ref):
    mq = qp.shape[0]
    lane = lax.broadcasted_iota(jnp.int32, (1, KV_W), 1)
    qk_head = (lane & (LANES - 1)) // HALF_DIM
    v_head = lane // HEAD_DIM
    row_group = lax.broadcasted_iota(jnp.int32, (GROUP * mq, 1), 0) // mq
    masks = [jnp.concatenate([seg[2]] * GROUP, axis=0) for seg in segs]
    out = None
    for k in range(N_KV):
        head = v_head == k
        qs = jnp.concatenate(
            [jnp.where(qk_head == k, qp[:, g * KV_W:(g + 1) * KV_W], jnp.zeros((), BF16))
             for g in range(GROUP)], axis=0)
        sink = jnp.full((GROUP * mq, 1), sinks_ref[k * GROUP], F32)
        for g in range(1, GROUP):
            sink = jnp.where(row_group == g, sinks_ref[k * GROUP + g], sink)
        scores = []
        for (kb, _, _, k_transposed), msk in zip(segs, masks):
            s = _dot(qs, kb) if k_transposed else _dot_nt(qs, kb)
            scores.append(jnp.where(msk, s, NEG))
        mx = sink
        for s in scores:
            mx = jnp.maximum(mx, jnp.max(s, axis=-1, keepdims=True))
        denom = jnp.exp(sink - mx)
        acc = None
        for s, (_, vb, _, _) in zip(scores, segs):
            p = jnp.exp(s - mx)
            denom = denom + jnp.sum(p, axis=-1, keepdims=True)
            pv = _dot(p.astype(BF16), jnp.where(head, vb, jnp.zeros((), BF16)))
            acc = pv if acc is None else acc + pv
        acc = acc * (1.0 / denom)
        out = acc if out is None else out + acc
    return jnp.concatenate([out[g * mq:(g + 1) * mq, :] for g in range(GROUP)], axis=1)


def _queries(x, g, wq_ref, cos, sin):
    q = _dot(_rms(x, g).astype(BF16), wq_ref[...])
    return (_rope(q, cos, sin) * (HEAD_DIM ** -0.5)).astype(BF16)


def _attn_main_kernel(sinks_ref, xm_ref, g_ref, wq_ref, wo_ref, cos_ref, sin_ref,
                      km_ref, vm_ref, kt_ref, vt_ref,
                      om_ref, q_ref, ao_ref, kbuf_ref, vbuf_ref):
    i = pl.program_id(0)
    start = i % TILES_PER_SEQ == 0

    x = xm_ref[...]
    q_ref[...] = _queries(x, g_ref[...], wq_ref, cos_ref[...], sin_ref[...])

    @pl.when(start)
    def _():
        pad = jnp.zeros((BLK - N_META, KV_W), BF16)
        kbuf_ref[0:BLK - N_META, :] = pad
        vbuf_ref[0:BLK - N_META, :] = pad
        kbuf_ref[BLK - N_META:BLK, :] = kt_ref[N_SAMPLE:N_TAIL, :].astype(BF16)
        vbuf_ref[BLK - N_META:BLK, :] = vt_ref[N_SAMPLE:N_TAIL, :].astype(BF16)

    @pl.when(jnp.logical_not(start))
    def _():
        kbuf_ref[0:BLK, :] = kbuf_ref[TM:TM + BLK, :]
        vbuf_ref[0:BLK, :] = vbuf_ref[TM:TM + BLK, :]

    kbuf_ref[BLK:BLK + TM, :] = km_ref[...].astype(BF16)
    vbuf_ref[BLK:BLK + TM, :] = vm_ref[...].astype(BF16)

    qi = lax.broadcasted_iota(jnp.int32, (BLK, 2 * BLK), 0)
    kj = lax.broadcasted_iota(jnp.int32, (BLK, 2 * BLK), 1)
    band = (kj >= qi) & (kj <= qi + WINDOW)
    first_key = jnp.where(start, BLK - N_META, 0)
    for blk in range(BLKS_PER_TILE):
        lo = blk * BLK
        mask = band & (kj >= first_key) if blk == 0 else band
        att = _attend(q_ref[lo:lo + BLK, :],
                      [(kbuf_ref[lo:lo + 2 * BLK, :], vbuf_ref[lo:lo + 2 * BLK, :], mask, False)],
                      sinks_ref)
        ao_ref[lo:lo + BLK, :] = att.astype(BF16)
    om_ref[...] = x + _dot(ao_ref[...], wo_ref[...])


def _attn_main_call(sinks, xm, g, wq, wo, cos_m, sin_m, km, vm, kt, vt):
    tile = lambda w: pl.BlockSpec((TM, w), lambda i: (i, 0))
    pos = pl.BlockSpec((TM, LANES), lambda i: (i % TILES_PER_SEQ, 0))
    return pl.pallas_call(
        _attn_main_kernel,
        grid=(N_TILES,),
        in_specs=[pl.BlockSpec(memory_space=pltpu.SMEM),
                  tile(D_MODEL), _resident((1, D_MODEL)),
                  _resident((D_MODEL, D_MODEL)), _resident((D_MODEL, D_MODEL)),
                  pos, pos, tile(KV_W), tile(KV_W),
                  _resident((N_TAIL, KV_W)), _resident((N_TAIL, KV_W))],
        out_specs=tile(D_MODEL),
        out_shape=jax.ShapeDtypeStruct((N_MAIN, D_MODEL), F32),
        scratch_shapes=[pltpu.VMEM((TM, D_MODEL), BF16), pltpu.VMEM((TM, D_MODEL), BF16),
                        pltpu.VMEM((BLK + TM, KV_W), BF16), pltpu.VMEM((BLK + TM, KV_W), BF16)],
        compiler_params=_params(),
        name="attn_main",
    )(sinks, xm, g.reshape(1, D_MODEL), wq, wo, cos_m, sin_m, km, vm, kt, vt)


def _attn_tail_kernel(sinks_ref, xt_ref, g_ref, wq_ref, wo_ref, cos_ref, sin_ref,
                      kt_ref, vt_ref, ck_ref, cv_ref, ot_ref, q_ref, ao_ref):
    s = pl.program_id(0)

    @pl.when(s == 0)
    def _():
        q_ref[...] = _queries(xt_ref[...], g_ref[...], wq_ref, cos_ref[...], sin_ref[...])
        qi = lax.broadcasted_iota(jnp.int32, (N_META, N_META), 0)
        kj = lax.broadcasted_iota(jnp.int32, (N_META, N_META), 1)
        km = kt_ref[N_SAMPLE:N_TAIL, :].astype(BF16)
        vm = vt_ref[N_SAMPLE:N_TAIL, :].astype(BF16)
        att = _attend(q_ref[N_SAMPLE:N_TAIL, :], [(km, vm, kj <= qi, False)], sinks_ref)
        ao_ref[N_SAMPLE:N_TAIL, :] = att.astype(BF16)

    rows = pl.ds(pl.multiple_of(s * GROUP_ROWS, GROUP_ROWS), GROUP_ROWS)
    k_new = kt_ref[rows, :]
    v_new = vt_ref[rows, :]
    ck = [ck_ref[b].reshape(KV_W, WINDOW) for b in range(SAMPLE_GROUP)]
    cv = [cv_ref[b].reshape(KV_W, WINDOW) for b in range(SAMPLE_GROUP)]
    k_cached_t = jnp.concatenate([_rows_heads_to_split(c) for c in ck], axis=1).astype(BF16)
    v_cached = jnp.concatenate([c.T for c in cv], axis=0).astype(BF16)

    n_cached = SAMPLE_GROUP * WINDOW
    qr = lax.broadcasted_iota(jnp.int32, (GROUP_ROWS, n_cached), 0)
    kc = lax.broadcasted_iota(jnp.int32, (GROUP_ROWS, n_cached), 1)
    mask_cached = ((kc // WINDOW) == (qr // DEC_SEQ)) & ((kc % WINDOW) >= (qr % DEC_SEQ))
    qn = lax.broadcasted_iota(jnp.int32, (GROUP_ROWS, GROUP_ROWS), 0)
    kn = lax.broadcasted_iota(jnp.int32, (GROUP_ROWS, GROUP_ROWS), 1)
    mask_new = ((kn // DEC_SEQ) == (qn // DEC_SEQ)) & ((kn % DEC_SEQ) <= (qn % DEC_SEQ))
    att = _attend(q_ref[rows, :],
                  [(k_cached_t, v_cached, mask_cached, True),
                   (k_new.astype(BF16), v_new.astype(BF16), mask_new, False)],
                  sinks_ref)
    ao_ref[rows, :] = att.astype(BF16)

    @pl.when(s == N_GROUPS - 1)
    def _():
        ot_ref[...] = xt_ref[...] + _dot(ao_ref[...], wo_ref[...])


def _attn_tail_call(sinks, xt, g, wq, wo, cos_t, sin_t, kt, vt, ck, cv):
    cached = pl.BlockSpec((SAMPLE_GROUP, N_KV, HEAD_DIM, WINDOW), lambda s: (s, 0, 0, 0))
    return pl.pallas_call(
        _attn_tail_kernel,
        grid=(N_GROUPS,),
        in_specs=[pl.BlockSpec(memory_space=pltpu.SMEM),
                  _resident((N_TAIL, D_MODEL)), _resident((1, D_MODEL)),
                  _resident((D_MODEL, D_MODEL)), _resident((D_MODEL, D_MODEL)),
                  _resident((N_TAIL, LANES)), _resident((N_TAIL, LANES)),
                  _resident((N_TAIL, KV_W)), _resident((N_TAIL, KV_W)),
                  cached, cached],
        out_specs=pl.BlockSpec((N_TAIL, D_MODEL), lambda s: (0, 0)),
        out_shape=jax.ShapeDtypeStruct((N_TAIL, D_MODEL), F32),
        scratch_shapes=[pltpu.VMEM((N_TAIL, D_MODEL), BF16), pltpu.VMEM((N_TAIL, D_MODEL), BF16)],
        compiler_params=_params(),
        name="attn_tail",
    )(sinks, xt, g.reshape(1, D_MODEL), wq, wo, cos_t, sin_t, kt, vt, ck, cv)


def _rope_tables(pos):
    inv = 1.0 / (ROPE_THETA ** (jnp.arange(0, HEAD_DIM, 2, dtype=F32) / HEAD_DIM))
    ang = pos.astype(F32)[:, None] * inv[None, :]
    cos, sin = jnp.cos(ang), jnp.sin(ang)
    return jnp.tile(cos, (1, N_KV)), jnp.tile(sin, (1, N_KV))


def kernel(x_prompt, x_sample, state_conv, cache_k, cache_v, meta_tokens, norm_g, ffn_w_in,
           ffn_w_out, conv_w_in, conv_kernel, conv_w_out, kv_norm_g, w_kv, w_q, w_o, sinks,
           final_norm_g):
    assert x_prompt.shape == (BATCH, SEQ, D_MODEL) and x_sample.shape == (DEC_BATCH, DEC_SEQ, D_MODEL)
    assert cache_k.shape == (DEC_BATCH, WINDOW, N_KV, HEAD_DIM)
    assert norm_g.shape[0] == 2 and conv_w_in.shape[0] == 1 and w_q.shape[0] == 1

    xm = x_prompt.reshape(N_MAIN, D_MODEL)
    xt = jnp.concatenate([x_sample.reshape(N_SAMPLE, D_MODEL), meta_tokens], axis=0)

    wq = w_q[0].reshape(D_MODEL, N_KV, GROUP, 2, HALF_DIM).transpose(0, 2, 3, 1, 4)
    wq = wq.reshape(D_MODEL, D_MODEL).astype(BF16)
    wo = w_o[0].reshape(N_KV, GROUP, HEAD_DIM, D_MODEL).transpose(1, 0, 2, 3)
    wo = wo.reshape(D_MODEL, D_MODEL).astype(BF16)
    wk = w_kv[:, :KV_W].reshape(D_MODEL, N_KV, 2, HALF_DIM).transpose(0, 2, 1, 3)
    wkv = jnp.concatenate([wk.reshape(D_MODEL, KV_W), w_kv[:, KV_W:]], axis=1).astype(BF16)

    pos_main = N_META + jnp.arange(SEQ, dtype=jnp.int32)
    pos_tail = jnp.concatenate([
        jnp.tile(PAST_LEN + jnp.arange(DEC_SEQ, dtype=jnp.int32), DEC_BATCH),
        jnp.arange(N_META, dtype=jnp.int32)])
    cos_m, sin_m = _rope_tables(pos_main)
    cos_t, sin_t = _rope_tables(pos_tail)

    sc = state_conv[0]
    grow = lambda a: jnp.pad(a, ((0, 0), (0, DEC_SEQ - a.shape[1]), (0, 0))).reshape(N_SAMPLE, D_MODEL)
    s1 = grow(sc[:, 1:])
    s2 = grow(sc)

    ffn_casts = lambda l, j: [(ffn_w_in, (l, j), N_TILES), (ffn_w_out, (l, j), N_TILES // 2)]
    slot_minor = lambda a: a.transpose(0, 2, 3, 1)
    ck, cv = slot_minor(cache_k), slot_minor(cache_v)

    xm, xt, cw_in, cw_out, w_in, w_out = _ffn_call(
        xm, xt, norm_g[0, 0], ffn_w_in[0, 0].astype(BF16), ffn_w_out[0, 0].astype(BF16),
        name="ffn_l0a",
        casts=[(conv_w_in, (0,), N_TILES), (conv_w_out, (0,), N_TILES)] + ffn_casts(0, 1))
    xm, xt, u_tail, sp = _conv_call(xm, xt, norm_g[0, 1], cw_in, conv_kernel[0], cw_out, s1, s2)
    xm, xt, w_in, w_out = _ffn_call(
        xm, xt, norm_g[0, 2], w_in, w_out, name="ffn_l0b", casts=ffn_casts(1, 0))

    xm, xt, km, vm, kt, vt, k_last, v_last, nk, nv, w_in, w_out = _ffn_call(
        xm, xt, norm_g[1, 0], w_in, w_out, name="ffn_l1a_kv",
        kv=(kv_norm_g, wkv, cos_m, sin_m, cos_t, sin_t, ck, cv), casts=ffn_casts(1, 1))
    xt = _attn_tail_call(sinks[0], xt, norm_g[1, 1], wq, wo, cos_t, sin_t, kt, vt, ck, cv)
    xm = _attn_main_call(sinks[0], xm, norm_g[1, 1], wq, wo, cos_m, sin_m, km, vm, kt, vt)
    ym, yt = _ffn_call(xm, xt, norm_g[1, 2], w_in, w_out, name="ffn_l1b_final",
                       final_g=final_norm_g)

    y_prompt = ym.reshape(BATCH, SEQ, D_MODEL)
    y_sample = yt[:N_SAMPLE].reshape(DEC_BATCH, DEC_SEQ, D_MODEL)
    new_state_conv_p = sp[:, CARRY - (CONV_W - 1):][None]
    new_state_conv_s = u_tail[:N_SAMPLE].reshape(DEC_BATCH, DEC_SEQ, D_MODEL)[:, DEC_SEQ - (CONV_W - 1):][None]
    slot_major = lambda a: a.transpose(0, 3, 1, 2)
    new_k_p = slot_major(k_last.reshape(BATCH, N_KV, HEAD_DIM, WINDOW))
    new_v_p = slot_major(v_last.reshape(BATCH, N_KV, HEAD_DIM, WINDOW))
    return (y_prompt, y_sample, new_state_conv_p, new_state_conv_s,
            new_k_p, new_v_p, slot_major(nk), slot_major(nv))
```

```python
import jax
import jax.numpy as jnp
from jax import lax
from jax.experimental import pallas as pl
from jax.experimental.pallas import tpu as pltpu

D_MODEL = 1024
BATCH = 2
SEQ = 8192
DEC_BATCH = 128
DEC_SEQ = 4
PAST_LEN = 8192
N_META = 16
D_FF = 2816
CONV_W = 3
HEAD_DIM = 64
N_HEADS = 16
N_KV = 4
GROUP = 4
WINDOW = 128
ROPE_THETA = 10000.0
EPS = 1e-6
NEG = -1e30

KV_W = N_KV * HEAD_DIM
HALF_DIM = HEAD_DIM // 2
N_MAIN = BATCH * SEQ
N_SAMPLE = DEC_BATCH * DEC_SEQ
N_TAIL = N_SAMPLE + N_META
TM = 512
N_TILES = N_MAIN // TM
TILES_PER_SEQ = SEQ // TM
FF_CHUNK = 256
OUT_CHUNK = 256
BLK = WINDOW
BLKS_PER_TILE = TM // BLK
SAMPLE_GROUP = 8
N_GROUPS = DEC_BATCH // SAMPLE_GROUP
CACHE_SEQS = DEC_BATCH // N_TILES
GROUP_ROWS = SAMPLE_GROUP * DEC_SEQ
LANES = 128
BF16_ROWS = 16
V7X_VMEM_LIMIT = 56 * 1024 * 1024

F32 = jnp.float32
BF16 = jnp.bfloat16


def _rms(x, g):
    return x * lax.rsqrt(jnp.mean(x * x, axis=-1, keepdims=True) + EPS) * g


def _rope(x, cos, sin):
    outs = []
    for c in range(x.shape[1] // KV_W):
        x1 = x[:, c * KV_W:c * KV_W + LANES]
        x2 = x[:, c * KV_W + LANES:(c + 1) * KV_W]
        outs += [x1 * cos - x2 * sin, x2 * cos + x1 * sin]
    return jnp.concatenate(outs, axis=1)


def _rows_split_to_heads(t):
    return jnp.concatenate([t[half * LANES + h * HALF_DIM:half * LANES + (h + 1) * HALF_DIM]
                            for h in range(N_KV) for half in range(2)], axis=0)


def _rows_heads_to_split(t):
    return jnp.concatenate([t[h * HEAD_DIM + half * HALF_DIM:h * HEAD_DIM + (half + 1) * HALF_DIM]
                            for half in range(2) for h in range(N_KV)], axis=0)


def _dot(a, b):
    return jnp.dot(a, b, preferred_element_type=F32)


def _dot_nt(a, b):
    return lax.dot_general(a, b, (((1,), (1,)), ((), ())), preferred_element_type=F32)


def _resident(shape):
    nd = len(shape)
    return pl.BlockSpec(shape, lambda i: (0,) * nd, pipeline_mode=pl.Buffered(1))


def _main_tile(i):
    return jnp.maximum(i - 1, 0)


def _main_spec(width):
    return pl.BlockSpec((TM, width), lambda i: (_main_tile(i), 0))


def _main_pos_spec():
    return pl.BlockSpec((TM, LANES), lambda i: (_main_tile(i) % TILES_PER_SEQ, 0))


def _params():
    return pltpu.CompilerParams(dimension_semantics=("arbitrary",),
                                vmem_limit_bytes=V7X_VMEM_LIMIT)


def _ffn_rows(x_ref, o_ref, g, w_in_ref, w_out_ref, a_ref, final_g=None):
    x = x_ref[...]
    m = x.shape[0]
    hn = _rms(x, g).astype(BF16)
    for c in range(D_FF // FF_CHUNK):
        lo = c * FF_CHUNK
        gate = _dot(hn, w_in_ref[:, lo:lo + FF_CHUNK])
        up = _dot(hn, w_in_ref[:, D_FF + lo:D_FF + lo + FF_CHUNK])
        a_ref[0:m, lo:lo + FF_CHUNK] = (gate * jax.nn.sigmoid(gate) * up).astype(BF16)
    sq = None
    for c in range(D_MODEL // OUT_CHUNK):
        cols = slice(c * OUT_CHUNK, (c + 1) * OUT_CHUNK)
        y = x[:, cols] + 0.5 * _dot(a_ref[0:m, :], w_out_ref[:, cols])
        o_ref[:, cols] = y
        if final_g is not None:
            part = jnp.sum(y * y, axis=-1, keepdims=True)
            sq = part if sq is None else sq + part
    if final_g is not None:
        o_ref[...] = o_ref[...] * lax.rsqrt(sq * (1.0 / D_MODEL) + EPS) * final_g


def _make_ffn_kernel(has_final, has_kv, n_cast):
    def body(*refs):
        refs = list(refs)
        xm_ref, xt_ref, g_ref, w_in_ref, w_out_ref = refs[:5]
        pos = 5
        if has_final:
            gf_ref = refs[pos]
            pos += 1
        if has_kv:
            gkv_ref, wkv_ref, cm_ref, sm_ref, ct_ref, st_ref, ck_ref, cv_ref = refs[pos:pos + 8]
            pos += 8
        cast_src = refs[pos:pos + n_cast]
        pos += n_cast
        om_ref, ot_ref = refs[pos:pos + 2]
        pos += 2
        if has_kv:
            km_ref, vm_ref, kt_ref, vt_ref, kl_ref, vl_ref, nk_ref, nv_ref = refs[pos:pos + 8]
            pos += 8
        cast_dst = refs[pos:pos + n_cast]
        pos += n_cast
        a_ref = refs[pos]
        if has_kv:
            newk_ref, newv_ref = refs[pos + 1:pos + 3]
        i = pl.program_id(0)

        for src, dst in zip(cast_src, cast_dst):
            dst[...] = src[...].astype(BF16)

        def rows(x_ref, o_ref):
            _ffn_rows(x_ref, o_ref, g_ref[...], w_in_ref, w_out_ref, a_ref,
                      gf_ref[...] if has_final else None)

        def project_kv(x_ref, cos_ref, sin_ref, k_ref, v_ref):
            kv = _dot(_rms(x_ref[...], gkv_ref[...]).astype(BF16), wkv_ref[...])
            k = _rope(kv[:, :KV_W], cos_ref[...], sin_ref[...])
            v = kv[:, KV_W:]
            k_ref[...] = k
            v_ref[...] = v
            return k, v

        @pl.when(i == 0)
        def _():
            if has_kv:
                k, v = project_kv(xt_ref, ct_ref, st_ref, kt_ref, vt_ref)
                for c in range(N_SAMPLE // LANES):
                    newk_ref[c] = _rows_split_to_heads(k[c * LANES:(c + 1) * LANES, :].T)
                    newv_ref[c] = v[c * LANES:(c + 1) * LANES, :].T
            rows(xt_ref, ot_ref)

        @pl.when(i > 0)
        def _():
            if has_kv:
                t = i - 1
                k, v = project_kv(xm_ref, cm_ref, sm_ref, km_ref, vm_ref)
                kl_ref[...] = _rows_split_to_heads(k[TM - WINDOW:, :].T)
                vl_ref[...] = v[TM - WINDOW:, :].T

                lane = lax.broadcasted_iota(jnp.int32, (1, WINDOW), 1)
                is_new = lane >= WINDOW - DEC_SEQ
                first_lane = (t * CACHE_SEQS * DEC_SEQ) % LANES
                for new_ref, c_ref, n_ref in ((newk_ref, ck_ref, nk_ref), (newv_ref, cv_ref, nv_ref)):
                    new_t = new_ref[(t * CACHE_SEQS * DEC_SEQ) // LANES]
                    for b in range(CACHE_SEQS):
                        shifted = pltpu.roll(c_ref[b].reshape(KV_W, WINDOW), WINDOW - DEC_SEQ, 1)
                        amount = (WINDOW - DEC_SEQ - b * DEC_SEQ - first_lane) & (LANES - 1)
                        placed = pltpu.roll(new_t, amount, 1)
                        n_ref[b] = jnp.where(is_new, placed, shifted).reshape(N_KV, HEAD_DIM, WINDOW)
            rows(xm_ref, om_ref)

    return body


def _cast_specs(casts):
    in_specs, out_specs, out_shape, args = [], [], [], []
    for arr, lead, steps in casts:
        n_rows, n_cols = arr.shape[-2:]
        rows = n_rows // steps
        assert rows * steps == n_rows and rows % BF16_ROWS == 0
        blk = lambda i, steps=steps: jnp.minimum(_main_tile(i), steps - 1)
        in_specs.append(pl.BlockSpec((None,) * len(lead) + (rows, n_cols),
                                     lambda i, lead=lead, blk=blk: lead + (blk(i), 0)))
        out_specs.append(pl.BlockSpec((rows, n_cols), lambda i, blk=blk: (blk(i), 0)))
        out_shape.append(jax.ShapeDtypeStruct((n_rows, n_cols), BF16))
        args.append(arr)
    return in_specs, out_specs, out_shape, args


def _ffn_call(xm, xt, g, w_in, w_out, *, name, final_g=None, kv=None, casts=()):
    in_specs = [_main_spec(D_MODEL), _resident((N_TAIL, D_MODEL)), _resident((1, D_MODEL)),
                _resident((D_MODEL, 2 * D_FF)), _resident((D_FF, D_MODEL))]
    args = [xm, xt, g.reshape(1, D_MODEL), w_in, w_out]
    out_shape = [jax.ShapeDtypeStruct((N_MAIN, D_MODEL), F32),
                 jax.ShapeDtypeStruct((N_TAIL, D_MODEL), F32)]
    out_specs = [_main_spec(D_MODEL), pl.BlockSpec((N_TAIL, D_MODEL), lambda i: (0, 0))]
    if final_g is not None:
        in_specs.append(_resident((1, D_MODEL)))
        args.append(final_g.reshape(1, D_MODEL))
    scratch = [pltpu.VMEM((N_TAIL, D_FF), BF16)]
    if kv is not None:
        g_kv, w_kv, cos_m, sin_m, cos_t, sin_t, ck, cv = kv
        cached = pl.BlockSpec((CACHE_SEQS, N_KV, HEAD_DIM, WINDOW), lambda i: (_main_tile(i), 0, 0, 0))
        in_specs += [_resident((1, D_MODEL)), _resident((D_MODEL, 2 * KV_W)),
                     _main_pos_spec(), _main_pos_spec(),
                     _resident((N_TAIL, LANES)), _resident((N_TAIL, LANES)), cached, cached]
        args += [g_kv.reshape(1, D_MODEL), w_kv, cos_m, sin_m, cos_t, sin_t, ck, cv]
        out_shape += [jax.ShapeDtypeStruct((N_MAIN, KV_W), F32)] * 2
        out_shape += [jax.ShapeDtypeStruct((N_TAIL, KV_W), F32)] * 2
        out_shape += [jax.ShapeDtypeStruct((BATCH, KV_W, WINDOW), F32)] * 2
        out_shape += [jax.ShapeDtypeStruct((DEC_BATCH, N_KV, HEAD_DIM, WINDOW), F32)] * 2
        last = pl.BlockSpec((None, KV_W, WINDOW), lambda i: (_main_tile(i) // TILES_PER_SEQ, 0, 0))
        out_specs += [_main_spec(KV_W), _main_spec(KV_W),
                      pl.BlockSpec((N_TAIL, KV_W), lambda i: (0, 0)),
                      pl.BlockSpec((N_TAIL, KV_W), lambda i: (0, 0)), last, last, cached, cached]
        scratch += [pltpu.VMEM((N_SAMPLE // LANES, KV_W, LANES), F32)] * 2
    c_in, c_out, c_shape, c_args = _cast_specs(casts)
    return pl.pallas_call(
        _make_ffn_kernel(final_g is not None, kv is not None, len(casts)),
        grid=(1 + N_TILES,),
        in_specs=in_specs + c_in,
        out_specs=out_specs + c_out,
        out_shape=out_shape + c_shape,
        scratch_shapes=scratch,
        compiler_params=_params(),
        name=name,
    )(*args, *c_args)


CARRY = 8
CONV_TM = 1024


def _conv_rows(x_ref, o_ref, g, w_in_ref, kern_ref, w_out_ref, ubuf_ref, mbuf_ref, fix):
    x = x_ref[...]
    m = x.shape[0]
    hn = _rms(x, g).astype(BF16)
    for c in range(D_MODEL // OUT_CHUNK):
        cols = slice(c * OUT_CHUNK, (c + 1) * OUT_CHUNK)
        b, cc, z = (_dot(hn, w_in_ref[:, j * D_MODEL + c * OUT_CHUNK:j * D_MODEL + (c + 1) * OUT_CHUNK])
                    for j in range(3))
        u = cc * z
        ubuf_ref[CARRY:CARRY + m, cols] = u
        prev1 = ubuf_ref[CARRY - 1:CARRY - 1 + m, cols]
        prev2 = ubuf_ref[CARRY - 2:CARRY - 2 + m, cols]
        if fix is not None:
            prev1, prev2 = fix(prev1, prev2, cols)
        conv = kern_ref[0:1, cols] * prev2 + kern_ref[1:2, cols] * prev1 + kern_ref[2:3, cols] * u
        mbuf_ref[0:m, cols] = (b * conv).astype(BF16)
    for c in range(D_MODEL // OUT_CHUNK):
        cols = slice(c * OUT_CHUNK, (c + 1) * OUT_CHUNK)
        o_ref[:, cols] = x[:, cols] + _dot(mbuf_ref[0:m, :], w_out_ref[:, cols])


def _conv_kernel(xm_ref, xt_ref, g_ref, w_in_ref, kern_ref, w_out_ref, s1_ref, s2_ref,
                 om_ref, ot_ref, ut_ref, sp_ref, ubuf_ref, meta_ref, mbuf_ref):
    i = pl.program_id(0)

    @pl.when(i == 0)
    def _():
        ubuf_ref[0:CARRY, :] = jnp.zeros((CARRY, D_MODEL), F32)

        def fix(prev1, prev2, cols):
            step = lax.broadcasted_iota(jnp.int32, (N_SAMPLE, 1), 0) & (DEC_SEQ - 1)
            meta = lax.broadcasted_iota(jnp.int32, (N_META, 1), 0)
            p1 = [jnp.where(step >= 1, prev1[:N_SAMPLE], s1_ref[:, cols]),
                  jnp.where(meta >= 1, prev1[N_SAMPLE:], 0.0)]
            p2 = [jnp.where(step >= 2, prev2[:N_SAMPLE], s2_ref[:, cols]),
                  jnp.where(meta >= 2, prev2[N_SAMPLE:], 0.0)]
            return jnp.concatenate(p1, axis=0), jnp.concatenate(p2, axis=0)

        _conv_rows(xt_ref, ot_ref, g_ref[...], w_in_ref, kern_ref, w_out_ref, ubuf_ref, mbuf_ref, fix)
        ut_ref[...] = ubuf_ref[CARRY:CARRY + N_TAIL, :]
        meta_ref[...] = ubuf_ref[N_TAIL:N_TAIL + CARRY, :]

    @pl.when(i > 0)
    def _():
        t = i - 1
        per_seq = SEQ // CONV_TM

        @pl.when(t % per_seq == 0)
        def _():
            ubuf_ref[0:CARRY, :] = meta_ref[...]

        _conv_rows(xm_ref, om_ref, g_ref[...], w_in_ref, kern_ref, w_out_ref, ubuf_ref, mbuf_ref, None)
        last = ubuf_ref[CONV_TM:CONV_TM + CARRY, :]
        ubuf_ref[0:CARRY, :] = last

        @pl.when(t % per_seq == per_seq - 1)
        def _():
            sp_ref[t // per_seq] = last


def _conv_call(xm, xt, g, w_in, kern, w_out, s1, s2):
    main = pl.BlockSpec((CONV_TM, D_MODEL), lambda i: (_main_tile(i), 0))
    return pl.pallas_call(
        _conv_kernel,
        grid=(1 + N_MAIN // CONV_TM,),
        in_specs=[main, _resident((N_TAIL, D_MODEL)), _resident((1, D_MODEL)),
                  _resident((D_MODEL, 3 * D_MODEL)), _resident((CONV_W, D_MODEL)),
                  _resident((D_MODEL, D_MODEL)),
                  _resident((N_SAMPLE, D_MODEL)), _resident((N_SAMPLE, D_MODEL))],
        out_specs=[main,
                   pl.BlockSpec((N_TAIL, D_MODEL), lambda i: (0, 0)),
                   pl.BlockSpec((N_TAIL, D_MODEL), lambda i: (0, 0)),
                   pl.BlockSpec((BATCH, CARRY, D_MODEL), lambda i: (0, 0, 0))],
        out_shape=[jax.ShapeDtypeStruct((N_MAIN, D_MODEL), F32),
                   jax.ShapeDtypeStruct((N_TAIL, D_MODEL), F32),
                   jax.ShapeDtypeStruct((N_TAIL, D_MODEL), F32),
                   jax.ShapeDtypeStruct((BATCH, CARRY, D_MODEL), F32)],
        scratch_shapes=[pltpu.VMEM((CARRY + max(N_TAIL, CONV_TM), D_MODEL), F32),
                        pltpu.VMEM((CARRY, D_MODEL), F32),
                        pltpu.VMEM((max(N_TAIL, CONV_TM), D_MODEL), BF16)],
        compiler_params=_params(),
        name="conv_mixer",
    )(xm, xt, g.reshape(1, D_MODEL), w_in, kern, w_out, s1, s2)


def _attend(qp, segs, sinks_ref):
    mq = qp.shape[0]
    lane = lax.broadcasted_iota(jnp.int32, (1, KV_W), 1)
    qk_head = (lane & (LANES - 1)) // HALF_DIM
    v_head = lane // HEAD_DIM
    row_group = lax.broadcasted_iota(jnp.int32, (GROUP * mq, 1), 0) // mq
    masks = [jnp.concatenate([seg[2]] * GROUP, axis=0) for seg in segs]
    out = None
    for k in range(N_KV):
        head = v_head == k
        qs = jnp.concatenate(
            [jnp.where(qk_head == k, qp[:, g * KV_W:(g + 1) * KV_W], jnp.zeros((), BF16))
             for g in range(GROUP)], axis=0)
        sink = jnp.full((GROUP * mq, 1), sinks_ref[k * GROUP], F32)
        for g in range(1, GROUP):
            sink = jnp.where(row_group == g, sinks_ref[k * GROUP + g], sink)
        scores = []
        for (kb, _, _, k_transposed), msk in zip(segs, masks):
            s = _dot(qs, kb) if k_transposed else _dot_nt(qs, kb)
            scores.append(jnp.where(msk, s, NEG))
        mx = sink
        for s in scores:
            mx = jnp.maximum(mx, jnp.max(s, axis=-1, keepdims=True))
        denom = jnp.exp(sink - mx)
        acc = None
        for s, (_, vb, _, _) in zip(scores, segs):
            p = jnp.exp(s - mx)
            denom = denom + jnp.sum(p, axis=-1, keepdims=True)
            pv = _dot(p.astype(BF16), jnp.where(head, vb, jnp.zeros((), BF16)))
            acc = pv if acc is None else acc + pv
        acc = acc * (1.0 / denom)
        out = acc if out is None else out + acc
    return jnp.concatenate([out[g * mq:(g + 1) * mq, :] for g in range(GROUP)], axis=1)


def _queries(x, g, wq_ref, cos, sin):
    q = _dot(_rms(x, g).astype(BF16), wq_ref[...])
    return (_rope(q, cos, sin) * (HEAD_DIM ** -0.5)).astype(BF16)


def _attn_main_kernel(sinks_ref, xm_ref, g_ref, wq_ref, wo_ref, cos_ref, sin_ref,
                      km_ref, vm_ref, kt_ref, vt_ref,
                      om_ref, q_ref, ao_ref, kbuf_ref, vbuf_ref):
    i = pl.program_id(0)
    start = i % TILES_PER_SEQ == 0

    x = xm_ref[...]
    q_ref[...] = _queries(x, g_ref[...], wq_ref, cos_ref[...], sin_ref[...])

    @pl.when(start)
    def _():
        pad = jnp.zeros((BLK - N_META, KV_W), BF16)
        kbuf_ref[0:BLK - N_META, :] = pad
        vbuf_ref[0:BLK - N_META, :] = pad
        kbuf_ref[BLK - N_META:BLK, :] = kt_ref[N_SAMPLE:N_TAIL, :].astype(BF16)
        vbuf_ref[BLK - N_META:BLK, :] = vt_ref[N_SAMPLE:N_TAIL, :].astype(BF16)

    @pl.when(jnp.logical_not(start))
    def _():
        kbuf_ref[0:BLK, :] = kbuf_ref[TM:TM + BLK, :]
        vbuf_ref[0:BLK, :] = vbuf_ref[TM:TM + BLK, :]

    kbuf_ref[BLK:BLK + TM, :] = km_ref[...].astype(BF16)
    vbuf_ref[BLK:BLK + TM, :] = vm_ref[...].astype(BF16)

    qi = lax.broadcasted_iota(jnp.int32, (BLK, 2 * BLK), 0)
    kj = lax.broadcasted_iota(jnp.int32, (BLK, 2 * BLK), 1)
    band = (kj >= qi) & (kj <= qi + WINDOW)
    first_key = jnp.where(start, BLK - N_META, 0)
    for blk in range(BLKS_PER_TILE):
        lo = blk * BLK
        mask = band & (kj >= first_key) if blk == 0 else band
        att = _attend(q_ref[lo:lo + BLK, :],
                      [(kbuf_ref[lo:lo + 2 * BLK, :], vbuf_ref[lo:lo + 2 * BLK, :], mask, False)],
                      sinks_ref)
        ao_ref[lo:lo + BLK, :] = att.astype(BF16)
    om_ref[...] = x + _dot(ao_ref[...], wo_ref[...])


def _attn_main_call(sinks, xm, g, wq, wo, cos_m, sin_m, km, vm, kt, vt):
    tile = lambda w: pl.BlockSpec((TM, w), lambda i: (i, 0))
    pos = pl.BlockSpec((TM, LANES), lambda i: (i % TILES_PER_SEQ, 0))
    return pl.pallas_call(
        _attn_main_kernel,
        grid=(N_TILES,),
        in_specs=[pl.BlockSpec(memory_space=pltpu.SMEM),
                  tile(D_MODEL), _resident((1, D_MODEL)),
                  _resident((D_MODEL, D_MODEL)), _resident((D_MODEL, D_MODEL)),
                  pos, pos, tile(KV_W), tile(KV_W),
                  _resident((N_TAIL, KV_W)), _resident((N_TAIL, KV_W))],
        out_specs=tile(D_MODEL),
        out_shape=jax.ShapeDtypeStruct((N_MAIN, D_MODEL), F32),
        scratch_shapes=[pltpu.VMEM((TM, D_MODEL), BF16), pltpu.VMEM((TM, D_MODEL), BF16),
                        pltpu.VMEM((BLK + TM, KV_W), BF16), pltpu.VMEM((BLK + TM, KV_W), BF16)],
        compiler_params=_params(),
        name="attn_main",
    )(sinks, xm, g.reshape(1, D_MODEL), wq, wo, cos_m, sin_m, km, vm, kt, vt)


def _attn_tail_kernel(sinks_ref, xt_ref, g_ref, wq_ref, wo_ref, cos_ref, sin_ref,
                      kt_ref, vt_ref, ck_ref, cv_ref, ot_ref, q_ref, ao_ref):
    s = pl.program_id(0)

    @pl.when(s == 0)
    def _():
        q_ref[...] = _queries(xt_ref[...], g_ref[...], wq_ref, cos_ref[...], sin_ref[...])
        qi = lax.broadcasted_iota(jnp.int32, (N_META, N_META), 0)
        kj = lax.broadcasted_iota(jnp.int32, (N_META, N_META), 1)
        km = kt_ref[N_SAMPLE:N_TAIL, :].astype(BF16)
        vm = vt_ref[N_SAMPLE:N_TAIL, :].astype(BF16)
        att = _attend(q_ref[N_SAMPLE:N_TAIL, :], [(km, vm, kj <= qi, False)], sinks_ref)
        ao_ref[N_SAMPLE:N_TAIL, :] = att.astype(BF16)

    rows = pl.ds(pl.multiple_of(s * GROUP_ROWS, GROUP_ROWS), GROUP_ROWS)
    k_new = kt_ref[rows, :]
    v_new = vt_ref[rows, :]
    ck = [ck_ref[b].reshape(KV_W, WINDOW) for b in range(SAMPLE_GROUP)]
    cv = [cv_ref[b].reshape(KV_W, WINDOW) for b in range(SAMPLE_GROUP)]
    k_cached_t = jnp.concatenate([_rows_heads_to_split(c) for c in ck], axis=1).astype(BF16)
    v_cached = jnp.concatenate([c.T for c in cv], axis=0).astype(BF16)

    n_cached = SAMPLE_GROUP * WINDOW
    qr = lax.broadcasted_iota(jnp.int32, (GROUP_ROWS, n_cached), 0)
    kc = lax.broadcasted_iota(jnp.int32, (GROUP_ROWS, n_cached), 1)
    mask_cached = ((kc // WINDOW) == (qr // DEC_SEQ)) & ((kc % WINDOW) >= (qr % DEC_SEQ))
    qn = lax.broadcasted_iota(jnp.int32, (GROUP_ROWS, GROUP_ROWS), 0)
    kn = lax.broadcasted_iota(jnp.int32, (GROUP_ROWS, GROUP_ROWS), 1)
    mask_new = ((kn // DEC_SEQ) == (qn // DEC_SEQ)) & ((kn % DEC_SEQ) <= (qn % DEC_SEQ))
    att = _attend(q_ref[rows, :],
                  [(k_cached_t, v_cached, mask_cached, True),
                   (k_new.astype(BF16), v_new.astype(BF16), mask_new, False)],
                  sinks_ref)
    ao_ref[rows, :] = att.astype(BF16)

    @pl.when(s == N_GROUPS - 1)
    def _():
        ot_ref[...] = xt_ref[...] + _dot(ao_ref[...], wo_ref[...])


def _attn_tail_call(sinks, xt, g, wq, wo, cos_t, sin_t, kt, vt, ck, cv):
    cached = pl.BlockSpec((SAMPLE_GROUP, N_KV, HEAD_DIM, WINDOW), lambda s: (s, 0, 0, 0))
    return pl.pallas_call(
        _attn_tail_kernel,
        grid=(N_GROUPS,),
        in_specs=[pl.BlockSpec(memory_space=pltpu.SMEM),
                  _resident((N_TAIL, D_MODEL)), _resident((1, D_MODEL)),
                  _resident((D_MODEL, D_MODEL)), _resident((D_MODEL, D_MODEL)),
                  _resident((N_TAIL, LANES)), _resident((N_TAIL, LANES)),
                  _resident((N_TAIL, KV_W)), _resident((N_TAIL, KV_W)),
                  cached, cached],
        out_specs=pl.BlockSpec((N_TAIL, D_MODEL), lambda s: (0, 0)),
        out_shape=jax.ShapeDtypeStruct((N_TAIL, D_MODEL), F32),
        scratch_shapes=[pltpu.VMEM((N_TAIL, D_MODEL), BF16), pltpu.VMEM((N_TAIL, D_MODEL), BF16)],
        compiler_params=_params(),
        name="attn_tail",
    )(sinks, xt, g.reshape(1, D_MODEL), wq, wo, cos_t, sin_t, kt, vt, ck, cv)


def _rope_tables(pos):
    inv = 1.0 / (ROPE_THETA ** (jnp.arange(0, HEAD_DIM, 2, dtype=F32) / HEAD_DIM))
    ang = pos.astype(F32)[:, None] * inv[None, :]
    cos, sin = jnp.cos(ang), jnp.sin(ang)
    return jnp.tile(cos, (1, N_KV)), jnp.tile(sin, (1, N_KV))


def kernel(x_prompt, x_sample, state_conv, cache_k, cache_v, meta_tokens, norm_g, ffn_w_in,
           ffn_w_out, conv_w_in, conv_kernel, conv_w_out, kv_norm_g, w_kv, w_q, w_o, sinks,
           final_norm_g):
    assert x_prompt.shape == (BATCH, SEQ, D_MODEL) and x_sample.shape == (DEC_BATCH, DEC_SEQ, D_MODEL)
    assert cache_k.shape == (DEC_BATCH, WINDOW, N_KV, HEAD_DIM)
    assert norm_g.shape[0] == 2 and conv_w_in.shape[0] == 1 and w_q.shape[0] == 1

    xm = x_prompt.reshape(N_MAIN, D_MODEL)
    xt = jnp.concatenate([x_sample.reshape(N_SAMPLE, D_MODEL), meta_tokens], axis=0)

    wq = w_q[0].reshape(D_MODEL, N_KV, GROUP, 2, HALF_DIM).transpose(0, 2, 3, 1, 4)
    wq = wq.reshape(D_MODEL, D_MODEL).astype(BF16)
    wo = w_o[0].reshape(N_KV, GROUP, HEAD_DIM, D_MODEL).transpose(1, 0, 2, 3)
    wo = wo.reshape(D_MODEL, D_MODEL).astype(BF16)
    wk = w_kv[:, :KV_W].reshape(D_MODEL, N_KV, 2, HALF_DIM).transpose(0, 2, 1, 3)
    wkv = jnp.concatenate([wk.reshape(D_MODEL, KV_W), w_kv[:, KV_W:]], axis=1).astype(BF16)

    pos_main = N_META + jnp.arange(SEQ, dtype=jnp.int32)
    pos_tail = jnp.concatenate([
        jnp.tile(PAST_LEN + jnp.arange(DEC_SEQ, dtype=jnp.int32), DEC_BATCH),
        jnp.arange(N_META, dtype=jnp.int32)])
    cos_m, sin_m = _rope_tables(pos_main)
    cos_t, sin_t = _rope_tables(pos_tail)

    sc = state_conv[0]
    grow = lambda a: jnp.pad(a, ((0, 0), (0, DEC_SEQ - a.shape[1]), (0, 0))).reshape(N_SAMPLE, D_MODEL)
    s1 = grow(sc[:, 1:])
    s2 = grow(sc)

    ffn_casts = lambda l, j: [(ffn_w_in, (l, j), N_TILES), (ffn_w_out, (l, j), N_TILES // 2)]
    slot_minor = lambda a: a.transpose(0, 2, 3, 1)
    ck, cv = slot_minor(cache_k), slot_minor(cache_v)

    xm, xt, cw_in, cw_out, w_in, w_out = _ffn_call(
        xm, xt, norm_g[0, 0], ffn_w_in[0, 0].astype(BF16), ffn_w_out[0, 0].astype(BF16),
        name="ffn_l0a",
        casts=[(conv_w_in, (0,), N_TILES), (conv_w_out, (0,), N_TILES)] + ffn_casts(0, 1))
    xm, xt, u_tail, sp = _conv_call(xm, xt, norm_g[0, 1], cw_in, conv_kernel[0], cw_out, s1, s2)
    xm, xt, w_in, w_out = _ffn_call(
        xm, xt, norm_g[0, 2], w_in, w_out, name="ffn_l0b", casts=ffn_casts(1, 0))

    xm, xt, km, vm, kt, vt, k_last, v_last, nk, nv, w_in, w_out = _ffn_call(
        xm, xt, norm_g[1, 0], w_in, w_out, name="ffn_l1a_kv",
        kv=(kv_norm_g, wkv, cos_m, sin_m, cos_t, sin_t, ck, cv), casts=ffn_casts(1, 1))
    xt = _attn_tail_call(sinks[0], xt, norm_g[1, 1], wq, wo, cos_t, sin_t, kt, vt, ck, cv)
    xm = _attn_main_call(sinks[0], xm, norm_g[1, 1], wq, wo, cos_m, sin_m, km, vm, kt, vt)
    ym, yt = _ffn_call(xm, xt, norm_g[1, 2], w_in, w_out, name="ffn_l1b_final",
                       final_g=final_norm_g)

    y_prompt = ym.reshape(BATCH, SEQ, D_MODEL)
    y_sample = yt[:N_SAMPLE].reshape(DEC_BATCH, DEC_SEQ, D_MODEL)
    new_state_conv_p = sp[:, CARRY - (CONV_W - 1):][None]
    new_state_conv_s = u_tail[:N_SAMPLE].reshape(DEC_BATCH, DEC_SEQ, D_MODEL)[:, DEC_SEQ - (CONV_W - 1):][None]
    slot_major = lambda a: a.transpose(0, 3, 1, 2)
    new_k_p = slot_major(k_last.reshape(BATCH, N_KV, HEAD_DIM, WINDOW))
    new_v_p = slot_major(v_last.reshape(BATCH, N_KV, HEAD_DIM, WINDOW))
    return (y_prompt, y_sample, new_state_conv_p, new_state_conv_s,
            new_k_p, new_v_p, slot_major(nk), slot_major(nv))
```

```python
import jax
import jax.numpy as jnp
from jax import lax
from jax.experimental import pallas as pl
from jax.experimental.pallas import tpu as pltpu

D_MODEL = 1024
BATCH = 2
SEQ = 8192
DEC_BATCH = 128
DEC_SEQ = 4
PAST_LEN = 8192
N_META = 16
D_FF = 2816
CONV_W = 3
HEAD_DIM = 64
N_HEADS = 16
N_KV = 4
GROUP = 4
WINDOW = 128
ROPE_THETA = 10000.0
EPS = 1e-6
NEG = -1e30
LOG2E = 1.4426950408889634

KV_W = N_KV * HEAD_DIM
HALF_DIM = HEAD_DIM // 2
N_MAIN = BATCH * SEQ
N_SAMPLE = DEC_BATCH * DEC_SEQ
N_TAIL = N_SAMPLE + N_META
TM = 512
N_TILES = N_MAIN // TM
TILES_PER_SEQ = SEQ // TM
FF_CHUNK = 256
OUT_CHUNK = 256
BLK = WINDOW
BLKS_PER_TILE = TM // BLK
SAMPLE_GROUP = 8
N_GROUPS = DEC_BATCH // SAMPLE_GROUP
CACHE_SEQS = DEC_BATCH // N_TILES
GROUP_ROWS = SAMPLE_GROUP * DEC_SEQ
LANES = 128
BF16_ROWS = 16
V7X_VMEM_LIMIT = 56 * 1024 * 1024

F32 = jnp.float32
BF16 = jnp.bfloat16


def _rms(x, g):
    return x * lax.rsqrt(jnp.mean(x * x, axis=-1, keepdims=True) + EPS) * g


def _rope(x, cos, sin):
    outs = []
    for c in range(x.shape[1] // KV_W):
        x1 = x[:, c * KV_W:c * KV_W + LANES]
        x2 = x[:, c * KV_W + LANES:(c + 1) * KV_W]
        outs += [x1 * cos - x2 * sin, x2 * cos + x1 * sin]
    return jnp.concatenate(outs, axis=1)


def _rows_split_to_heads(t):
    return jnp.concatenate([t[half * LANES + h * HALF_DIM:half * LANES + (h + 1) * HALF_DIM]
                            for h in range(N_KV) for half in range(2)], axis=0)


def _rows_heads_to_split(t):
    return jnp.concatenate([t[h * HEAD_DIM + half * HALF_DIM:h * HEAD_DIM + (half + 1) * HALF_DIM]
                            for half in range(2) for h in range(N_KV)], axis=0)


def _dot(a, b):
    return jnp.dot(a, b, preferred_element_type=F32)


def _dot_nt(a, b):
    return lax.dot_general(a, b, (((1,), (1,)), ((), ())), preferred_element_type=F32)


def _resident(shape):
    nd = len(shape)
    return pl.BlockSpec(shape, lambda i: (0,) * nd, pipeline_mode=pl.Buffered(1))


def _main_tile(i):
    return jnp.maximum(i - 1, 0)


def _main_spec(width):
    return pl.BlockSpec((TM, width), lambda i: (_main_tile(i), 0))


def _main_pos_spec():
    return pl.BlockSpec((TM, LANES), lambda i: (_main_tile(i) % TILES_PER_SEQ, 0))


def _params():
    return pltpu.CompilerParams(dimension_semantics=("arbitrary",),
                                vmem_limit_bytes=V7X_VMEM_LIMIT)


def _ffn_rows(x_ref, o_ref, g, w_in_ref, w_out_ref, a_ref, final_g=None):
    x = x_ref[...]
    m = x.shape[0]
    hn = _rms(x, g).astype(BF16)
    for c in range(D_FF // FF_CHUNK):
        lo = c * FF_CHUNK
        gate = _dot(hn, w_in_ref[:, lo:lo + FF_CHUNK])
        up = _dot(hn, w_in_ref[:, D_FF + lo:D_FF + lo + FF_CHUNK])
        a_ref[0:m, lo:lo + FF_CHUNK] = (gate * jax.nn.sigmoid(gate) * up).astype(BF16)
    sq = None
    for c in range(D_MODEL // OUT_CHUNK):
        cols = slice(c * OUT_CHUNK, (c + 1) * OUT_CHUNK)
        y = x[:, cols] + 0.5 * _dot(a_ref[0:m, :], w_out_ref[:, cols])
        o_ref[:, cols] = y
        if final_g is not None:
            part = jnp.sum(y * y, axis=-1, keepdims=True)
            sq = part if sq is None else sq + part
    if final_g is not None:
        o_ref[...] = o_ref[...] * lax.rsqrt(sq * (1.0 / D_MODEL) + EPS) * final_g


def _make_ffn_kernel(has_final, has_kv, n_cast):
    def body(*refs):
        refs = list(refs)
        xm_ref, xt_ref, g_ref, w_in_ref, w_out_ref = refs[:5]
        pos = 5
        if has_final:
            gf_ref = refs[pos]
            pos += 1
        if has_kv:
            gkv_ref, wkv_ref, cm_ref, sm_ref, ct_ref, st_ref, ck_ref, cv_ref = refs[pos:pos + 8]
            pos += 8
        cast_src = refs[pos:pos + n_cast]
        pos += n_cast
        om_ref, ot_ref = refs[pos:pos + 2]
        pos += 2
        if has_kv:
            km_ref, vm_ref, kt_ref, vt_ref, kl_ref, vl_ref, nk_ref, nv_ref = refs[pos:pos + 8]
            pos += 8
        cast_dst = refs[pos:pos + n_cast]
        pos += n_cast
        a_ref = refs[pos]
        if has_kv:
            newk_ref, newv_ref = refs[pos + 1:pos + 3]
        i = pl.program_id(0)

        for src, dst in zip(cast_src, cast_dst):
            dst[...] = src[...].astype(BF16)

        def rows(x_ref, o_ref):
            _ffn_rows(x_ref, o_ref, g_ref[...], w_in_ref, w_out_ref, a_ref,
                      gf_ref[...] if has_final else None)

        def project_kv(x_ref, cos_ref, sin_ref, k_ref, v_ref):
            kv = _dot(_rms(x_ref[...], gkv_ref[...]).astype(BF16), wkv_ref[...])
            k = _rope(kv[:, :KV_W], cos_ref[...], sin_ref[...])
            v = kv[:, KV_W:]
            k_ref[...] = k
            v_ref[...] = v
            return k, v

        @pl.when(i == 0)
        def _():
            if has_kv:
                k, v = project_kv(xt_ref, ct_ref, st_ref, kt_ref, vt_ref)
                for c in range(N_SAMPLE // LANES):
                    newk_ref[c] = _rows_split_to_heads(k[c * LANES:(c + 1) * LANES, :].T)
                    newv_ref[c] = v[c * LANES:(c + 1) * LANES, :].T
            rows(xt_ref, ot_ref)

        @pl.when(i > 0)
        def _():
            if has_kv:
                t = i - 1
                k, v = project_kv(xm_ref, cm_ref, sm_ref, km_ref, vm_ref)
                kl_ref[...] = _rows_split_to_heads(k[TM - WINDOW:, :].T)
                vl_ref[...] = v[TM - WINDOW:, :].T

                lane = lax.broadcasted_iota(jnp.int32, (1, WINDOW), 1)
                is_new = lane >= WINDOW - DEC_SEQ
                first_lane = (t * CACHE_SEQS * DEC_SEQ) % LANES
                for new_ref, c_ref, n_ref in ((newk_ref, ck_ref, nk_ref), (newv_ref, cv_ref, nv_ref)):
                    new_t = new_ref[(t * CACHE_SEQS * DEC_SEQ) // LANES]
                    for b in range(CACHE_SEQS):
                        shifted = pltpu.roll(c_ref[b].reshape(KV_W, WINDOW), WINDOW - DEC_SEQ, 1)
                        amount = (WINDOW - DEC_SEQ - b * DEC_SEQ - first_lane) & (LANES - 1)
                        placed = pltpu.roll(new_t, amount, 1)
                        n_ref[b] = jnp.where(is_new, placed, shifted).reshape(N_KV, HEAD_DIM, WINDOW)
            rows(xm_ref, om_ref)

    return body


def _cast_specs(casts):
    in_specs, out_specs, out_shape, args = [], [], [], []
    for arr, lead, steps in casts:
        n_rows, n_cols = arr.shape[-2:]
        rows = n_rows // steps
        assert rows * steps == n_rows and rows % BF16_ROWS == 0
        blk = lambda i, steps=steps: jnp.minimum(_main_tile(i), steps - 1)
        in_specs.append(pl.BlockSpec((None,) * len(lead) + (rows, n_cols),
                                     lambda i, lead=lead, blk=blk: lead + (blk(i), 0)))
        out_specs.append(pl.BlockSpec((rows, n_cols), lambda i, blk=blk: (blk(i), 0)))
        out_shape.append(jax.ShapeDtypeStruct((n_rows, n_cols), BF16))
        args.append(arr)
    return in_specs, out_specs, out_shape, args


def _ffn_call(xm, xt, g, w_in, w_out, *, name, final_g=None, kv=None, casts=()):
    in_specs = [_main_spec(D_MODEL), _resident((N_TAIL, D_MODEL)), _resident((1, D_MODEL)),
                _resident((D_MODEL, 2 * D_FF)), _resident((D_FF, D_MODEL))]
    args = [xm, xt, g.reshape(1, D_MODEL), w_in, w_out]
    out_shape = [jax.ShapeDtypeStruct((N_MAIN, D_MODEL), F32),
                 jax.ShapeDtypeStruct((N_TAIL, D_MODEL), F32)]
    out_specs = [_main_spec(D_MODEL), pl.BlockSpec((N_TAIL, D_MODEL), lambda i: (0, 0))]
    if final_g is not None:
        in_specs.append(_resident((1, D_MODEL)))
        args.append(final_g.reshape(1, D_MODEL))
    scratch = [pltpu.VMEM((N_TAIL, D_FF), BF16)]
    if kv is not None:
        g_kv, w_kv, cos_m, sin_m, cos_t, sin_t, ck, cv = kv
        cached = pl.BlockSpec((CACHE_SEQS, N_KV, HEAD_DIM, WINDOW), lambda i: (_main_tile(i), 0, 0, 0))
        in_specs += [_resident((1, D_MODEL)), _resident((D_MODEL, 2 * KV_W)),
                     _main_pos_spec(), _main_pos_spec(),
                     _resident((N_TAIL, LANES)), _resident((N_TAIL, LANES)), cached, cached]
        args += [g_kv.reshape(1, D_MODEL), w_kv, cos_m, sin_m, cos_t, sin_t, ck, cv]
        out_shape += [jax.ShapeDtypeStruct((N_MAIN, KV_W), F32)] * 2
        out_shape += [jax.ShapeDtypeStruct((N_TAIL, KV_W), F32)] * 2
        out_shape += [jax.ShapeDtypeStruct((BATCH, KV_W, WINDOW), F32)] * 2
        out_shape += [jax.ShapeDtypeStruct((DEC_BATCH, N_KV, HEAD_DIM, WINDOW), F32)] * 2
        last = pl.BlockSpec((None, KV_W, WINDOW), lambda i: (_main_tile(i) // TILES_PER_SEQ, 0, 0))
        out_specs += [_main_spec(KV_W), _main_spec(KV_W),
                      pl.BlockSpec((N_TAIL, KV_W), lambda i: (0, 0)),
                      pl.BlockSpec((N_TAIL, KV_W), lambda i: (0, 0)), last, last, cached, cached]
        scratch += [pltpu.VMEM((N_SAMPLE // LANES, KV_W, LANES), F32)] * 2
    c_in, c_out, c_shape, c_args = _cast_specs(casts)
    return pl.pallas_call(
        _make_ffn_kernel(final_g is not None, kv is not None, len(casts)),
        grid=(1 + N_TILES,),
        in_specs=in_specs + c_in,
        out_specs=out_specs + c_out,
        out_shape=out_shape + c_shape,
        scratch_shapes=scratch,
        compiler_params=_params(),
        name=name,
    )(*args, *c_args)


CARRY = 8
CONV_TM = 1024


def _conv_rows(x_ref, o_ref, g, w_in_ref, kern_ref, w_out_ref, ubuf_ref, mbuf_ref, fix):
    x = x_ref[...]
    m = x.shape[0]
    hn = _rms(x, g).astype(BF16)
    for c in range(D_MODEL // OUT_CHUNK):
        cols = slice(c * OUT_CHUNK, (c + 1) * OUT_CHUNK)
        b, cc, z = (_dot(hn, w_in_ref[:, j * D_MODEL + c * OUT_CHUNK:j * D_MODEL + (c + 1) * OUT_CHUNK])
                    for j in range(3))
        u = cc * z
        ubuf_ref[CARRY:CARRY + m, cols] = u
        prev1 = ubuf_ref[CARRY - 1:CARRY - 1 + m, cols]
        prev2 = ubuf_ref[CARRY - 2:CARRY - 2 + m, cols]
        if fix is not None:
            prev1, prev2 = fix(prev1, prev2, cols)
        conv = kern_ref[0:1, cols] * prev2 + kern_ref[1:2, cols] * prev1 + kern_ref[2:3, cols] * u
        mbuf_ref[0:m, cols] = (b * conv).astype(BF16)
    for c in range(D_MODEL // OUT_CHUNK):
        cols = slice(c * OUT_CHUNK, (c + 1) * OUT_CHUNK)
        o_ref[:, cols] = x[:, cols] + _dot(mbuf_ref[0:m, :], w_out_ref[:, cols])


def _conv_kernel(xm_ref, xt_ref, g_ref, w_in_ref, kern_ref, w_out_ref, s1_ref, s2_ref,
                 om_ref, ot_ref, ut_ref, sp_ref, ubuf_ref, meta_ref, mbuf_ref):
    i = pl.program_id(0)

    @pl.when(i == 0)
    def _():
        ubuf_ref[0:CARRY, :] = jnp.zeros((CARRY, D_MODEL), F32)

        def fix(prev1, prev2, cols):
            step = lax.broadcasted_iota(jnp.int32, (N_SAMPLE, 1), 0) & (DEC_SEQ - 1)
            meta = lax.broadcasted_iota(jnp.int32, (N_META, 1), 0)
            p1 = [jnp.where(step >= 1, prev1[:N_SAMPLE], s1_ref[:, cols]),
                  jnp.where(meta >= 1, prev1[N_SAMPLE:], 0.0)]
            p2 = [jnp.where(step >= 2, prev2[:N_SAMPLE], s2_ref[:, cols]),
                  jnp.where(meta >= 2, prev2[N_SAMPLE:], 0.0)]
            return jnp.concatenate(p1, axis=0), jnp.concatenate(p2, axis=0)

        _conv_rows(xt_ref, ot_ref, g_ref[...], w_in_ref, kern_ref, w_out_ref, ubuf_ref, mbuf_ref, fix)
        ut_ref[...] = ubuf_ref[CARRY:CARRY + N_TAIL, :]
        meta_ref[...] = ubuf_ref[N_TAIL:N_TAIL + CARRY, :]

    @pl.when(i > 0)
    def _():
        t = i - 1
        per_seq = SEQ // CONV_TM

        @pl.when(t % per_seq == 0)
        def _():
            ubuf_ref[0:CARRY, :] = meta_ref[...]

        _conv_rows(xm_ref, om_ref, g_ref[...], w_in_ref, kern_ref, w_out_ref, ubuf_ref, mbuf_ref, None)
        last = ubuf_ref[CONV_TM:CONV_TM + CARRY, :]
        ubuf_ref[0:CARRY, :] = last

        @pl.when(t % per_seq == per_seq - 1)
        def _():
            sp_ref[t // per_seq] = last


def _conv_call(xm, xt, g, w_in, kern, w_out, s1, s2):
    main = pl.BlockSpec((CONV_TM, D_MODEL), lambda i: (_main_tile(i), 0))
    return pl.pallas_call(
        _conv_kernel,
        grid=(1 + N_MAIN // CONV_TM,),
        in_specs=[main, _resident((N_TAIL, D_MODEL)), _resident((1, D_MODEL)),
                  _resident((D_MODEL, 3 * D_MODEL)), _resident((CONV_W, D_MODEL)),
                  _resident((D_MODEL, D_MODEL)),
                  _resident((N_SAMPLE, D_MODEL)), _resident((N_SAMPLE, D_MODEL))],
        out_specs=[main,
                   pl.BlockSpec((N_TAIL, D_MODEL), lambda i: (0, 0)),
                   pl.BlockSpec((N_TAIL, D_MODEL), lambda i: (0, 0)),
                   pl.BlockSpec((BATCH, CARRY, D_MODEL), lambda i: (0, 0, 0))],
        out_shape=[jax.ShapeDtypeStruct((N_MAIN, D_MODEL), F32),
                   jax.ShapeDtypeStruct((N_TAIL, D_MODEL), F32),
                   jax.ShapeDtypeStruct((N_TAIL, D_MODEL), F32),
                   jax.ShapeDtypeStruct((BATCH, CARRY, D_MODEL), F32)],
        scratch_shapes=[pltpu.VMEM((CARRY + max(N_TAIL, CONV_TM), D_MODEL), F32),
                        pltpu.VMEM((CARRY, D_MODEL), F32),
                        pltpu.VMEM((max(N_TAIL, CONV_TM), D_MODEL), BF16)],
        compiler_params=_params(),
        name="conv_mixer",
    )(xm, xt, g.reshape(1, D_MODEL), w_in, kern, w_out, s1, s2)


def _attend(qp, segs, sinks_ref, heads_per_pass=1):
    mq = qp.shape[0]
    rows = GROUP * mq
    lane = lax.broadcasted_iota(jnp.int32, (1, KV_W), 1)
    qk_head = (lane & (LANES - 1)) // HALF_DIM
    pair_head = lax.broadcasted_iota(jnp.int32, (1, LANES), 1) // HEAD_DIM
    row_head = lax.broadcasted_iota(jnp.int32, (heads_per_pass * rows, 1), 0) // mq
    masks = [jnp.concatenate([seg[2]] * (heads_per_pass * GROUP), axis=0) for seg in segs]
    outs = []
    for k0 in range(0, N_KV, heads_per_pass):
        heads = range(k0, k0 + heads_per_pass)
        qs = jnp.concatenate(
            [jnp.where(qk_head == k, qp[:, g * KV_W:(g + 1) * KV_W], jnp.zeros((), BF16))
             for k in heads for g in range(GROUP)], axis=0)
        sink = jnp.full((heads_per_pass * rows, 1), sinks_ref[k0 * GROUP] * LOG2E, F32)
        for j in range(1, heads_per_pass * GROUP):
            sink = jnp.where(row_head == j, sinks_ref[k0 * GROUP + j] * LOG2E, sink)
        scores = []
        for (kb, _, _, k_transposed), msk in zip(segs, masks):
            s = _dot(qs, kb) if k_transposed else _dot_nt(qs, kb)
            scores.append(jnp.where(msk, s, NEG))
        mx = sink
        for s in scores:
            mx = jnp.maximum(mx, jnp.max(s, axis=-1, keepdims=True))
        denom = jnp.exp2(sink - mx)
        probs = []
        for s in scores:
            p = jnp.exp2(s - mx)
            denom = denom + jnp.sum(p, axis=-1, keepdims=True)
            probs.append(p.astype(BF16))
        inv = 1.0 / denom
        for n, k in enumerate(heads):
            pair = slice((k // 2) * LANES, (k // 2 + 1) * LANES)
            part = slice(n * rows, (n + 1) * rows)
            acc = None
            for p, (_, vb, _, _) in zip(probs, segs):
                pv = _dot(p[part], jnp.where(pair_head == k % 2, vb[:, pair], jnp.zeros((), BF16)))
                acc = pv if acc is None else acc + pv
            outs.append(acc * inv[part])
    out = jnp.concatenate([outs[k] + outs[k + 1] for k in range(0, N_KV, 2)], axis=1)
    return jnp.concatenate([out[g * mq:(g + 1) * mq, :] for g in range(GROUP)], axis=1)


def _queries(x, g, wq_ref, cos, sin):
    q = _dot(_rms(x, g).astype(BF16), wq_ref[...])
    return (_rope(q, cos, sin) * (HEAD_DIM ** -0.5 * LOG2E)).astype(BF16)


def _attn_main_kernel(sinks_ref, xm_ref, g_ref, wq_ref, wo_ref, cos_ref, sin_ref,
                      km_ref, vm_ref, kt_ref, vt_ref,
                      om_ref, q_ref, ao_ref, kbuf_ref, vbuf_ref):
    i = pl.program_id(0)
    start = i % TILES_PER_SEQ == 0

    x = xm_ref[...]
    q_ref[...] = _queries(x, g_ref[...], wq_ref, cos_ref[...], sin_ref[...])

    @pl.when(start)
    def _():
        pad = jnp.zeros((BLK - N_META, KV_W), BF16)
        kbuf_ref[0:BLK - N_META, :] = pad
        vbuf_ref[0:BLK - N_META, :] = pad
        kbuf_ref[BLK - N_META:BLK, :] = kt_ref[N_SAMPLE:N_TAIL, :].astype(BF16)
        vbuf_ref[BLK - N_META:BLK, :] = vt_ref[N_SAMPLE:N_TAIL, :].astype(BF16)

    @pl.when(jnp.logical_not(start))
    def _():
        kbuf_ref[0:BLK, :] = kbuf_ref[TM:TM + BLK, :]
        vbuf_ref[0:BLK, :] = vbuf_ref[TM:TM + BLK, :]

    kbuf_ref[BLK:BLK + TM, :] = km_ref[...].astype(BF16)
    vbuf_ref[BLK:BLK + TM, :] = vm_ref[...].astype(BF16)

    qi = lax.broadcasted_iota(jnp.int32, (BLK, 2 * BLK), 0)
    kj = lax.broadcasted_iota(jnp.int32, (BLK, 2 * BLK), 1)
    band = (kj >= qi) & (kj <= qi + WINDOW)
    first_key = jnp.where(start, BLK - N_META, 0)
    for blk in range(BLKS_PER_TILE):
        lo = blk * BLK
        mask = band & (kj >= first_key) if blk == 0 else band
        att = _attend(q_ref[lo:lo + BLK, :],
                      [(kbuf_ref[lo:lo + 2 * BLK, :], vbuf_ref[lo:lo + 2 * BLK, :], mask, False)],
                      sinks_ref)
        ao_ref[lo:lo + BLK, :] = att.astype(BF16)
    om_ref[...] = x + _dot(ao_ref[...], wo_ref[...])


def _attn_main_call(sinks, xm, g, wq, wo, cos_m, sin_m, km, vm, kt, vt):
    tile = lambda w: pl.BlockSpec((TM, w), lambda i: (i, 0))
    pos = pl.BlockSpec((TM, LANES), lambda i: (i % TILES_PER_SEQ, 0))
    return pl.pallas_call(
        _attn_main_kernel,
        grid=(N_TILES,),
        in_specs=[pl.BlockSpec(memory_space=pltpu.SMEM),
                  tile(D_MODEL), _resident((1, D_MODEL)),
                  _resident((D_MODEL, D_MODEL)), _resident((D_MODEL, D_MODEL)),
                  pos, pos, tile(KV_W), tile(KV_W),
                  _resident((N_TAIL, KV_W)), _resident((N_TAIL, KV_W))],
        out_specs=tile(D_MODEL),
        out_shape=jax.ShapeDtypeStruct((N_MAIN, D_MODEL), F32),
        scratch_shapes=[pltpu.VMEM((TM, D_MODEL), BF16), pltpu.VMEM((TM, D_MODEL), BF16),
                        pltpu.VMEM((BLK + TM, KV_W), BF16), pltpu.VMEM((BLK + TM, KV_W), BF16)],
        compiler_params=_params(),
        name="attn_main",
    )(sinks, xm, g.reshape(1, D_MODEL), wq, wo, cos_m, sin_m, km, vm, kt, vt)


def _attn_tail_kernel(sinks_ref, xt_ref, g_ref, wq_ref, wo_ref, cos_ref, sin_ref,
                      kt_ref, vt_ref, ck_ref, cv_ref, ot_ref, q_ref, ao_ref):
    s = pl.program_id(0)

    @pl.when(s == 0)
    def _():
        q_ref[...] = _queries(xt_ref[...], g_ref[...], wq_ref, cos_ref[...], sin_ref[...])
        qi = lax.broadcasted_iota(jnp.int32, (N_META, N_META), 0)
        kj = lax.broadcasted_iota(jnp.int32, (N_META, N_META), 1)
        km = kt_ref[N_SAMPLE:N_TAIL, :].astype(BF16)
        vm = vt_ref[N_SAMPLE:N_TAIL, :].astype(BF16)
        att = _attend(q_ref[N_SAMPLE:N_TAIL, :], [(km, vm, kj <= qi, False)], sinks_ref, N_KV)
        ao_ref[N_SAMPLE:N_TAIL, :] = att.astype(BF16)

    rows = pl.ds(pl.multiple_of(s * GROUP_ROWS, GROUP_ROWS), GROUP_ROWS)
    k_new = kt_ref[rows, :]
    v_new = vt_ref[rows, :]
    ck = [ck_ref[b].reshape(KV_W, WINDOW) for b in range(SAMPLE_GROUP)]
    cv = [cv_ref[b].reshape(KV_W, WINDOW) for b in range(SAMPLE_GROUP)]
    k_cached_t = jnp.concatenate([_rows_heads_to_split(c) for c in ck], axis=1).astype(BF16)
    v_cached = jnp.concatenate([c.T for c in cv], axis=0).astype(BF16)

    n_cached = SAMPLE_GROUP * WINDOW
    qr = lax.broadcasted_iota(jnp.int32, (GROUP_ROWS, n_cached), 0)
    kc = lax.broadcasted_iota(jnp.int32, (GROUP_ROWS, n_cached), 1)
    mask_cached = ((kc // WINDOW) == (qr // DEC_SEQ)) & ((kc % WINDOW) >= (qr % DEC_SEQ))
    qn = lax.broadcasted_iota(jnp.int32, (GROUP_ROWS, GROUP_ROWS), 0)
    kn = lax.broadcasted_iota(jnp.int32, (GROUP_ROWS, GROUP_ROWS), 1)
    mask_new = ((kn // DEC_SEQ) == (qn // DEC_SEQ)) & ((kn % DEC_SEQ) <= (qn % DEC_SEQ))
    att = _attend(q_ref[rows, :],
                  [(k_cached_t, v_cached, mask_cached, True),
                   (k_new.astype(BF16), v_new.astype(BF16), mask_new, False)],
                  sinks_ref, N_KV)
    ao_ref[rows, :] = att.astype(BF16)

    @pl.when(s == N_GROUPS - 1)
    def _():
        ot_ref[...] = xt_ref[...] + _dot(ao_ref[...], wo_ref[...])


def _attn_tail_call(sinks, xt, g, wq, wo, cos_t, sin_t, kt, vt, ck, cv):
    cached = pl.BlockSpec((SAMPLE_GROUP, N_KV, HEAD_DIM, WINDOW), lambda s: (s, 0, 0, 0))
    return pl.pallas_call(
        _attn_tail_kernel,
        grid=(N_GROUPS,),
        in_specs=[pl.BlockSpec(memory_space=pltpu.SMEM),
                  _resident((N_TAIL, D_MODEL)), _resident((1, D_MODEL)),
                  _resident((D_MODEL, D_MODEL)), _resident((D_MODEL, D_MODEL)),
                  _resident((N_TAIL, LANES)), _resident((N_TAIL, LANES)),
                  _resident((N_TAIL, KV_W)), _resident((N_TAIL, KV_W)),
                  cached, cached],
        out_specs=pl.BlockSpec((N_TAIL, D_MODEL), lambda s: (0, 0)),
        out_shape=jax.ShapeDtypeStruct((N_TAIL, D_MODEL), F32),
        scratch_shapes=[pltpu.VMEM((N_TAIL, D_MODEL), BF16), pltpu.VMEM((N_TAIL, D_MODEL), BF16)],
        compiler_params=_params(),
        name="attn_tail",
    )(sinks, xt, g.reshape(1, D_MODEL), wq, wo, cos_t, sin_t, kt, vt, ck, cv)


def _rope_tables(pos):
    inv = 1.0 / (ROPE_THETA ** (jnp.arange(0, HEAD_DIM, 2, dtype=F32) / HEAD_DIM))
    ang = pos.astype(F32)[:, None] * inv[None, :]
    cos, sin = jnp.cos(ang), jnp.sin(ang)
    return jnp.tile(cos, (1, N_KV)), jnp.tile(sin, (1, N_KV))


def kernel(x_prompt, x_sample, state_conv, cache_k, cache_v, meta_tokens, norm_g, ffn_w_in,
           ffn_w_out, conv_w_in, conv_kernel, conv_w_out, kv_norm_g, w_kv, w_q, w_o, sinks,
           final_norm_g):
    assert x_prompt.shape == (BATCH, SEQ, D_MODEL) and x_sample.shape == (DEC_BATCH, DEC_SEQ, D_MODEL)
    assert cache_k.shape == (DEC_BATCH, WINDOW, N_KV, HEAD_DIM)
    assert norm_g.shape[0] == 2 and conv_w_in.shape[0] == 1 and w_q.shape[0] == 1

    xm = x_prompt.reshape(N_MAIN, D_MODEL)
    xt = jnp.concatenate([x_sample.reshape(N_SAMPLE, D_MODEL), meta_tokens], axis=0)

    wq = w_q[0].reshape(D_MODEL, N_KV, GROUP, 2, HALF_DIM).transpose(0, 2, 3, 1, 4)
    wq = wq.reshape(D_MODEL, D_MODEL).astype(BF16)
    wo = w_o[0].reshape(N_KV, GROUP, HEAD_DIM, D_MODEL).transpose(1, 0, 2, 3)
    wo = wo.reshape(D_MODEL, D_MODEL).astype(BF16)
    wk = w_kv[:, :KV_W].reshape(D_MODEL, N_KV, 2, HALF_DIM).transpose(0, 2, 1, 3)
    wkv = jnp.concatenate([wk.reshape(D_MODEL, KV_W), w_kv[:, KV_W:]], axis=1).astype(BF16)

    pos_main = N_META + jnp.arange(SEQ, dtype=jnp.int32)
    pos_tail = jnp.concatenate([
        jnp.tile(PAST_LEN + jnp.arange(DEC_SEQ, dtype=jnp.int32), DEC_BATCH),
        jnp.arange(N_META, dtype=jnp.int32)])
    cos_m, sin_m = _rope_tables(pos_main)
    cos_t, sin_t = _rope_tables(pos_tail)

    sc = state_conv[0]
    grow = lambda a: jnp.pad(a, ((0, 0), (0, DEC_SEQ - a.shape[1]), (0, 0))).reshape(N_SAMPLE, D_MODEL)
    s1 = grow(sc[:, 1:])
    s2 = grow(sc)

    ffn_casts = lambda l, j: [(ffn_w_in, (l, j), N_TILES), (ffn_w_out, (l, j), N_TILES // 2)]
    slot_minor = lambda a: a.transpose(0, 2, 3, 1)
    ck, cv = slot_minor(cache_k), slot_minor(cache_v)

    xm, xt, cw_in, cw_out, w_in, w_out = _ffn_call(
        xm, xt, norm_g[0, 0], ffn_w_in[0, 0].astype(BF16), ffn_w_out[0, 0].astype(BF16),
        name="ffn_l0a",
        casts=[(conv_w_in, (0,), N_TILES), (conv_w_out, (0,), N_TILES)] + ffn_casts(0, 1))
    xm, xt, u_tail, sp = _conv_call(xm, xt, norm_g[0, 1], cw_in, conv_kernel[0], cw_out, s1, s2)
    xm, xt, w_in, w_out = _ffn_call(
        xm, xt, norm_g[0, 2], w_in, w_out, name="ffn_l0b", casts=ffn_casts(1, 0))

    xm, xt, km, vm, kt, vt, k_last, v_last, nk, nv, w_in, w_out = _ffn_call(
        xm, xt, norm_g[1, 0], w_in, w_out, name="ffn_l1a_kv",
        kv=(kv_norm_g, wkv, cos_m, sin_m, cos_t, sin_t, ck, cv), casts=ffn_casts(1, 1))
    xt = _attn_tail_call(sinks[0], xt, norm_g[1, 1], wq, wo, cos_t, sin_t, kt, vt, ck, cv)
    xm = _attn_main_call(sinks[0], xm, norm_g[1, 1], wq, wo, cos_m, sin_m, km, vm, kt, vt)
    ym, yt = _ffn_call(xm, xt, norm_g[1, 2], w_in, w_out, name="ffn_l1b_final",
                       final_g=final_norm_g)

    y_prompt = ym.reshape(BATCH, SEQ, D_MODEL)
    y_sample = yt[:N_SAMPLE].reshape(DEC_BATCH, DEC_SEQ, D_MODEL)
    new_state_conv_p = sp[:, CARRY - (CONV_W - 1):][None]
    new_state_conv_s = u_tail[:N_SAMPLE].reshape(DEC_BATCH, DEC_SEQ, D_MODEL)[:, DEC_SEQ - (CONV_W - 1):][None]
    slot_major = lambda a: a.transpose(0, 3, 1, 2)
    new_k_p = slot_major(k_last.reshape(BATCH, N_KV, HEAD_DIM, WINDOW))
    new_v_p = slot_major(v_last.reshape(BATCH, N_KV, HEAD_DIM, WINDOW))
    return (y_prompt, y_sample, new_state_conv_p, new_state_conv_s,
            new_k_p, new_v_p, slot_major(nk), slot_major(nv))
```

```python
import jax
import jax.numpy as jnp
from jax import lax
from jax.experimental import pallas as pl
from jax.experimental.pallas import tpu as pltpu

D_MODEL = 1024
BATCH = 2
SEQ = 8192
DEC_BATCH = 128
DEC_SEQ = 4
PAST_LEN = 8192
N_META = 16
D_FF = 2816
CONV_W = 3
HEAD_DIM = 64
N_HEADS = 16
N_KV = 4
GROUP = 4
WINDOW = 128
ROPE_THETA = 10000.0
EPS = 1e-6
NEG = -1e30
LOG2E = 1.4426950408889634

KV_W = N_KV * HEAD_DIM
HALF_DIM = HEAD_DIM // 2
N_MAIN = BATCH * SEQ
N_SAMPLE = DEC_BATCH * DEC_SEQ
N_TAIL = N_SAMPLE + N_META
TM = 512
N_TILES = N_MAIN // TM
TILES_PER_SEQ = SEQ // TM
FINAL_TM = 1024
FF_CHUNK = 256
OUT_CHUNK = 256
BLK = WINDOW
ATT_TM = 1024
SAMPLE_GROUP = 8
N_GROUPS = DEC_BATCH // SAMPLE_GROUP
CACHE_SEQS = DEC_BATCH // N_TILES
GROUP_ROWS = SAMPLE_GROUP * DEC_SEQ
LANES = 128
BF16_ROWS = 16
V7X_VMEM_LIMIT = 56 * 1024 * 1024

F32 = jnp.float32
BF16 = jnp.bfloat16


def _rms(x, g):
    return x * lax.rsqrt(jnp.mean(x * x, axis=-1, keepdims=True) + EPS) * g


def _rope(x, cos, sin):
    outs = []
    for c in range(x.shape[1] // KV_W):
        x1 = x[:, c * KV_W:c * KV_W + LANES]
        x2 = x[:, c * KV_W + LANES:(c + 1) * KV_W]
        outs += [x1 * cos - x2 * sin, x2 * cos + x1 * sin]
    return jnp.concatenate(outs, axis=1)


def _rows_split_to_heads(t):
    return jnp.concatenate([t[half * LANES + h * HALF_DIM:half * LANES + (h + 1) * HALF_DIM]
                            for h in range(N_KV) for half in range(2)], axis=0)


def _rows_heads_to_split(t):
    return jnp.concatenate([t[h * HEAD_DIM + half * HALF_DIM:h * HEAD_DIM + (half + 1) * HALF_DIM]
                            for half in range(2) for h in range(N_KV)], axis=0)


def _dot(a, b):
    return jnp.dot(a, b, preferred_element_type=F32)


def _dot_nt(a, b):
    return lax.dot_general(a, b, (((1,), (1,)), ((), ())), preferred_element_type=F32)


def _resident(shape):
    nd = len(shape)
    return pl.BlockSpec(shape, lambda i: (0,) * nd, pipeline_mode=pl.Buffered(1))


def _main_tile(i):
    return jnp.maximum(i - 1, 0)


def _main_spec(width):
    return pl.BlockSpec((TM, width), lambda i: (_main_tile(i), 0))


def _main_pos_spec():
    return pl.BlockSpec((TM, LANES), lambda i: (_main_tile(i) % TILES_PER_SEQ, 0))


def _params():
    return pltpu.CompilerParams(dimension_semantics=("arbitrary",),
                                vmem_limit_bytes=V7X_VMEM_LIMIT)


def _ffn_rows(x_ref, o_ref, g, w_in_ref, w_out_ref, a_ref, final_g=None):
    x = x_ref[...]
    m = x.shape[0]
    hn = _rms(x, g).astype(BF16)
    for c in range(D_FF // FF_CHUNK):
        lo = c * FF_CHUNK
        gate = _dot(hn, w_in_ref[:, lo:lo + FF_CHUNK])
        up = _dot(hn, w_in_ref[:, D_FF + lo:D_FF + lo + FF_CHUNK])
        a_ref[0:m, lo:lo + FF_CHUNK] = (gate * jax.nn.sigmoid(gate) * up).astype(BF16)
    sq = None
    for c in range(D_MODEL // OUT_CHUNK):
        cols = slice(c * OUT_CHUNK, (c + 1) * OUT_CHUNK)
        y = x[:, cols] + 0.5 * _dot(a_ref[0:m, :], w_out_ref[:, cols])
        o_ref[:, cols] = y
        if final_g is not None:
            part = jnp.sum(y * y, axis=-1, keepdims=True)
            sq = part if sq is None else sq + part
    if final_g is not None:
        o_ref[...] = o_ref[...] * lax.rsqrt(sq * (1.0 / D_MODEL) + EPS) * final_g


def _make_ffn_kernel(has_final, has_kv, n_cast):
    def body(*refs):
        refs = list(refs)
        xm_ref, xt_ref, g_ref, w_in_ref, w_out_ref = refs[:5]
        pos = 5
        if has_final:
            gf_ref = refs[pos]
            pos += 1
        if has_kv:
            gkv_ref, wkv_ref, cm_ref, sm_ref, ct_ref, st_ref, ck_ref, cv_ref = refs[pos:pos + 8]
            pos += 8
        cast_src = refs[pos:pos + n_cast]
        pos += n_cast
        om_ref, ot_ref = refs[pos:pos + 2]
        pos += 2
        if has_kv:
            km_ref, vm_ref, kt_ref, vt_ref, kl_ref, vl_ref, nk_ref, nv_ref = refs[pos:pos + 8]
            pos += 8
        cast_dst = refs[pos:pos + n_cast]
        pos += n_cast
        a_ref = refs[pos]
        if has_kv:
            newk_ref, newv_ref = refs[pos + 1:pos + 3]
        i = pl.program_id(0)

        for src, dst in zip(cast_src, cast_dst):
            dst[...] = src[...].astype(BF16)

        def rows(x_ref, o_ref):
            _ffn_rows(x_ref, o_ref, g_ref[...], w_in_ref, w_out_ref, a_ref,
                      gf_ref[...] if has_final else None)

        def project_kv(x_ref, cos_ref, sin_ref, k_ref, v_ref):
            kv = _dot(_rms(x_ref[...], gkv_ref[...]).astype(BF16), wkv_ref[...])
            k = _rope(kv[:, :KV_W], cos_ref[...], sin_ref[...])
            v = kv[:, KV_W:]
            k_ref[...] = k
            v_ref[...] = v
            return k, v

        @pl.when(i == 0)
        def _():
            if has_kv:
                k, v = project_kv(xt_ref, ct_ref, st_ref, kt_ref, vt_ref)
                for c in range(N_SAMPLE // LANES):
                    newk_ref[c] = _rows_split_to_heads(k[c * LANES:(c + 1) * LANES, :].T)
                    newv_ref[c] = v[c * LANES:(c + 1) * LANES, :].T
            rows(xt_ref, ot_ref)

        @pl.when(i > 0)
        def _():
            if has_kv:
                t = i - 1
                k, v = project_kv(xm_ref, cm_ref, sm_ref, km_ref, vm_ref)
                kl_ref[...] = _rows_split_to_heads(k[TM - WINDOW:, :].T)
                vl_ref[...] = v[TM - WINDOW:, :].T

                lane = lax.broadcasted_iota(jnp.int32, (1, WINDOW), 1)
                is_new = lane >= WINDOW - DEC_SEQ
                first_lane = (t * CACHE_SEQS * DEC_SEQ) % LANES
                for new_ref, c_ref, n_ref in ((newk_ref, ck_ref, nk_ref), (newv_ref, cv_ref, nv_ref)):
                    new_t = new_ref[(t * CACHE_SEQS * DEC_SEQ) // LANES]
                    for b in range(CACHE_SEQS):
                        shifted = pltpu.roll(c_ref[b].reshape(KV_W, WINDOW), WINDOW - DEC_SEQ, 1)
                        amount = (WINDOW - DEC_SEQ - b * DEC_SEQ - first_lane) & (LANES - 1)
                        placed = pltpu.roll(new_t, amount, 1)
                        n_ref[b] = jnp.where(is_new, placed, shifted).reshape(N_KV, HEAD_DIM, WINDOW)
            rows(xm_ref, om_ref)

    return body


def _cast_specs(casts):
    in_specs, out_specs, out_shape, args = [], [], [], []
    for arr, lead, steps, src_block in casts:
        n_rows, n_cols = arr.shape[-2:]
        rows = n_rows // steps
        assert rows * steps == n_rows and rows % BF16_ROWS == 0
        blk = lambda i, steps=steps: jnp.minimum(_main_tile(i), steps - 1)
        src = blk if src_block is None else (lambda i, blk=blk, f=src_block: f(blk(i)))
        in_specs.append(pl.BlockSpec((None,) * len(lead) + (rows, n_cols),
                                     lambda i, lead=lead, src=src: lead + (src(i), 0)))
        out_specs.append(pl.BlockSpec((rows, n_cols), lambda i, blk=blk: (blk(i), 0)))
        out_shape.append(jax.ShapeDtypeStruct((n_rows, n_cols), BF16))
        args.append(arr)
    return in_specs, out_specs, out_shape, args


def _ffn_call(xm, xt, g, w_in, w_out, *, name, final_g=None, kv=None, casts=(), tm=TM):
    assert tm == TM or (kv is None and not casts)
    main = pl.BlockSpec((tm, D_MODEL), lambda i: (_main_tile(i), 0))
    in_specs = [main, _resident((N_TAIL, D_MODEL)), _resident((1, D_MODEL)),
                _resident((D_MODEL, 2 * D_FF)), _resident((D_FF, D_MODEL))]
    args = [xm, xt, g.reshape(1, D_MODEL), w_in, w_out]
    out_shape = [jax.ShapeDtypeStruct((N_MAIN, D_MODEL), F32),
                 jax.ShapeDtypeStruct((N_TAIL, D_MODEL), F32)]
    out_specs = [main, pl.BlockSpec((N_TAIL, D_MODEL), lambda i: (0, 0))]
    if final_g is not None:
        in_specs.append(_resident((1, D_MODEL)))
        args.append(final_g.reshape(1, D_MODEL))
    scratch = [pltpu.VMEM((max(N_TAIL, tm), D_FF), BF16)]
    if kv is not None:
        g_kv, w_kv, cos_m, sin_m, cos_t, sin_t, ck, cv = kv
        cached = pl.BlockSpec((CACHE_SEQS, N_KV, HEAD_DIM, WINDOW), lambda i: (_main_tile(i), 0, 0, 0))
        in_specs += [_resident((1, D_MODEL)), _resident((D_MODEL, 2 * KV_W)),
                     _main_pos_spec(), _main_pos_spec(),
                     _resident((N_TAIL, LANES)), _resident((N_TAIL, LANES)), cached, cached]
        args += [g_kv.reshape(1, D_MODEL), w_kv, cos_m, sin_m, cos_t, sin_t, ck, cv]
        out_shape += [jax.ShapeDtypeStruct((N_MAIN, KV_W), F32)] * 2
        out_shape += [jax.ShapeDtypeStruct((N_TAIL, KV_W), F32)] * 2
        out_shape += [jax.ShapeDtypeStruct((BATCH, KV_W, WINDOW), F32)] * 2
        out_shape += [jax.ShapeDtypeStruct((DEC_BATCH, N_KV, HEAD_DIM, WINDOW), F32)] * 2
        last = pl.BlockSpec((None, KV_W, WINDOW), lambda i: (_main_tile(i) // TILES_PER_SEQ, 0, 0))
        out_specs += [_main_spec(KV_W), _main_spec(KV_W),
                      pl.BlockSpec((N_TAIL, KV_W), lambda i: (0, 0)),
                      pl.BlockSpec((N_TAIL, KV_W), lambda i: (0, 0)), last, last, cached, cached]
        scratch += [pltpu.VMEM((N_SAMPLE // LANES, KV_W, LANES), F32)] * 2
    c_in, c_out, c_shape, c_args = _cast_specs(casts)
    return pl.pallas_call(
        _make_ffn_kernel(final_g is not None, kv is not None, len(casts)),
        grid=(1 + N_MAIN // tm,),
        in_specs=in_specs + c_in,
        out_specs=out_specs + c_out,
        out_shape=out_shape + c_shape,
        scratch_shapes=scratch,
        compiler_params=_params(),
        name=name,
    )(*args, *c_args)


CARRY = 8
CONV_TM = 1024


def _conv_rows(x_ref, o_ref, g, w_in_ref, kern_ref, w_out_ref, ubuf_ref, mbuf_ref, fix):
    x = x_ref[...]
    m = x.shape[0]
    hn = _rms(x, g).astype(BF16)
    for c in range(D_MODEL // OUT_CHUNK):
        cols = slice(c * OUT_CHUNK, (c + 1) * OUT_CHUNK)
        b, cc, z = (_dot(hn, w_in_ref[:, j * D_MODEL + c * OUT_CHUNK:j * D_MODEL + (c + 1) * OUT_CHUNK])
                    for j in range(3))
        u = cc * z
        ubuf_ref[CARRY:CARRY + m, cols] = u
        prev1 = ubuf_ref[CARRY - 1:CARRY - 1 + m, cols]
        prev2 = ubuf_ref[CARRY - 2:CARRY - 2 + m, cols]
        if fix is not None:
            prev1, prev2 = fix(prev1, prev2, cols)
        conv = kern_ref[0:1, cols] * prev2 + kern_ref[1:2, cols] * prev1 + kern_ref[2:3, cols] * u
        mbuf_ref[0:m, cols] = (b * conv).astype(BF16)
    for c in range(D_MODEL // OUT_CHUNK):
        cols = slice(c * OUT_CHUNK, (c + 1) * OUT_CHUNK)
        o_ref[:, cols] = x[:, cols] + _dot(mbuf_ref[0:m, :], w_out_ref[:, cols])


def _conv_kernel(xm_ref, xt_ref, g_ref, w_in_ref, kern_ref, w_out_ref, s1_ref, s2_ref,
                 om_ref, ot_ref, ut_ref, sp_ref, ubuf_ref, meta_ref, mbuf_ref):
    i = pl.program_id(0)

    @pl.when(i == 0)
    def _():
        ubuf_ref[0:CARRY, :] = jnp.zeros((CARRY, D_MODEL), F32)

        def fix(prev1, prev2, cols):
            step = lax.broadcasted_iota(jnp.int32, (N_SAMPLE, 1), 0) & (DEC_SEQ - 1)
            meta = lax.broadcasted_iota(jnp.int32, (N_META, 1), 0)
            p1 = [jnp.where(step >= 1, prev1[:N_SAMPLE], s1_ref[:, cols]),
                  jnp.where(meta >= 1, prev1[N_SAMPLE:], 0.0)]
            p2 = [jnp.where(step >= 2, prev2[:N_SAMPLE], s2_ref[:, cols]),
                  jnp.where(meta >= 2, prev2[N_SAMPLE:], 0.0)]
            return jnp.concatenate(p1, axis=0), jnp.concatenate(p2, axis=0)

        _conv_rows(xt_ref, ot_ref, g_ref[...], w_in_ref, kern_ref, w_out_ref, ubuf_ref, mbuf_ref, fix)
        ut_ref[...] = ubuf_ref[CARRY:CARRY + N_TAIL, :]
        meta_ref[...] = ubuf_ref[N_TAIL:N_TAIL + CARRY, :]

    @pl.when(i > 0)
    def _():
        t = i - 1
        per_seq = SEQ // CONV_TM

        @pl.when(t % per_seq == 0)
        def _():
            ubuf_ref[0:CARRY, :] = meta_ref[...]

        _conv_rows(xm_ref, om_ref, g_ref[...], w_in_ref, kern_ref, w_out_ref, ubuf_ref, mbuf_ref, None)
        last = ubuf_ref[CONV_TM:CONV_TM + CARRY, :]
        ubuf_ref[0:CARRY, :] = last

        @pl.when(t % per_seq == per_seq - 1)
        def _():
            sp_ref[t // per_seq] = last


def _conv_call(xm, xt, g, w_in, kern, w_out, s1, s2):
    main = pl.BlockSpec((CONV_TM, D_MODEL), lambda i: (_main_tile(i), 0))
    return pl.pallas_call(
        _conv_kernel,
        grid=(1 + N_MAIN // CONV_TM,),
        in_specs=[main, _resident((N_TAIL, D_MODEL)), _resident((1, D_MODEL)),
                  _resident((D_MODEL, 3 * D_MODEL)), _resident((CONV_W, D_MODEL)),
                  _resident((D_MODEL, D_MODEL)),
                  _resident((N_SAMPLE, D_MODEL)), _resident((N_SAMPLE, D_MODEL))],
        out_specs=[main,
                   pl.BlockSpec((N_TAIL, D_MODEL), lambda i: (0, 0)),
                   pl.BlockSpec((N_TAIL, D_MODEL), lambda i: (0, 0)),
                   pl.BlockSpec((BATCH, CARRY, D_MODEL), lambda i: (0, 0, 0))],
        out_shape=[jax.ShapeDtypeStruct((N_MAIN, D_MODEL), F32),
                   jax.ShapeDtypeStruct((N_TAIL, D_MODEL), F32),
                   jax.ShapeDtypeStruct((N_TAIL, D_MODEL), F32),
                   jax.ShapeDtypeStruct((BATCH, CARRY, D_MODEL), F32)],
        scratch_shapes=[pltpu.VMEM((CARRY + max(N_TAIL, CONV_TM), D_MODEL), F32),
                        pltpu.VMEM((CARRY, D_MODEL), F32),
                        pltpu.VMEM((max(N_TAIL, CONV_TM), D_MODEL), BF16)],
        compiler_params=_params(),
        name="conv_mixer",
    )(xm, xt, g.reshape(1, D_MODEL), w_in, kern, w_out, s1, s2)


def _attend(qp, segs, sinks_ref, heads_per_pass=1):
    mq = qp.shape[0]
    rows = GROUP * mq
    lane = lax.broadcasted_iota(jnp.int32, (1, KV_W), 1)
    qk_head = (lane & (LANES - 1)) // HALF_DIM
    pair_head = lax.broadcasted_iota(jnp.int32, (1, LANES), 1) // HEAD_DIM
    row_head = lax.broadcasted_iota(jnp.int32, (heads_per_pass * rows, 1), 0) // mq
    masks = [jnp.concatenate([seg[2]] * (heads_per_pass * GROUP), axis=0) for seg in segs]
    outs = []
    for k0 in range(0, N_KV, heads_per_pass):
        heads = range(k0, k0 + heads_per_pass)
        qs = jnp.concatenate(
            [jnp.where(qk_head == k, qp[:, g * KV_W:(g + 1) * KV_W], jnp.zeros((), BF16))
             for k in heads for g in range(GROUP)], axis=0)
        sink = jnp.full((heads_per_pass * rows, 1), sinks_ref[k0 * GROUP] * LOG2E, F32)
        for j in range(1, heads_per_pass * GROUP):
            sink = jnp.where(row_head == j, sinks_ref[k0 * GROUP + j] * LOG2E, sink)
        scores = []
        for (kb, _, _, k_transposed), msk in zip(segs, masks):
            s = _dot(qs, kb) if k_transposed else _dot_nt(qs, kb)
            scores.append(jnp.where(msk, s, NEG))
        mx = sink
        for s in scores:
            mx = jnp.maximum(mx, jnp.max(s, axis=-1, keepdims=True))
        denom = jnp.exp2(sink - mx)
        probs = []
        for s in scores:
            p = jnp.exp2(s - mx)
            denom = denom + jnp.sum(p, axis=-1, keepdims=True)
            probs.append(p.astype(BF16))
        inv = 1.0 / denom
        for n, k in enumerate(heads):
            pair = slice((k // 2) * LANES, (k // 2 + 1) * LANES)
            part = slice(n * rows, (n + 1) * rows)
            acc = None
            for p, (_, vb, _, _) in zip(probs, segs):
                pv = _dot(p[part], jnp.where(pair_head == k % 2, vb[:, pair], jnp.zeros((), BF16)))
                acc = pv if acc is None else acc + pv
            outs.append(acc * inv[part])
    out = jnp.concatenate([outs[k] + outs[k + 1] for k in range(0, N_KV, 2)], axis=1)
    return jnp.concatenate([out[g * mq:(g + 1) * mq, :] for g in range(GROUP)], axis=1)


def _queries(x, g, wq_ref, cos, sin):
    q = _dot(_rms(x, g).astype(BF16), wq_ref[...])
    return (_rope(q, cos, sin) * (HEAD_DIM ** -0.5 * LOG2E)).astype(BF16)


def _attn_main_kernel(sinks_ref, xm_ref, g_ref, wq_ref, wo_ref, cos_ref, sin_ref,
                      km_ref, vm_ref, kt_ref, vt_ref,
                      om_ref, q_ref, ao_ref, kbuf_ref, vbuf_ref):
    i = pl.program_id(0)
    start = i % (SEQ // ATT_TM) == 0

    x = xm_ref[...]
    q_ref[...] = _queries(x, g_ref[...], wq_ref, cos_ref[...], sin_ref[...])

    @pl.when(start)
    def _():
        pad = jnp.zeros((BLK - N_META, KV_W), BF16)
        kbuf_ref[0:BLK - N_META, :] = pad
        vbuf_ref[0:BLK - N_META, :] = pad
        kbuf_ref[BLK - N_META:BLK, :] = kt_ref[N_SAMPLE:N_TAIL, :].astype(BF16)
        vbuf_ref[BLK - N_META:BLK, :] = vt_ref[N_SAMPLE:N_TAIL, :].astype(BF16)

    @pl.when(jnp.logical_not(start))
    def _():
        kbuf_ref[0:BLK, :] = kbuf_ref[ATT_TM:ATT_TM + BLK, :]
        vbuf_ref[0:BLK, :] = vbuf_ref[ATT_TM:ATT_TM + BLK, :]

    kbuf_ref[BLK:BLK + ATT_TM, :] = km_ref[...].astype(BF16)
    vbuf_ref[BLK:BLK + ATT_TM, :] = vm_ref[...].astype(BF16)

    qi = lax.broadcasted_iota(jnp.int32, (BLK, 2 * BLK), 0)
    kj = lax.broadcasted_iota(jnp.int32, (BLK, 2 * BLK), 1)
    band = (kj >= qi) & (kj <= qi + WINDOW)
    first_key = jnp.where(start, BLK - N_META, 0)
    for blk in range(ATT_TM // BLK):
        lo = blk * BLK
        mask = band & (kj >= first_key) if blk == 0 else band
        att = _attend(q_ref[lo:lo + BLK, :],
                      [(kbuf_ref[lo:lo + 2 * BLK, :], vbuf_ref[lo:lo + 2 * BLK, :], mask, False)],
                      sinks_ref)
        ao_ref[lo:lo + BLK, :] = att.astype(BF16)
    om_ref[...] = x + _dot(ao_ref[...], wo_ref[...])


def _attn_main_call(sinks, xm, g, wq, wo, cos_m, sin_m, km, vm, kt, vt):
    tile = lambda w: pl.BlockSpec((ATT_TM, w), lambda i: (i, 0))
    pos = pl.BlockSpec((ATT_TM, LANES), lambda i: (i % (SEQ // ATT_TM), 0))
    return pl.pallas_call(
        _attn_main_kernel,
        grid=(N_MAIN // ATT_TM,),
        in_specs=[pl.BlockSpec(memory_space=pltpu.SMEM),
                  tile(D_MODEL), _resident((1, D_MODEL)),
                  _resident((D_MODEL, D_MODEL)), _resident((D_MODEL, D_MODEL)),
                  pos, pos, tile(KV_W), tile(KV_W),
                  _resident((N_TAIL, KV_W)), _resident((N_TAIL, KV_W))],
        out_specs=tile(D_MODEL),
        out_shape=jax.ShapeDtypeStruct((N_MAIN, D_MODEL), F32),
        scratch_shapes=[pltpu.VMEM((ATT_TM, D_MODEL), BF16), pltpu.VMEM((ATT_TM, D_MODEL), BF16),
                        pltpu.VMEM((BLK + ATT_TM, KV_W), BF16),
                        pltpu.VMEM((BLK + ATT_TM, KV_W), BF16)],
        compiler_params=_params(),
        name="attn_main",
    )(sinks, xm, g.reshape(1, D_MODEL), wq, wo, cos_m, sin_m, km, vm, kt, vt)


def _attn_tail_kernel(sinks_ref, xt_ref, g_ref, wq_ref, wo_ref, cos_ref, sin_ref,
                      kt_ref, vt_ref, ck_ref, cv_ref, ot_ref, q_ref, ao_ref):
    s = pl.program_id(0)

    @pl.when(s == 0)
    def _():
        q_ref[...] = _queries(xt_ref[...], g_ref[...], wq_ref, cos_ref[...], sin_ref[...])
        qi = lax.broadcasted_iota(jnp.int32, (N_META, N_META), 0)
        kj = lax.broadcasted_iota(jnp.int32, (N_META, N_META), 1)
        km = kt_ref[N_SAMPLE:N_TAIL, :].astype(BF16)
        vm = vt_ref[N_SAMPLE:N_TAIL, :].astype(BF16)
        att = _attend(q_ref[N_SAMPLE:N_TAIL, :], [(km, vm, kj <= qi, False)], sinks_ref, N_KV)
        ao_ref[N_SAMPLE:N_TAIL, :] = att.astype(BF16)

    rows = pl.ds(pl.multiple_of(s * GROUP_ROWS, GROUP_ROWS), GROUP_ROWS)
    k_new = kt_ref[rows, :]
    v_new = vt_ref[rows, :]
    ck = [ck_ref[b].reshape(KV_W, WINDOW) for b in range(SAMPLE_GROUP)]
    cv = [cv_ref[b].reshape(KV_W, WINDOW) for b in range(SAMPLE_GROUP)]
    k_cached_t = jnp.concatenate([_rows_heads_to_split(c) for c in ck], axis=1).astype(BF16)
    v_cached = jnp.concatenate([c.T for c in cv], axis=0).astype(BF16)

    n_cached = SAMPLE_GROUP * WINDOW
    qr = lax.broadcasted_iota(jnp.int32, (GROUP_ROWS, n_cached), 0)
    kc = lax.broadcasted_iota(jnp.int32, (GROUP_ROWS, n_cached), 1)
    mask_cached = ((kc // WINDOW) == (qr // DEC_SEQ)) & ((kc % WINDOW) >= (qr % DEC_SEQ))
    qn = lax.broadcasted_iota(jnp.int32, (GROUP_ROWS, GROUP_ROWS), 0)
    kn = lax.broadcasted_iota(jnp.int32, (GROUP_ROWS, GROUP_ROWS), 1)
    mask_new = ((kn // DEC_SEQ) == (qn // DEC_SEQ)) & ((kn % DEC_SEQ) <= (qn % DEC_SEQ))
    att = _attend(q_ref[rows, :],
                  [(k_cached_t, v_cached, mask_cached, True),
                   (k_new.astype(BF16), v_new.astype(BF16), mask_new, False)],
                  sinks_ref, N_KV)
    ao_ref[rows, :] = att.astype(BF16)

    @pl.when(s == N_GROUPS - 1)
    def _():
        ot_ref[...] = xt_ref[...] + _dot(ao_ref[...], wo_ref[...])


def _attn_tail_call(sinks, xt, g, wq, wo, cos_t, sin_t, kt, vt, ck, cv):
    cached = pl.BlockSpec((SAMPLE_GROUP, N_KV, HEAD_DIM, WINDOW), lambda s: (s, 0, 0, 0))
    return pl.pallas_call(
        _attn_tail_kernel,
        grid=(N_GROUPS,),
        in_specs=[pl.BlockSpec(memory_space=pltpu.SMEM),
                  _resident((N_TAIL, D_MODEL)), _resident((1, D_MODEL)),
                  _resident((D_MODEL, D_MODEL)), _resident((D_MODEL, D_MODEL)),
                  _resident((N_TAIL, LANES)), _resident((N_TAIL, LANES)),
                  _resident((N_TAIL, KV_W)), _resident((N_TAIL, KV_W)),
                  cached, cached],
        out_specs=pl.BlockSpec((N_TAIL, D_MODEL), lambda s: (0, 0)),
        out_shape=jax.ShapeDtypeStruct((N_TAIL, D_MODEL), F32),
        scratch_shapes=[pltpu.VMEM((N_TAIL, D_MODEL), BF16), pltpu.VMEM((N_TAIL, D_MODEL), BF16)],
        compiler_params=_params(),
        name="attn_tail",
    )(sinks, xt, g.reshape(1, D_MODEL), wq, wo, cos_t, sin_t, kt, vt, ck, cv)


def _rope_tables(pos):
    inv = 1.0 / (ROPE_THETA ** (jnp.arange(0, HEAD_DIM, 2, dtype=F32) / HEAD_DIM))
    ang = pos.astype(F32)[:, None] * inv[None, :]
    cos, sin = jnp.cos(ang), jnp.sin(ang)
    return jnp.tile(cos, (1, N_KV)), jnp.tile(sin, (1, N_KV))


def kernel(x_prompt, x_sample, state_conv, cache_k, cache_v, meta_tokens, norm_g, ffn_w_in,
           ffn_w_out, conv_w_in, conv_kernel, conv_w_out, kv_norm_g, w_kv, w_q, w_o, sinks,
           final_norm_g):
    assert x_prompt.shape == (BATCH, SEQ, D_MODEL) and x_sample.shape == (DEC_BATCH, DEC_SEQ, D_MODEL)
    assert cache_k.shape == (DEC_BATCH, WINDOW, N_KV, HEAD_DIM)
    assert norm_g.shape[0] == 2 and conv_w_in.shape[0] == 1 and w_q.shape[0] == 1

    xm = x_prompt.reshape(N_MAIN, D_MODEL)
    xt = jnp.concatenate([x_sample.reshape(N_SAMPLE, D_MODEL), meta_tokens], axis=0)

    wq = w_q[0].reshape(D_MODEL, N_KV, GROUP, 2, HALF_DIM).transpose(0, 2, 3, 1, 4)
    wq = wq.reshape(D_MODEL, D_MODEL).astype(BF16)
    wk =w_kv[:, :KV_W].reshape(D_MODEL, N_KV, 2, HALF_DIM).transpose(0, 2, 1, 3)
    wkv = jnp.concatenate([wk.reshape(D_MODEL, KV_W), w_kv[:, KV_W:]], axis=1).astype(BF16)

    pos_main = N_META + jnp.arange(SEQ, dtype=jnp.int32)
    pos_tail = jnp.concatenate([
        jnp.tile(PAST_LEN + jnp.arange(DEC_SEQ, dtype=jnp.int32), DEC_BATCH),
        jnp.arange(N_META, dtype=jnp.int32)])
    cos_m, sin_m = _rope_tables(pos_main)
    cos_t, sin_t = _rope_tables(pos_tail)

    sc = state_conv[0]
    grow = lambda a: jnp.pad(a, ((0, 0), (0, DEC_SEQ - a.shape[1]), (0, 0))).reshape(N_SAMPLE, D_MODEL)
    s1 = grow(sc[:, 1:])
    s2 = grow(sc)

    ffn_casts = lambda l, j: [(ffn_w_in, (l, j), N_TILES, None),
                              (ffn_w_out, (l, j), N_TILES // 2, None)]
    wo_cast = (w_o, (0,), N_HEADS, lambda b: (b % N_KV) * GROUP + b // N_KV)
    slot_minor = lambda a: a.transpose(0, 2, 3, 1)
    ck, cv = slot_minor(cache_k), slot_minor(cache_v)

    xm, xt, cw_in, cw_out, w_in, w_out = _ffn_call(
        xm, xt, norm_g[0, 0], ffn_w_in[0, 0].astype(BF16), ffn_w_out[0, 0].astype(BF16),
        name="ffn_l0a",
        casts=[(conv_w_in, (0,), N_TILES, None), (conv_w_out, (0,), N_TILES, None)] + ffn_casts(0, 1))
    xm, xt, u_tail, sp = _conv_call(xm, xt, norm_g[0, 1], cw_in, conv_kernel[0], cw_out, s1, s2)
    xm, xt, wo, w_in, w_out = _ffn_call(
        xm, xt, norm_g[0, 2], w_in, w_out, name="ffn_l0b", casts=[wo_cast] + ffn_casts(1, 0))

    xm, xt, km, vm, kt, vt, k_last, v_last, nk, nv, w_in, w_out = _ffn_call(
        xm, xt, norm_g[1, 0], w_in, w_out, name="ffn_l1a_kv",
        kv=(kv_norm_g, wkv, cos_m, sin_m, cos_t, sin_t, ck, cv), casts=ffn_casts(1, 1))
    xt = _attn_tail_call(sinks[0], xt, norm_g[1, 1], wq, wo, cos_t, sin_t, kt, vt, ck, cv)
    xm = _attn_main_call(sinks[0], xm, norm_g[1, 1], wq, wo, cos_m, sin_m, km, vm, kt, vt)
    ym, yt = _ffn_call(xm, xt, norm_g[1, 2], w_in, w_out, name="ffn_l1b_final",
                       final_g=final_norm_g, tm=FINAL_TM)

    y_prompt = ym.reshape(BATCH, SEQ, D_MODEL)
    y_sample = yt[:N_SAMPLE].reshape(DEC_BATCH, DEC_SEQ, D_MODEL)
    new_state_conv_p = sp[:, CARRY - (CONV_W - 1):][None]
    new_state_conv_s = u_tail[:N_SAMPLE].reshape(DEC_BATCH, DEC_SEQ, D_MODEL)[:, DEC_SEQ - (CONV_W - 1):][None]
    slot_major = lambda a: a.transpose(0, 3, 1, 2)
    new_k_p = slot_major(k_last.reshape(BATCH, N_KV, HEAD_DIM, WINDOW))
    new_v_p = slot_major(v_last.reshape(BATCH, N_KV, HEAD_DIM, WINDOW))
    return (y_prompt, y_sample, new_state_conv_p, new_state_conv_s,
            new_k_p, new_v_p, slot_major(nk), slot_major(nv))
```

```python
import jax
import jax.numpy as jnp
from jax import lax
from jax.experimental import pallas as pl
from jax.experimental.pallas import tpu as pltpu

D_MODEL = 1024
BATCH = 2
SEQ = 8192
DEC_BATCH = 128
DEC_SEQ = 4
PAST_LEN = 8192
N_META = 16
D_FF = 2816
CONV_W = 3
HEAD_DIM = 64
N_HEADS = 16
N_KV = 4
GROUP = 4
WINDOW = 128
ROPE_THETA = 10000.0
EPS = 1e-6
NEG = -1e30
LOG2E = 1.4426950408889634

KV_W = N_KV * HEAD_DIM
HALF_DIM = HEAD_DIM // 2
N_MAIN = BATCH * SEQ
N_SAMPLE = DEC_BATCH * DEC_SEQ
N_TAIL = N_SAMPLE + N_META
TM = 512
N_TILES = N_MAIN // TM
TILES_PER_SEQ = SEQ // TM
FINAL_TM = 1024
FF_CHUNK = 256
OUT_CHUNK = 256
BLK = WINDOW
ATT_TM = 1024
SAMPLE_GROUP = 8
N_GROUPS = DEC_BATCH // SAMPLE_GROUP
CACHE_SEQS = DEC_BATCH // N_TILES
GROUP_ROWS = SAMPLE_GROUP * DEC_SEQ
LANES = 128
BF16_ROWS = 16
V7X_VMEM_LIMIT = 56 * 1024 * 1024

F32 = jnp.float32
BF16 = jnp.bfloat16


def _rms(x, g):
    return x * lax.rsqrt(jnp.mean(x * x, axis=-1, keepdims=True) + EPS) * g


def _rope(x, cos, sin):
    outs = []
    for c in range(x.shape[1] // KV_W):
        x1 = x[:, c * KV_W:c * KV_W + LANES]
        x2 = x[:, c * KV_W + LANES:(c + 1) * KV_W]
        outs += [x1 * cos - x2 * sin, x2 * cos + x1 * sin]
    return jnp.concatenate(outs, axis=1)


def _rows_split_to_heads(t):
    return jnp.concatenate([t[half * LANES + h * HALF_DIM:half * LANES + (h + 1) * HALF_DIM]
                            for h in range(N_KV) for half in range(2)], axis=0)


def _rows_heads_to_split(t):
    return jnp.concatenate([t[h * HEAD_DIM + half * HALF_DIM:h * HEAD_DIM + (half + 1) * HALF_DIM]
                            for half in range(2) for h in range(N_KV)], axis=0)


def _dot(a, b):
    return jnp.dot(a, b, preferred_element_type=F32)


def _dot_nt(a, b):
    return lax.dot_general(a, b, (((1,), (1,)), ((), ())), preferred_element_type=F32)


def _resident(shape):
    nd = len(shape)
    return pl.BlockSpec(shape, lambda i: (0,) * nd, pipeline_mode=pl.Buffered(1))


def _main_tile(i):
    return jnp.maximum(i - 1, 0)


def _main_spec(width):
    return pl.BlockSpec((TM, width), lambda i: (_main_tile(i), 0))


def _main_pos_spec():
    return pl.BlockSpec((TM, LANES), lambda i: (_main_tile(i) % TILES_PER_SEQ, 0))


def _params():
    return pltpu.CompilerParams(dimension_semantics=("arbitrary",),
                                vmem_limit_bytes=V7X_VMEM_LIMIT)


def _ffn_rows(x_ref, o_ref, g, w_in_ref, w_out_ref, a_ref, final_g=None):
    x = x_ref[...]
    m = x.shape[0]
    hn = _rms(x, g).astype(BF16)
    for c in range(D_FF // FF_CHUNK):
        lo = c * FF_CHUNK
        gate = _dot(hn, w_in_ref[:, lo:lo + FF_CHUNK])
        up = _dot(hn, w_in_ref[:, D_FF + lo:D_FF + lo + FF_CHUNK])
        a_ref[0:m, lo:lo + FF_CHUNK] = (gate * jax.nn.sigmoid(gate) * up).astype(BF16)
    sq = None
    for c in range(D_MODEL // OUT_CHUNK):
        cols = slice(c * OUT_CHUNK, (c + 1) * OUT_CHUNK)
        y = x[:, cols] + 0.5 * _dot(a_ref[0:m, :], w_out_ref[:, cols])
        o_ref[:, cols] = y
        if final_g is not None:
            part = jnp.sum(y * y, axis=-1, keepdims=True)
            sq = part if sq is None else sq + part
    if final_g is not None:
        o_ref[...] = o_ref[...] * lax.rsqrt(sq * (1.0 / D_MODEL) + EPS) * final_g


def _make_ffn_kernel(has_final, has_kv, n_cast):
    def body(*refs):
        refs = list(refs)
        xm_ref, xt_ref, g_ref, w_in_ref, w_out_ref = refs[:5]
        pos = 5
        if has_final:
            gf_ref = refs[pos]
            pos += 1
        if has_kv:
            gkv_ref, wkv_ref, cm_ref, sm_ref, ct_ref, st_ref, ck_ref, cv_ref = refs[pos:pos + 8]
            pos += 8
        cast_src = refs[pos:pos + n_cast]
        pos += n_cast
        om_ref, ot_ref = refs[pos:pos + 2]
        pos += 2
        if has_kv:
            km_ref, vm_ref, kt_ref, vt_ref, kl_ref, vl_ref, nk_ref, nv_ref = refs[pos:pos + 8]
            pos += 8
        cast_dst = refs[pos:pos + n_cast]
        pos += n_cast
        a_ref = refs[pos]
        if has_kv:
            newk_ref, newv_ref = refs[pos + 1:pos + 3]
        i = pl.program_id(0)

        for src, dst in zip(cast_src, cast_dst):
            dst[...] = src[...].astype(BF16)

        def rows(x_ref, o_ref):
            _ffn_rows(x_ref, o_ref, g_ref[...], w_in_ref, w_out_ref, a_ref,
                      gf_ref[...] if has_final else None)

        def project_kv(x_ref, cos_ref, sin_ref, k_ref, v_ref):
            kv = _dot(_rms(x_ref[...], gkv_ref[...]).astype(BF16), wkv_ref[...])
            k = _rope(kv[:, :KV_W], cos_ref[...], sin_ref[...])
            v = kv[:, KV_W:]
            k_ref[...] = k
            v_ref[...] = v
            return k, v

        @pl.when(i == 0)
        def _():
            if has_kv:
                k, v = project_kv(xt_ref, ct_ref, st_ref, kt_ref, vt_ref)
                for c in range(N_SAMPLE // LANES):
                    newk_ref[c] = _rows_split_to_heads(k[c * LANES:(c + 1) * LANES, :].T)
                    newv_ref[c] = v[c * LANES:(c + 1) * LANES, :].T
            rows(xt_ref, ot_ref)

        @pl.when(i > 0)
        def _():
            if has_kv:
                t = i - 1
                k, v = project_kv(xm_ref, cm_ref, sm_ref, km_ref, vm_ref)
                kl_ref[...] = _rows_split_to_heads(k[TM - WINDOW:, :].T)
                vl_ref[...] = v[TM - WINDOW:, :].T

                lane = lax.broadcasted_iota(jnp.int32, (1, WINDOW), 1)
                is_new = lane >= WINDOW - DEC_SEQ
                first_lane = (t * CACHE_SEQS * DEC_SEQ) % LANES
                for new_ref, c_ref, n_ref in ((newk_ref, ck_ref, nk_ref), (newv_ref, cv_ref, nv_ref)):
                    new_t = new_ref[(t * CACHE_SEQS * DEC_SEQ) // LANES]
                    for b in range(CACHE_SEQS):
                        shifted = pltpu.roll(c_ref[b].reshape(KV_W, WINDOW), WINDOW - DEC_SEQ, 1)
                        amount = (WINDOW - DEC_SEQ - b * DEC_SEQ - first_lane) & (LANES - 1)
                        placed = pltpu.roll(new_t, amount, 1)
                        n_ref[b] = jnp.where(is_new, placed, shifted).reshape(N_KV, HEAD_DIM, WINDOW)
            rows(xm_ref, om_ref)

    return body


def _cast_specs(casts):
    in_specs, out_specs, out_shape, args = [], [], [], []
    for arr, lead, steps, src_block in casts:
        n_rows, n_cols = arr.shape[-2:]
        rows = n_rows // steps
        assert rows * steps == n_rows and rows % BF16_ROWS == 0
        blk = lambda i, steps=steps: jnp.minimum(_main_tile(i), steps - 1)
        src = blk if src_block is None else (lambda i, blk=blk, f=src_block: f(blk(i)))
        in_specs.append(pl.BlockSpec((None,) * len(lead) + (rows, n_cols),
                                     lambda i, lead=lead, src=src: lead + (src(i), 0)))
        out_specs.append(pl.BlockSpec((rows, n_cols), lambda i, blk=blk: (blk(i), 0)))
        out_shape.append(jax.ShapeDtypeStruct((n_rows, n_cols), BF16))
        args.append(arr)
    return in_specs, out_specs, out_shape, args


def _ffn_call(xm, xt, g, w_in, w_out, *, name, final_g=None, kv=None, casts=(), tm=TM):
    assert tm == TM or (kv is None and not casts)
    main = pl.BlockSpec((tm, D_MODEL), lambda i: (_main_tile(i), 0))
    in_specs = [main, _resident((N_TAIL, D_MODEL)), _resident((1, D_MODEL)),
                _resident((D_MODEL, 2 * D_FF)), _resident((D_FF, D_MODEL))]
    args = [xm, xt, g.reshape(1, D_MODEL), w_in, w_out]
    out_shape = [jax.ShapeDtypeStruct((N_MAIN, D_MODEL), F32),
                 jax.ShapeDtypeStruct((N_TAIL, D_MODEL), F32)]
    out_specs = [main, pl.BlockSpec((N_TAIL, D_MODEL), lambda i: (0, 0))]
    if final_g is not None:
        in_specs.append(_resident((1, D_MODEL)))
        args.append(final_g.reshape(1, D_MODEL))
    scratch = [pltpu.VMEM((max(N_TAIL, tm), D_FF), BF16)]
    if kv is not None:
        g_kv, w_kv, cos_m, sin_m, cos_t, sin_t, ck, cv = kv
        cached = pl.BlockSpec((CACHE_SEQS, N_KV, HEAD_DIM, WINDOW), lambda i: (_main_tile(i), 0, 0, 0))
        in_specs += [_resident((1, D_MODEL)), _resident((D_MODEL, 2 * KV_W)),
                     _main_pos_spec(), _main_pos_spec(),
                     _resident((N_TAIL, LANES)), _resident((N_TAIL, LANES)), cached, cached]
        args += [g_kv.reshape(1, D_MODEL), w_kv, cos_m, sin_m, cos_t, sin_t, ck, cv]
        out_shape += [jax.ShapeDtypeStruct((N_MAIN, KV_W), F32)] * 2
        out_shape += [jax.ShapeDtypeStruct((N_TAIL, KV_W), F32)] * 2
        out_shape += [jax.ShapeDtypeStruct((BATCH, KV_W, WINDOW), F32)] * 2
        out_shape += [jax.ShapeDtypeStruct((DEC_BATCH, N_KV, HEAD_DIM, WINDOW), F32)] * 2
        last = pl.BlockSpec((None, KV_W, WINDOW), lambda i: (_main_tile(i) // TILES_PER_SEQ, 0, 0))
        out_specs += [_main_spec(KV_W), _main_spec(KV_W),
                      pl.BlockSpec((N_TAIL, KV_W), lambda i: (0, 0)),
                      pl.BlockSpec((N_TAIL, KV_W), lambda i: (0, 0)), last, last, cached, cached]
        scratch += [pltpu.VMEM((N_SAMPLE // LANES, KV_W, LANES), F32)] * 2
    c_in, c_out, c_shape, c_args = _cast_specs(casts)
    return pl.pallas_call(
        _make_ffn_kernel(final_g is not None, kv is not None, len(casts)),
        grid=(1 + N_MAIN // tm,),
        in_specs=in_specs + c_in,
        out_specs=out_specs + c_out,
        out_shape=out_shape + c_shape,
        scratch_shapes=scratch,
        compiler_params=_params(),
        name=name,
    )(*args, *c_args)


CARRY = 8
CONV_TM = 1024


def _conv_rows(x_ref, o_ref, g, w_in_ref, kern_ref, w_out_ref, ubuf_ref, mbuf_ref, fix):
    x = x_ref[...]
    m = x.shape[0]
    hn = _rms(x, g).astype(BF16)
    for c in range(D_MODEL // OUT_CHUNK):
        cols = slice(c * OUT_CHUNK, (c + 1) * OUT_CHUNK)
        b, cc, z = (_dot(hn, w_in_ref[:, j * D_MODEL + c * OUT_CHUNK:j * D_MODEL + (c + 1) * OUT_CHUNK])
                    for j in range(3))
        u = cc * z
        ubuf_ref[CARRY:CARRY + m, cols] = u
        prev1 = ubuf_ref[CARRY - 1:CARRY - 1 + m, cols]
        prev2 = ubuf_ref[CARRY - 2:CARRY - 2 + m, cols]
        if fix is not None:
            prev1, prev2 = fix(prev1, prev2, cols)
        conv = kern_ref[0:1, cols] * prev2 + kern_ref[1:2, cols] * prev1 + kern_ref[2:3, cols] * u
        mbuf_ref[0:m, cols] = (b * conv).astype(BF16)
    for c in range(D_MODEL // OUT_CHUNK):
        cols = slice(c * OUT_CHUNK, (c + 1) * OUT_CHUNK)
        o_ref[:, cols] = x[:, cols] + _dot(mbuf_ref[0:m, :], w_out_ref[:, cols])


def _conv_kernel(xm_ref, xt_ref, g_ref, w_in_ref, kern_ref, w_out_ref, s1_ref, s2_ref,
                 om_ref, ot_ref, ut_ref, sp_ref, ubuf_ref, meta_ref, mbuf_ref):
    i = pl.program_id(0)

    @pl.when(i == 0)
    def _():
        ubuf_ref[0:CARRY, :] = jnp.zeros((CARRY, D_MODEL), F32)

        def fix(prev1, prev2, cols):
            step = lax.broadcasted_iota(jnp.int32, (N_SAMPLE, 1), 0) & (DEC_SEQ - 1)
            meta = lax.broadcasted_iota(jnp.int32, (N_META, 1), 0)
            p1 = [jnp.where(step >= 1, prev1[:N_SAMPLE], s1_ref[:, cols]),
                  jnp.where(meta >= 1, prev1[N_SAMPLE:], 0.0)]
            p2 = [jnp.where(step >= 2, prev2[:N_SAMPLE], s2_ref[:, cols]),
                  jnp.where(meta >= 2, prev2[N_SAMPLE:], 0.0)]
            return jnp.concatenate(p1, axis=0), jnp.concatenate(p2, axis=0)

        _conv_rows(xt_ref, ot_ref, g_ref[...], w_in_ref, kern_ref, w_out_ref, ubuf_ref, mbuf_ref, fix)
        ut_ref[...] = ubuf_ref[CARRY:CARRY + N_TAIL, :]
        meta_ref[...] = ubuf_ref[N_TAIL:N_TAIL + CARRY, :]

    @pl.when(i > 0)
    def _():
        t = i - 1
        per_seq = SEQ // CONV_TM

        @pl.when(t % per_seq == 0)
        def _():
            ubuf_ref[0:CARRY, :] = meta_ref[...]

        _conv_rows(xm_ref, om_ref, g_ref[...], w_in_ref, kern_ref, w_out_ref, ubuf_ref, mbuf_ref, None)
        last = ubuf_ref[CONV_TM:CONV_TM + CARRY, :]
        ubuf_ref[0:CARRY, :] = last

        @pl.when(t % per_seq == per_seq - 1)
        def _():
            sp_ref[t // per_seq] = last


def _conv_call(xm, xt, g, w_in, kern, w_out, s1, s2):
    main = pl.BlockSpec((CONV_TM, D_MODEL), lambda i: (_main_tile(i), 0))
    return pl.pallas_call(
        _conv_kernel,
        grid=(1 + N_MAIN // CONV_TM,),
        in_specs=[main, _resident((N_TAIL, D_MODEL)), _resident((1, D_MODEL)),
                  _resident((D_MODEL, 3 * D_MODEL)), _resident((CONV_W, D_MODEL)),
                  _resident((D_MODEL, D_MODEL)),
                  _resident((N_SAMPLE, D_MODEL)), _resident((N_SAMPLE, D_MODEL))],
        out_specs=[main,
                   pl.BlockSpec((N_TAIL, D_MODEL), lambda i: (0, 0)),
                   pl.BlockSpec((N_TAIL, D_MODEL), lambda i: (0, 0)),
                   pl.BlockSpec((BATCH, CARRY, D_MODEL), lambda i: (0, 0, 0))],
        out_shape=[jax.ShapeDtypeStruct((N_MAIN, D_MODEL), F32),
                   jax.ShapeDtypeStruct((N_TAIL, D_MODEL), F32),
                   jax.ShapeDtypeStruct((N_TAIL, D_MODEL), F32),
                   jax.ShapeDtypeStruct((BATCH, CARRY, D_MODEL), F32)],
        scratch_shapes=[pltpu.VMEM((CARRY + max(N_TAIL, CONV_TM), D_MODEL), F32),
                        pltpu.VMEM((CARRY, D_MODEL), F32),
                        pltpu.VMEM((max(N_TAIL, CONV_TM), D_MODEL), BF16)],
        compiler_params=_params(),
        name="conv_mixer",
    )(xm, xt, g.reshape(1, D_MODEL), w_in, kern, w_out, s1, s2)


def _attend(qp, segs, sinks_ref, heads_per_pass=1):
    mq = qp.shape[0]
    rows = GROUP * mq
    lane = lax.broadcasted_iota(jnp.int32, (1, KV_W), 1)
    qk_head = (lane & (LANES - 1)) // HALF_DIM
    pair_head = lax.broadcasted_iota(jnp.int32, (1, LANES), 1) // HEAD_DIM
    row_head = lax.broadcasted_iota(jnp.int32, (heads_per_pass * rows, 1), 0) // mq
    masks = [jnp.concatenate([seg[2]] * (heads_per_pass * GROUP), axis=0) for seg in segs]
    outs = []
    for k0 in range(0, N_KV, heads_per_pass):
        heads = range(k0, k0 + heads_per_pass)
        qs = jnp.concatenate(
            [jnp.where(qk_head == k, qp[:, g * KV_W:(g + 1) * KV_W], jnp.zeros((), BF16))
             for k in heads for g in range(GROUP)], axis=0)
        sink = jnp.full((heads_per_pass * rows, 1), sinks_ref[k0 * GROUP] * LOG2E, F32)
        for j in range(1, heads_per_pass * GROUP):
            sink = jnp.where(row_head == j, sinks_ref[k0 * GROUP + j] * LOG2E, sink)
        scores = []
        for (kb, _, _, k_transposed), msk in zip(segs, masks):
            s = _dot(qs, kb) if k_transposed else _dot_nt(qs, kb)
            scores.append(jnp.where(msk, s, NEG))
        mx = sink
        for s in scores:
            mx = jnp.maximum(mx, jnp.max(s, axis=-1, keepdims=True))
        denom = jnp.exp2(sink - mx)
        probs = []
        for s in scores:
            p = jnp.exp2(s - mx)
            denom = denom + jnp.sum(p, axis=-1, keepdims=True)
            probs.append(p.astype(BF16))
        inv = 1.0 / denom
        for n, k in enumerate(heads):
            pair = slice((k // 2) * LANES, (k // 2 + 1) * LANES)
            part = slice(n * rows, (n + 1) * rows)
            acc = None
            for p, (_, vb, _, _) in zip(probs, segs):
                pv = _dot(p[part], jnp.where(pair_head == k % 2, vb[:, pair], jnp.zeros((), BF16)))
                acc = pv if acc is None else acc + pv
            outs.append(acc * inv[part])
    out = jnp.concatenate([outs[k] + outs[k + 1] for k in range(0, N_KV, 2)], axis=1)
    return jnp.concatenate([out[g * mq:(g + 1) * mq, :] for g in range(GROUP)], axis=1)


def _queries(x, g, wq_ref, cos, sin):
    q = _dot(_rms(x, g).astype(BF16), wq_ref[...])
    return (_rope(q, cos, sin) * (HEAD_DIM ** -0.5 * LOG2E)).astype(BF16)


def _attn_main_kernel(sinks_ref, xm_ref, g_ref, wq_ref, wo_ref, cos_ref, sin_ref,
                      km_ref, vm_ref, kt_ref, vt_ref,
                      om_ref, q_ref, ao_ref, kbuf_ref, vbuf_ref):
    i = pl.program_id(0)
    start = i % (SEQ // ATT_TM) == 0

    x = xm_ref[...]
    q_ref[...] = _queries(x, g_ref[...], wq_ref, cos_ref[...], sin_ref[...])

    @pl.when(start)
    def _():
        pad = jnp.zeros((BLK - N_META, KV_W), BF16)
        kbuf_ref[0:BLK - N_META, :] = pad
        vbuf_ref[0:BLK - N_META, :] = pad
        kbuf_ref[BLK - N_META:BLK, :] = kt_ref[N_SAMPLE:N_TAIL, :].astype(BF16)
        vbuf_ref[BLK - N_META:BLK, :] = vt_ref[N_SAMPLE:N_TAIL, :].astype(BF16)

    @pl.when(jnp.logical_not(start))
    def _():
        kbuf_ref[0:BLK, :] = kbuf_ref[ATT_TM:ATT_TM + BLK, :]
        vbuf_ref[0:BLK, :] = vbuf_ref[ATT_TM:ATT_TM + BLK, :]

    kbuf_ref[BLK:BLK + ATT_TM, :] = km_ref[...].astype(BF16)
    vbuf_ref[BLK:BLK + ATT_TM, :] = vm_ref[...].astype(BF16)

    qi = lax.broadcasted_iota(jnp.int32, (BLK, 2 * BLK), 0)
    kj = lax.broadcasted_iota(jnp.int32, (BLK, 2 * BLK), 1)
    band = (kj >= qi) & (kj <= qi + WINDOW)
    first_key = jnp.where(start, BLK - N_META, 0)
    for blk in range(ATT_TM // BLK):
        lo = blk * BLK
        mask = band & (kj >= first_key) if blk == 0 else band
        att = _attend(q_ref[lo:lo + BLK, :],
                      [(kbuf_ref[lo:lo + 2 * BLK, :], vbuf_ref[lo:lo + 2 * BLK, :], mask, False)],
                      sinks_ref)
        ao_ref[lo:lo + BLK, :] = att.astype(BF16)
    om_ref[...] = x + _dot(ao_ref[...], wo_ref[...])


def _attn_main_call(sinks, xm, g, wq, wo, cos_m, sin_m, km, vm, kt, vt):
    tile = lambda w: pl.BlockSpec((ATT_TM, w), lambda i: (i, 0))
    pos = pl.BlockSpec((ATT_TM, LANES), lambda i: (i % (SEQ // ATT_TM), 0))
    return pl.pallas_call(
        _attn_main_kernel,
        grid=(N_MAIN // ATT_TM,),
        in_specs=[pl.BlockSpec(memory_space=pltpu.SMEM),
                  tile(D_MODEL), _resident((1, D_MODEL)),
                  _resident((D_MODEL, D_MODEL)), _resident((D_MODEL, D_MODEL)),
                  pos, pos, tile(KV_W), tile(KV_W),
                  _resident((N_TAIL, KV_W)), _resident((N_TAIL, KV_W))],
        out_specs=tile(D_MODEL),
        out_shape=jax.ShapeDtypeStruct((N_MAIN, D_MODEL), F32),
        scratch_shapes=[pltpu.VMEM((ATT_TM, D_MODEL), BF16), pltpu.VMEM((ATT_TM, D_MODEL), BF16),
                        pltpu.VMEM((BLK + ATT_TM, KV_W), BF16),
                        pltpu.VMEM((BLK + ATT_TM, KV_W), BF16)],
        compiler_params=_params(),
        name="attn_main",
    )(sinks, xm, g.reshape(1, D_MODEL), wq, wo, cos_m, sin_m, km, vm, kt, vt)


def _attn_tail_kernel(sinks_ref, xt_ref, g_ref, wq_ref, wo_ref, cos_ref, sin_ref,
                      kt_ref, vt_ref, ck_ref, cv_ref, ot_ref, q_ref, ao_ref):
    s = pl.program_id(0)

    @pl.when(s == 0)
    def _():
        q_ref[...] = _queries(xt_ref[...], g_ref[...], wq_ref, cos_ref[...], sin_ref[...])
        qi = lax.broadcasted_iota(jnp.int32, (N_META, N_META), 0)
        kj = lax.broadcasted_iota(jnp.int32, (N_META, N_META), 1)
        km = kt_ref[N_SAMPLE:N_TAIL, :].astype(BF16)
        vm = vt_ref[N_SAMPLE:N_TAIL, :].astype(BF16)
        att = _attend(q_ref[N_SAMPLE:N_TAIL, :], [(km, vm, kj <= qi, False)], sinks_ref, N_KV)
        ao_ref[N_SAMPLE:N_TAIL, :] = att.astype(BF16)

    rows = pl.ds(pl.multiple_of(s * GROUP_ROWS, GROUP_ROWS), GROUP_ROWS)
    k_new = kt_ref[rows, :]
    v_new = vt_ref[rows, :]
    ck = [ck_ref[b].reshape(KV_W, WINDOW) for b in range(SAMPLE_GROUP)]
    cv = [cv_ref[b].reshape(KV_W, WINDOW) for b in range(SAMPLE_GROUP)]
    k_cached_t = jnp.concatenate([_rows_heads_to_split(c) for c in ck], axis=1).astype(BF16)
    v_cached = jnp.concatenate([c.T for c in cv], axis=0).astype(BF16)

    n_cached = SAMPLE_GROUP * WINDOW
    qr = lax.broadcasted_iota(jnp.int32, (GROUP_ROWS, n_cached), 0)
    kc = lax.broadcasted_iota(jnp.int32, (GROUP_ROWS, n_cached), 1)
    mask_cached = ((kc // WINDOW) == (qr // DEC_SEQ)) & ((kc % WINDOW) >= (qr % DEC_SEQ))
    qn = lax.broadcasted_iota(jnp.int32, (GROUP_ROWS, GROUP_ROWS), 0)
    kn = lax.broadcasted_iota(jnp.int32, (GROUP_ROWS, GROUP_ROWS), 1)
    mask_new = ((kn // DEC_SEQ) == (qn // DEC_SEQ)) & ((kn % DEC_SEQ) <= (qn % DEC_SEQ))
    att = _attend(q_ref[rows, :],
                  [(k_cached_t, v_cached, mask_cached, True),
                   (k_new.astype(BF16), v_new.astype(BF16), mask_new, False)],
                  sinks_ref, N_KV)
    ao_ref[rows, :] = att.astype(BF16)

    @pl.when(s == N_GROUPS - 1)
    def _():
        ot_ref[...] = xt_ref[...] + _dot(ao_ref[...], wo_ref[...])


def _attn_tail_call(sinks, xt, g, wq, wo, cos_t, sin_t, kt, vt, ck, cv):
    cached = pl.BlockSpec((SAMPLE_GROUP, N_KV, HEAD_DIM, WINDOW), lambda s: (s, 0, 0, 0))
    return pl.pallas_call(
        _attn_tail_kernel,
        grid=(N_GROUPS,),
        in_specs=[pl.BlockSpec(memory_space=pltpu.SMEM),
                  _resident((N_TAIL, D_MODEL)), _resident((1, D_MODEL)),
                  _resident((D_MODEL, D_MODEL)), _resident((D_MODEL, D_MODEL)),
                  _resident((N_TAIL, LANES)), _resident((N_TAIL, LANES)),
                  _resident((N_TAIL, KV_W)), _resident((N_TAIL, KV_W)),
                  cached, cached],
        out_specs=pl.BlockSpec((N_TAIL, D_MODEL), lambda s: (0, 0)),
        out_shape=jax.ShapeDtypeStruct((N_TAIL, D_MODEL), F32),
        scratch_shapes=[pltpu.VMEM((N_TAIL, D_MODEL), BF16), pltpu.VMEM((N_TAIL, D_MODEL), BF16)],
        compiler_params=_params(),
        name="attn_tail",
    )(sinks, xt, g.reshape(1, D_MODEL), wq, wo, cos_t, sin_t, kt, vt, ck, cv)


def _rope_tables(pos):
    inv = 1.0 / (ROPE_THETA ** (jnp.arange(0, HEAD_DIM, 2, dtype=F32) / HEAD_DIM))
    ang = pos.astype(F32)[:, None] * inv[None, :]
    cos, sin = jnp.cos(ang), jnp.sin(ang)
    return jnp.tile(cos, (1, N_KV)), jnp.tile(sin, (1, N_KV))


def kernel(x_prompt, x_sample, state_conv, cache_k, cache_v, meta_tokens, norm_g, ffn_w_in,
           ffn_w_out, conv_w_in, conv_kernel, conv_w_out, kv_norm_g, w_kv, w_q, w_o, sinks,
           final_norm_g):
    assert x_prompt.shape == (BATCH, SEQ, D_MODEL) and x_sample.shape == (DEC_BATCH, DEC_SEQ, D_MODEL)
    assert cache_k.shape == (DEC_BATCH, WINDOW, N_KV, HEAD_DIM)
    assert norm_g.shape[0] == 2 and conv_w_in.shape[0] == 1 and w_q.shape[0] == 1

    xm = x_prompt.reshape(N_MAIN, D_MODEL)
    xt = jnp.concatenate([x_sample.reshape(N_SAMPLE, D_MODEL), meta_tokens], axis=0)

    wq = w_q[0].reshape(D_MODEL, N_KV, GROUP, 2, HALF_DIM).transpose(0, 2, 3, 1, 4)
    wq = wq.reshape(D_MODEL, D_MODEL).astype(BF16)
    wk =w_kv[:, :KV_W].reshape(D_MODEL, N_KV, 2, HALF_DIM).transpose(0, 2, 1, 3)
    wkv = jnp.concatenate([wk.reshape(D_MODEL, KV_W), w_kv[:, KV_W:]], axis=1).astype(BF16)

    pos_main = N_META + jnp.arange(SEQ, dtype=jnp.int32)
    pos_tail = jnp.concatenate([
        jnp.tile(PAST_LEN + jnp.arange(DEC_SEQ, dtype=jnp.int32), DEC_BATCH),
        jnp.arange(N_META, dtype=jnp.int32)])
    cos_m, sin_m = _rope_tables(pos_main)
    cos_t, sin_t = _rope_tables(pos_tail)

    sc = state_conv[0]
    grow = lambda a: jnp.pad(a, ((0, 0), (0, DEC_SEQ - a.shape[1]), (0, 0))).reshape(N_SAMPLE, D_MODEL)
    s1 = grow(sc[:, 1:])
    s2 = grow(sc)

    ffn_casts = lambda l, j: [(ffn_w_in, (l, j), N_TILES, None),
                              (ffn_w_out, (l, j), N_TILES // 2, None)]
    wo_cast = (w_o, (0,), N_HEADS, lambda b: (b % N_KV) * GROUP + b // N_KV)
    slot_minor = lambda a: a.transpose(0, 2, 3, 1)
    ck, cv = slot_minor(cache_k), slot_minor(cache_v)

    xm, xt, cw_in, cw_out, w_in, w_out, w_in_10, w_out_10 = _ffn_call(
        xm, xt, norm_g[0, 0], ffn_w_in[0, 0].astype(BF16), ffn_w_out[0, 0].astype(BF16),
        name="ffn_l0a",
        casts=[(conv_w_in, (0,), N_TILES, None), (conv_w_out, (0,), N_TILES, None)]
        + ffn_casts(0, 1) + ffn_casts(1, 0))
    xm, xt, u_tail, sp = _conv_call(xm, xt, norm_g[0, 1], cw_in, conv_kernel[0], cw_out, s1, s2)
    xm, xt, wo, w_in_11, w_out_11 = _ffn_call(
        xm, xt, norm_g[0, 2], w_in, w_out, name="ffn_l0b", casts=[wo_cast] + ffn_casts(1, 1))

    xm, xt, km, vm, kt, vt, k_last, v_last, nk, nv = _ffn_call(
        xm, xt, norm_g[1, 0], w_in_10, w_out_10, name="ffn_l1a_kv",
        kv=(kv_norm_g, wkv, cos_m, sin_m, cos_t, sin_t, ck, cv))
    w_in, w_out = w_in_11, w_out_11
    xt = _attn_tail_call(sinks[0], xt, norm_g[1, 1], wq, wo, cos_t, sin_t, kt, vt, ck, cv)
    xm = _attn_main_call(sinks[0], xm, norm_g[1, 1], wq, wo, cos_m, sin_m, km, vm, kt, vt)
    ym, yt = _ffn_call(xm, xt, norm_g[1, 2], w_in, w_out, name="ffn_l1b_final",
                       final_g=final_norm_g, tm=FINAL_TM)

    y_prompt = ym.reshape(BATCH, SEQ, D_MODEL)
    y_sample = yt[:N_SAMPLE].reshape(DEC_BATCH, DEC_SEQ, D_MODEL)
    new_state_conv_p = sp[:, CARRY - (CONV_W - 1):][None]
    new_state_conv_s = u_tail[:N_SAMPLE].reshape(DEC_BATCH, DEC_SEQ, D_MODEL)[:, DEC_SEQ - (CONV_W - 1):][None]
    slot_major = lambda a: a.transpose(0, 3, 1, 2)
    new_k_p = slot_major(k_last.reshape(BATCH, N_KV, HEAD_DIM, WINDOW))
    new_v_p = slot_major(v_last.reshape(BATCH, N_KV, HEAD_DIM, WINDOW))
    return (y_prompt, y_sample, new_state_conv_p, new_state_conv_s,
            new_k_p, new_v_p, slot_major(nk), slot_major(nv))
```

```python
import jax
import jax.numpy as jnp
from jax import lax
from jax.experimental import pallas as pl
from jax.experimental.pallas import tpu as pltpu

D_MODEL = 1024
BATCH = 2
SEQ = 8192
DEC_BATCH = 128
DEC_SEQ = 4
PAST_LEN = 8192
N_META = 16
D_FF = 2816
CONV_W = 3
HEAD_DIM = 64
N_HEADS = 16
N_KV = 4
GROUP = 4
WINDOW = 128
ROPE_THETA = 10000.0
EPS = 1e-6
NEG = -1e30
LOG2E = 1.4426950408889634

KV_W = N_KV * HEAD_DIM
HALF_DIM = HEAD_DIM // 2
N_MAIN = BATCH * SEQ
N_SAMPLE = DEC_BATCH * DEC_SEQ
N_TAIL = N_SAMPLE + N_META
TM = 512
N_TILES = N_MAIN // TM
TILES_PER_SEQ = SEQ // TM
FINAL_TM = 1024
FF_CHUNK = 256
OUT_CHUNK = 256
BLK = WINDOW
ATT_TM = 1024
SAMPLE_GROUP = 8
N_GROUPS = DEC_BATCH // SAMPLE_GROUP
CACHE_SEQS = DEC_BATCH // N_TILES
GROUP_ROWS = SAMPLE_GROUP * DEC_SEQ
LANES = 128
BF16_ROWS = 16
V7X_VMEM_LIMIT = 56 * 1024 * 1024

F32 = jnp.float32
BF16 = jnp.bfloat16


def _rms(x, g):
    return x * lax.rsqrt(jnp.mean(x * x, axis=-1, keepdims=True) + EPS) * g


def _rope(x, cos, sin):
    outs = []
    for c in range(x.shape[1] // KV_W):
        x1 = x[:, c * KV_W:c * KV_W + LANES]
        x2 = x[:, c * KV_W + LANES:(c + 1) * KV_W]
        outs += [x1 * cos - x2 * sin, x2 * cos + x1 * sin]
    return jnp.concatenate(outs, axis=1)


def _rows_split_to_heads(t):
    return jnp.concatenate([t[half * LANES + h * HALF_DIM:half * LANES + (h + 1) * HALF_DIM]
                            for h in range(N_KV) for half in range(2)], axis=0)


def _rows_heads_to_split(t):
    return jnp.concatenate([t[h * HEAD_DIM + half * HALF_DIM:h * HEAD_DIM + (half + 1) * HALF_DIM]
                            for half in range(2) for h in range(N_KV)], axis=0)


def _dot(a, b):
    return jnp.dot(a, b, preferred_element_type=F32)


def _dot_nt(a, b):
    return lax.dot_general(a, b, (((1,), (1,)), ((), ())), preferred_element_type=F32)


def _resident(shape):
    nd = len(shape)
    return pl.BlockSpec(shape, lambda i: (0,) * nd, pipeline_mode=pl.Buffered(1))


def _main_tile(i):
    return jnp.maximum(i - 1, 0)


def _main_spec(width):
    return pl.BlockSpec((TM, width), lambda i: (_main_tile(i), 0))


def _main_pos_spec():
    return pl.BlockSpec((TM, LANES), lambda i: (_main_tile(i) % TILES_PER_SEQ, 0))


def _params():
    return pltpu.CompilerParams(dimension_semantics=("arbitrary",),
                                vmem_limit_bytes=V7X_VMEM_LIMIT)


def _ffn_rows(x_ref, o_ref, g, w_in_ref, w_out_ref, a_ref, final_g=None):
    x = x_ref[...]
    m = x.shape[0]
    hn = _rms(x, g).astype(BF16)
    for c in range(D_FF // FF_CHUNK):
        lo = c * FF_CHUNK
        a_ref[0:m, lo:lo + FF_CHUNK] = _swiglu_chunk(
            hn, w_in_ref[:, lo:lo + FF_CHUNK], w_in_ref[:, D_FF + lo:D_FF + lo + FF_CHUNK])
    _ffn_out(x, o_ref, a_ref, w_out_ref, final_g)


def _swiglu_chunk(hn, w_gate, w_up):
    gate = _dot(hn, w_gate)
    return (gate * jax.nn.sigmoid(gate) * _dot(hn, w_up)).astype(BF16)


def _ffn_out(x, o_ref, a_ref, w_out_ref, final_g):
    m = x.shape[0]
    sq = None
    for c in range(D_MODEL // OUT_CHUNK):
        cols = slice(c * OUT_CHUNK, (c + 1) * OUT_CHUNK)
        y = x[:, cols] + 0.5 * _dot(a_ref[0:m, :], w_out_ref[:, cols])
        o_ref[:, cols] = y
        if final_g is not None:
            part = jnp.sum(y * y, axis=-1, keepdims=True)
            sq = part if sq is None else sq + part
    if final_g is not None:
        o_ref[...] = o_ref[...] * lax.rsqrt(sq * (1.0 / D_MODEL) + EPS) * final_g


N_CHUNKS = D_FF // FF_CHUNK


def _make_ffn_first_kernel(n_cast):
    def body(*refs):
        xm_ref, xt_ref, g_ref, wg_ref, wu_ref, wo_ref = refs[:6]
        cast_src = refs[6:6 + n_cast]
        om_ref, ot_ref = refs[6 + n_cast:8 + n_cast]
        cast_dst = refs[8 + n_cast:8 + 2 * n_cast]
        a_ref, w_in_ref, w_out_ref, hn_ref = refs[8 + 2 * n_cast:]
        i = pl.program_id(0)

        for src, dst in zip(cast_src, cast_dst):
            dst[...] = src[...].astype(BF16)

        @pl.when(i == 0)
        def _():
            hn_ref[...] = _rms(xt_ref[...], g_ref[...]).astype(BF16)

        for c in range(N_CHUNKS):
            @pl.when(i == c)
            def _(c=c):
                lo = c * FF_CHUNK
                w_gate = wg_ref[...].astype(BF16)
                w_up = wu_ref[...].astype(BF16)
                w_in_ref[:, lo:lo + FF_CHUNK] = w_gate
                w_in_ref[:, D_FF + lo:D_FF + lo + FF_CHUNK] = w_up
                w_out_ref[lo:lo + FF_CHUNK, :] = wo_ref[...].astype(BF16)
                a_ref[0:N_TAIL, lo:lo + FF_CHUNK] = _swiglu_chunk(hn_ref[...], w_gate, w_up)

        @pl.when(i == N_CHUNKS - 1)
        def _():
            _ffn_out(xt_ref[...], ot_ref, a_ref, w_out_ref, None)

        @pl.when(i >= N_CHUNKS)
        def _():
            _ffn_rows(xm_ref, om_ref, g_ref[...], w_in_ref, w_out_ref, a_ref)

    return body


def _ffn_first_call(xm, xt, g, ffn_w_in, ffn_w_out, lead, casts, *, name):
    chunk = lambda i: jnp.minimum(i, N_CHUNKS - 1)
    nl = (None,) * len(lead)
    main = pl.BlockSpec((TM, D_MODEL), lambda i: (jnp.maximum(i - N_CHUNKS, 0), 0))
    c_in, c_out, c_shape, c_args = _cast_specs(casts, N_CHUNKS)
    return pl.pallas_call(
        _make_ffn_first_kernel(len(casts)),
        grid=(N_CHUNKS + N_TILES,),
        in_specs=[main, _resident((N_TAIL, D_MODEL)), _resident((1, D_MODEL)),
                  pl.BlockSpec(nl + (D_MODEL, FF_CHUNK), lambda i: lead + (0, chunk(i))),
                  pl.BlockSpec(nl + (D_MODEL, FF_CHUNK), lambda i: lead + (0, N_CHUNKS + chunk(i))),
                  pl.BlockSpec(nl + (FF_CHUNK, D_MODEL), lambda i: lead + (chunk(i), 0))] + c_in,
        out_specs=[main, pl.BlockSpec((N_TAIL, D_MODEL), lambda i: (0, 0))] + c_out,
        out_shape=[jax.ShapeDtypeStruct((N_MAIN, D_MODEL), F32),
                   jax.ShapeDtypeStruct((N_TAIL, D_MODEL), F32)] + c_shape,
        scratch_shapes=[pltpu.VMEM((N_TAIL, D_FF), BF16),
                        pltpu.VMEM((D_MODEL, 2 * D_FF), BF16), pltpu.VMEM((D_FF, D_MODEL), BF16),
                        pltpu.VMEM((N_TAIL, D_MODEL), BF16)],
        compiler_params=_params(),
        name=name,
    )(xm, xt, g.reshape(1, D_MODEL), ffn_w_in, ffn_w_in, ffn_w_out, *c_args)


def _make_ffn_kernel(has_final, has_kv, n_cast):
    def body(*refs):
        refs = list(refs)
        xm_ref, xt_ref, g_ref, w_in_ref, w_out_ref = refs[:5]
        pos = 5
        if has_final:
            gf_ref = refs[pos]
            pos += 1
        if has_kv:
            gkv_ref, wkv_ref, cm_ref, sm_ref, ct_ref, st_ref, ck_ref, cv_ref = refs[pos:pos + 8]
            pos += 8
        cast_src = refs[pos:pos + n_cast]
        pos += n_cast
        om_ref, ot_ref = refs[pos:pos + 2]
        pos += 2
        if has_kv:
            km_ref, vm_ref, kt_ref, vt_ref, kl_ref, vl_ref, nk_ref, nv_ref = refs[pos:pos + 8]
            pos += 8
        cast_dst = refs[pos:pos + n_cast]
        pos += n_cast
        a_ref = refs[pos]
        if has_kv:
            newk_ref, newv_ref = refs[pos + 1:pos + 3]
        i = pl.program_id(0)

        for src, dst in zip(cast_src, cast_dst):
            dst[...] = src[...].astype(BF16)

        def rows(x_ref, o_ref):
            _ffn_rows(x_ref, o_ref, g_ref[...], w_in_ref, w_out_ref, a_ref,
                      gf_ref[...] if has_final else None)

        def project_kv(x_ref, cos_ref, sin_ref, k_ref, v_ref):
            kv = _dot(_rms(x_ref[...], gkv_ref[...]).astype(BF16), wkv_ref[...])
            k = _rope(kv[:, :KV_W], cos_ref[...], sin_ref[...])
            v = kv[:, KV_W:]
            k_ref[...] = k
            v_ref[...] = v
            return k, v

        @pl.when(i == 0)
        def _():
            if has_kv:
                k, v = project_kv(xt_ref, ct_ref, st_ref, kt_ref, vt_ref)
                for c in range(N_SAMPLE // LANES):
                    newk_ref[c] = _rows_split_to_heads(k[c * LANES:(c + 1) * LANES, :].T)
                    newv_ref[c] = v[c * LANES:(c + 1) * LANES, :].T
            rows(xt_ref, ot_ref)

        @pl.when(i > 0)
        def _():
            if has_kv:
                t = i - 1
                k, v = project_kv(xm_ref, cm_ref, sm_ref, km_ref, vm_ref)
                kl_ref[...] = _rows_split_to_heads(k[TM - WINDOW:, :].T)
                vl_ref[...] = v[TM - WINDOW:, :].T

                lane = lax.broadcasted_iota(jnp.int32, (1, WINDOW), 1)
                is_new = lane >= WINDOW - DEC_SEQ
                first_lane = (t * CACHE_SEQS * DEC_SEQ) % LANES
                for new_ref, c_ref, n_ref in ((newk_ref, ck_ref, nk_ref), (newv_ref, cv_ref, nv_ref)):
                    new_t = new_ref[(t * CACHE_SEQS * DEC_SEQ) // LANES]
                    for b in range(CACHE_SEQS):
                        shifted = pltpu.roll(c_ref[b].reshape(KV_W, WINDOW), WINDOW - DEC_SEQ, 1)
                        amount = (WINDOW - DEC_SEQ - b * DEC_SEQ - first_lane) & (LANES - 1)
                        placed = pltpu.roll(new_t, amount, 1)
                        n_ref[b] = jnp.where(is_new, placed, shifted).reshape(N_KV, HEAD_DIM, WINDOW)
            rows(xm_ref, om_ref)

    return body


def _cast_specs(casts, first_main_step=1):
    in_specs, out_specs, out_shape, args = [], [], [], []
    for arr, lead, steps, src_block in casts:
        n_rows, n_cols = arr.shape[-2:]
        rows = n_rows // steps
        assert rows * steps == n_rows and rows % BF16_ROWS == 0
        blk = lambda i, steps=steps: jnp.clip(i - first_main_step, 0, steps - 1)
        src = blk if src_block is None else (lambda i, blk=blk, f=src_block: f(blk(i)))
        in_specs.append(pl.BlockSpec((None,) * len(lead) + (rows, n_cols),
                                     lambda i, lead=lead, src=src: lead + (src(i), 0)))
        out_specs.append(pl.BlockSpec((rows, n_cols), lambda i, blk=blk: (blk(i), 0)))
        out_shape.append(jax.ShapeDtypeStruct((n_rows, n_cols), BF16))
        args.append(arr)
    return in_specs, out_specs, out_shape, args


def _ffn_call(xm, xt, g, w_in, w_out, *, name, final_g=None, kv=None, casts=(), tm=TM):
    assert tm == TM or (kv is None and not casts)
    main = pl.BlockSpec((tm, D_MODEL), lambda i: (_main_tile(i), 0))
    in_specs = [main, _resident((N_TAIL, D_MODEL)), _resident((1, D_MODEL)),
                _resident((D_MODEL, 2 * D_FF)), _resident((D_FF, D_MODEL))]
    args = [xm, xt, g.reshape(1, D_MODEL), w_in, w_out]
    out_shape = [jax.ShapeDtypeStruct((N_MAIN, D_MODEL), F32),
                 jax.ShapeDtypeStruct((N_TAIL, D_MODEL), F32)]
    out_specs = [main, pl.BlockSpec((N_TAIL, D_MODEL), lambda i: (0, 0))]
    if final_g is not None:
        in_specs.append(_resident((1, D_MODEL)))
        args.append(final_g.reshape(1, D_MODEL))
    scratch = [pltpu.VMEM((max(N_TAIL, tm), D_FF), BF16)]
    if kv is not None:
        g_kv, w_kv, cos_m, sin_m, cos_t, sin_t, ck, cv = kv
        cached = pl.BlockSpec((CACHE_SEQS, N_KV, HEAD_DIM, WINDOW), lambda i: (_main_tile(i), 0, 0, 0))
        in_specs += [_resident((1, D_MODEL)), _resident((D_MODEL, 2 * KV_W)),
                     _main_pos_spec(), _main_pos_spec(),
                     _resident((N_TAIL, LANES)), _resident((N_TAIL, LANES)), cached, cached]
        args += [g_kv.reshape(1, D_MODEL), w_kv, cos_m, sin_m, cos_t, sin_t, ck, cv]
        out_shape += [jax.ShapeDtypeStruct((N_MAIN, KV_W), F32)] * 2
        out_shape += [jax.ShapeDtypeStruct((N_TAIL, KV_W), F32)] * 2
        out_shape += [jax.ShapeDtypeStruct((BATCH, KV_W, WINDOW), F32)] * 2
        out_shape += [jax.ShapeDtypeStruct((DEC_BATCH, N_KV, HEAD_DIM, WINDOW), F32)] * 2
        last = pl.BlockSpec((None, KV_W, WINDOW), lambda i: (_main_tile(i) // TILES_PER_SEQ, 0, 0))
        out_specs += [_main_spec(KV_W), _main_spec(KV_W),
                      pl.BlockSpec((N_TAIL, KV_W), lambda i: (0, 0)),
                      pl.BlockSpec((N_TAIL, KV_W), lambda i: (0, 0)), last, last, cached, cached]
        scratch += [pltpu.VMEM((N_SAMPLE // LANES, KV_W, LANES), F32)] * 2
    c_in, c_out, c_shape, c_args = _cast_specs(casts)
    return pl.pallas_call(
        _make_ffn_kernel(final_g is not None, kv is not None, len(casts)),
        grid=(1 + N_MAIN // tm,),
        in_specs=in_specs + c_in,
        out_specs=out_specs + c_out,
        out_shape=out_shape + c_shape,
        scratch_shapes=scratch,
        compiler_params=_params(),
        name=name,
    )(*args, *c_args)


CARRY = 8
CONV_TM = 1024


def _conv_rows(x_ref, o_ref, g, w_in_ref, kern_ref, w_out_ref, ubuf_ref, mbuf_ref, fix):
    x = x_ref[...]
    m = x.shape[0]
    hn = _rms(x, g).astype(BF16)
    for c in range(D_MODEL // OUT_CHUNK):
        cols = slice(c * OUT_CHUNK, (c + 1) * OUT_CHUNK)
        b, cc, z = (_dot(hn, w_in_ref[:, j * D_MODEL + c * OUT_CHUNK:j * D_MODEL + (c + 1) * OUT_CHUNK])
                    for j in range(3))
        u = cc * z
        ubuf_ref[CARRY:CARRY + m, cols] = u
        prev1 = ubuf_ref[CARRY - 1:CARRY - 1 + m, cols]
        prev2 = ubuf_ref[CARRY - 2:CARRY - 2 + m, cols]
        if fix is not None:
            prev1, prev2 = fix(prev1, prev2, cols)
        conv = kern_ref[0:1, cols] * prev2 + kern_ref[1:2, cols] * prev1 + kern_ref[2:3, cols] * u
        mbuf_ref[0:m, cols] = (b * conv).astype(BF16)
    for c in range(D_MODEL // OUT_CHUNK):
        cols = slice(c * OUT_CHUNK, (c + 1) * OUT_CHUNK)
        o_ref[:, cols] = x[:, cols] + _dot(mbuf_ref[0:m, :], w_out_ref[:, cols])


def _conv_kernel(xm_ref, xt_ref, g_ref, w_in_ref, kern_ref, w_out_ref, s1_ref, s2_ref,
                 om_ref, ot_ref, ut_ref, sp_ref, ubuf_ref, meta_ref, mbuf_ref):
    i = pl.program_id(0)

    @pl.when(i == 0)
    def _():
        ubuf_ref[0:CARRY, :] = jnp.zeros((CARRY, D_MODEL), F32)

        def fix(prev1, prev2, cols):
            step = lax.broadcasted_iota(jnp.int32, (N_SAMPLE, 1), 0) & (DEC_SEQ - 1)
            meta = lax.broadcasted_iota(jnp.int32, (N_META, 1), 0)
            p1 = [jnp.where(step >= 1, prev1[:N_SAMPLE], s1_ref[:, cols]),
                  jnp.where(meta >= 1, prev1[N_SAMPLE:], 0.0)]
            p2 = [jnp.where(step >= 2, prev2[:N_SAMPLE], s2_ref[:, cols]),
                  jnp.where(meta >= 2, prev2[N_SAMPLE:], 0.0)]
            return jnp.concatenate(p1, axis=0), jnp.concatenate(p2, axis=0)

        _conv_rows(xt_ref, ot_ref, g_ref[...], w_in_ref, kern_ref, w_out_ref, ubuf_ref, mbuf_ref, fix)
        ut_ref[...] = ubuf_ref[CARRY:CARRY + N_TAIL, :]
        meta_ref[...] = ubuf_ref[N_TAIL:N_TAIL + CARRY, :]

    @pl.when(i > 0)
    def _():
        t = i - 1
        per_seq = SEQ // CONV_TM

        @pl.when(t % per_seq == 0)
        def _():
            ubuf_ref[0:CARRY, :] = meta_ref[...]

        _conv_rows(xm_ref, om_ref, g_ref[...], w_in_ref, kern_ref, w_out_ref, ubuf_ref, mbuf_ref, None)
        last = ubuf_ref[CONV_TM:CONV_TM + CARRY, :]
        ubuf_ref[0:CARRY, :] = last

        @pl.when(t % per_seq == per_seq - 1)
        def _():
            sp_ref[t // per_seq] = last


def _conv_call(xm, xt, g, w_in, kern, w_out, s1, s2):
    main = pl.BlockSpec((CONV_TM, D_MODEL), lambda i: (_main_tile(i), 0))
    return pl.pallas_call(
        _conv_kernel,
        grid=(1 + N_MAIN // CONV_TM,),
        in_specs=[main, _resident((N_TAIL, D_MODEL)), _resident((1, D_MODEL)),
                  _resident((D_MODEL, 3 * D_MODEL)), _resident((CONV_W, D_MODEL)),
                  _resident((D_MODEL, D_MODEL)),
                  _resident((N_SAMPLE, D_MODEL)), _resident((N_SAMPLE, D_MODEL))],
        out_specs=[main,
                   pl.BlockSpec((N_TAIL, D_MODEL), lambda i: (0, 0)),
                   pl.BlockSpec((N_TAIL, D_MODEL), lambda i: (0, 0)),
                   pl.BlockSpec((BATCH, CARRY, D_MODEL), lambda i: (0, 0, 0))],
        out_shape=[jax.ShapeDtypeStruct((N_MAIN, D_MODEL), F32),
                   jax.ShapeDtypeStruct((N_TAIL, D_MODEL), F32),
                   jax.ShapeDtypeStruct((N_TAIL, D_MODEL), F32),
                   jax.ShapeDtypeStruct((BATCH, CARRY, D_MODEL), F32)],
        scratch_shapes=[pltpu.VMEM((CARRY + max(N_TAIL, CONV_TM), D_MODEL), F32),
                        pltpu.VMEM((CARRY, D_MODEL), F32),
                        pltpu.VMEM((max(N_TAIL, CONV_TM), D_MODEL), BF16)],
        compiler_params=_params(),
        name="conv_mixer",
    )(xm, xt, g.reshape(1, D_MODEL), w_in, kern, w_out, s1, s2)


def _attend(qp, segs, sinks_ref, heads_per_pass=1):
    mq = qp.shape[0]
    rows = GROUP * mq
    lane = lax.broadcasted_iota(jnp.int32, (1, KV_W), 1)
    qk_head = (lane & (LANES - 1)) // HALF_DIM
    pair_head = lax.broadcasted_iota(jnp.int32, (1, LANES), 1) // HEAD_DIM
    row_head = lax.broadcasted_iota(jnp.int32, (heads_per_pass * rows, 1), 0) // mq
    masks = [jnp.concatenate([seg[2]] * (heads_per_pass * GROUP), axis=0) for seg in segs]
    outs = []
    for k0 in range(0, N_KV, heads_per_pass):
        heads = range(k0, k0 + heads_per_pass)
        qs = jnp.concatenate(
            [jnp.where(qk_head == k, qp[:, g * KV_W:(g + 1) * KV_W], jnp.zeros((), BF16))
             for k in heads for g in range(GROUP)], axis=0)
        sink = jnp.full((heads_per_pass * rows, 1), sinks_ref[k0 * GROUP] * LOG2E, F32)
        for j in range(1, heads_per_pass * GROUP):
            sink = jnp.where(row_head == j, sinks_ref[k0 * GROUP + j] * LOG2E, sink)
        scores = []
        for (kb, _, _, k_transposed), msk in zip(segs, masks):
            s = _dot(qs, kb) if k_transposed else _dot_nt(qs, kb)
            scores.append(jnp.where(msk, s, NEG))
        mx = sink
        for s in scores:
            mx = jnp.maximum(mx, jnp.max(s, axis=-1, keepdims=True))
        denom = jnp.exp2(sink - mx)
        probs = []
        for s in scores:
            p = jnp.exp2(s - mx)
            denom = denom + jnp.sum(p, axis=-1, keepdims=True)
            probs.append(p.astype(BF16))
        inv = 1.0 / denom
        for n, k in enumerate(heads):
            pair = slice((k // 2) * LANES, (k // 2 + 1) * LANES)
            part = slice(n * rows, (n + 1) * rows)
            acc = None
            for p, (_, vb, _, _) in zip(probs, segs):
                pv = _dot(p[part], jnp.where(pair_head == k % 2, vb[:, pair], jnp.zeros((), BF16)))
                acc = pv if acc is None else acc + pv
            outs.append(acc * inv[part])
    out = jnp.concatenate([outs[k] + outs[k + 1] for k in range(0, N_KV, 2)], axis=1)
    return jnp.concatenate([out[g * mq:(g + 1) * mq, :] for g in range(GROUP)], axis=1)


def _queries(x, g, wq_ref, cos, sin):
    q = _dot(_rms(x, g).astype(BF16), wq_ref[...])
    return (_rope(q, cos, sin) * (HEAD_DIM ** -0.5 * LOG2E)).astype(BF16)


def _attn_main_kernel(sinks_ref, xm_ref, g_ref, wq_ref, wo_ref, cos_ref, sin_ref,
                      km_ref, vm_ref, kt_ref, vt_ref,
                      om_ref, q_ref, ao_ref, kbuf_ref, vbuf_ref):
    i = pl.program_id(0)
    start = i % (SEQ // ATT_TM) == 0

    x = xm_ref[...]
    q_ref[...] = _queries(x, g_ref[...], wq_ref, cos_ref[...], sin_ref[...])

    @pl.when(start)
    def _():
        pad = jnp.zeros((BLK - N_META, KV_W), BF16)
        kbuf_ref[0:BLK - N_META, :] = pad
        vbuf_ref[0:BLK - N_META, :] = pad
        kbuf_ref[BLK - N_META:BLK, :] = kt_ref[N_SAMPLE:N_TAIL, :].astype(BF16)
        vbuf_ref[BLK - N_META:BLK, :] = vt_ref[N_SAMPLE:N_TAIL, :].astype(BF16)

    @pl.when(jnp.logical_not(start))
    def _():
        kbuf_ref[0:BLK, :] = kbuf_ref[ATT_TM:ATT_TM + BLK, :]
        vbuf_ref[0:BLK, :] = vbuf_ref[ATT_TM:ATT_TM + BLK, :]

    kbuf_ref[BLK:BLK + ATT_TM, :] = km_ref[...].astype(BF16)
    vbuf_ref[BLK:BLK + ATT_TM, :] = vm_ref[...].astype(BF16)

    qi = lax.broadcasted_iota(jnp.int32, (BLK, 2 * BLK), 0)
    kj = lax.broadcasted_iota(jnp.int32, (BLK, 2 * BLK), 1)
    band = (kj >= qi) & (kj <= qi + WINDOW)
    first_key = jnp.where(start, BLK - N_META, 0)
    for blk in range(ATT_TM // BLK):
        lo = blk * BLK
        mask = band & (kj >= first_key) if blk == 0 else band
        att = _attend(q_ref[lo:lo + BLK, :],
                      [(kbuf_ref[lo:lo + 2 * BLK, :], vbuf_ref[lo:lo + 2 * BLK, :], mask, False)],
                      sinks_ref)
        ao_ref[lo:lo + BLK, :] = att.astype(BF16)
    om_ref[...] = x + _dot(ao_ref[...], wo_ref[...])


def _attn_main_call(sinks, xm, g, wq, wo, cos_m, sin_m, km, vm, kt, vt):
    tile = lambda w: pl.BlockSpec((ATT_TM, w), lambda i: (i, 0))
    pos = pl.BlockSpec((ATT_TM, LANES), lambda i: (i % (SEQ // ATT_TM), 0))
    return pl.pallas_call(
        _attn_main_kernel,
        grid=(N_MAIN // ATT_TM,),
        in_specs=[pl.BlockSpec(memory_space=pltpu.SMEM),
                  tile(D_MODEL), _resident((1, D_MODEL)),
                  _resident((D_MODEL, D_MODEL)), _resident((D_MODEL, D_MODEL)),
                  pos, pos, tile(KV_W), tile(KV_W),
                  _resident((N_TAIL, KV_W)), _resident((N_TAIL, KV_W))],
        out_specs=tile(D_MODEL),
        out_shape=jax.ShapeDtypeStruct((N_MAIN, D_MODEL), F32),
        scratch_shapes=[pltpu.VMEM((ATT_TM, D_MODEL), BF16), pltpu.VMEM((ATT_TM, D_MODEL), BF16),
                        pltpu.VMEM((BLK + ATT_TM, KV_W), BF16),
                        pltpu.VMEM((BLK + ATT_TM, KV_W), BF16)],
        compiler_params=_params(),
        name="attn_main",
    )(sinks, xm, g.reshape(1, D_MODEL), wq, wo, cos_m, sin_m, km, vm, kt, vt)


def _attn_tail_kernel(sinks_ref, xt_ref, g_ref, wq_ref, wo_ref, cos_ref, sin_ref,
                      kt_ref, vt_ref, ck_ref, cv_ref, ot_ref, q_ref, ao_ref):
    s = pl.program_id(0)

    @pl.when(s == 0)
    def _():
        q_ref[...] = _queries(xt_ref[...], g_ref[...], wq_ref, cos_ref[...], sin_ref[...])
        qi = lax.broadcasted_iota(jnp.int32, (N_META, N_META), 0)
        kj = lax.broadcasted_iota(jnp.int32, (N_META, N_META), 1)
        km = kt_ref[N_SAMPLE:N_TAIL, :].astype(BF16)
        vm = vt_ref[N_SAMPLE:N_TAIL, :].astype(BF16)
        att = _attend(q_ref[N_SAMPLE:N_TAIL, :], [(km, vm, kj <= qi, False)], sinks_ref, N_KV)
        ao_ref[N_SAMPLE:N_TAIL, :] = att.astype(BF16)

    rows = pl.ds(pl.multiple_of(s * GROUP_ROWS, GROUP_ROWS), GROUP_ROWS)
    k_new = kt_ref[rows, :]
    v_new = vt_ref[rows, :]
    ck = [ck_ref[b].reshape(KV_W, WINDOW) for b in range(SAMPLE_GROUP)]
    cv = [cv_ref[b].reshape(KV_W, WINDOW) for b in range(SAMPLE_GROUP)]
    k_cached_t = jnp.concatenate([_rows_heads_to_split(c) for c in ck], axis=1).astype(BF16)
    v_cached = jnp.concatenate([c.T for c in cv], axis=0).astype(BF16)

    n_cached = SAMPLE_GROUP * WINDOW
    qr = lax.broadcasted_iota(jnp.int32, (GROUP_ROWS, n_cached), 0)
    kc = lax.broadcasted_iota(jnp.int32, (GROUP_ROWS, n_cached), 1)
    mask_cached = ((kc // WINDOW) == (qr // DEC_SEQ)) & ((kc % WINDOW) >= (qr % DEC_SEQ))
    qn = lax.broadcasted_iota(jnp.int32, (GROUP_ROWS, GROUP_ROWS), 0)
    kn = lax.broadcasted_iota(jnp.int32, (GROUP_ROWS, GROUP_ROWS), 1)
    mask_new = ((kn // DEC_SEQ) == (qn // DEC_SEQ)) & ((kn % DEC_SEQ) <= (qn % DEC_SEQ))
    att = _attend(q_ref[rows, :],
                  [(k_cached_t, v_cached, mask_cached, True),
                   (k_new.astype(BF16), v_new.astype(BF16), mask_new, False)],
                  sinks_ref, N_KV)
    ao_ref[rows, :] = att.astype(BF16)

    @pl.when(s == N_GROUPS - 1)
    def _():
        ot_ref[...] = xt_ref[...] + _dot(ao_ref[...], wo_ref[...])


def _attn_tail_call(sinks, xt, g, wq, wo, cos_t, sin_t, kt, vt, ck, cv):
    cached = pl.BlockSpec((SAMPLE_GROUP, N_KV, HEAD_DIM, WINDOW), lambda s: (s, 0, 0, 0))
    return pl.pallas_call(
        _attn_tail_kernel,
        grid=(N_GROUPS,),
        in_specs=[pl.BlockSpec(memory_space=pltpu.SMEM),
                  _resident((N_TAIL, D_MODEL)), _resident((1, D_MODEL)),
                  _resident((D_MODEL, D_MODEL)), _resident((D_MODEL, D_MODEL)),
                  _resident((N_TAIL, LANES)), _resident((N_TAIL, LANES)),
                  _resident((N_TAIL, KV_W)), _resident((N_TAIL, KV_W)),
                  cached, cached],
        out_specs=pl.BlockSpec((N_TAIL, D_MODEL), lambda s: (0, 0)),
        out_shape=jax.ShapeDtypeStruct((N_TAIL, D_MODEL), F32),
        scratch_shapes=[pltpu.VMEM((N_TAIL, D_MODEL), BF16), pltpu.VMEM((N_TAIL, D_MODEL), BF16)],
        compiler_params=_params(),
        name="attn_tail",
    )(sinks, xt, g.reshape(1, D_MODEL), wq, wo, cos_t, sin_t, kt, vt, ck, cv)


def _rope_tables(pos):
    inv = 1.0 / (ROPE_THETA ** (jnp.arange(0, HEAD_DIM, 2, dtype=F32) / HEAD_DIM))
    ang = pos.astype(F32)[:, None] * inv[None, :]
    cos, sin = jnp.cos(ang), jnp.sin(ang)
    return jnp.tile(cos, (1, N_KV)), jnp.tile(sin, (1, N_KV))


def kernel(x_prompt, x_sample, state_conv, cache_k, cache_v, meta_tokens, norm_g, ffn_w_in,
           ffn_w_out, conv_w_in, conv_kernel, conv_w_out, kv_norm_g, w_kv, w_q, w_o, sinks,
           final_norm_g):
    assert x_prompt.shape == (BATCH, SEQ, D_MODEL) and x_sample.shape == (DEC_BATCH, DEC_SEQ, D_MODEL)
    assert cache_k.shape == (DEC_BATCH, WINDOW, N_KV, HEAD_DIM)
    assert norm_g.shape[0] == 2 and conv_w_in.shape[0] == 1 and w_q.shape[0] == 1

    xm = x_prompt.reshape(N_MAIN, D_MODEL)
    xt = jnp.concatenate([x_sample.reshape(N_SAMPLE, D_MODEL), meta_tokens], axis=0)

    wq = w_q[0].reshape(D_MODEL, N_KV, GROUP, 2, HALF_DIM).transpose(0, 2, 3, 1, 4)
    wq = wq.reshape(D_MODEL, D_MODEL).astype(BF16)
    wk =w_kv[:, :KV_W].reshape(D_MODEL, N_KV, 2, HALF_DIM).transpose(0, 2, 1, 3)
    wkv = jnp.concatenate([wk.reshape(D_MODEL, KV_W), w_kv[:, KV_W:]], axis=1).astype(BF16)

    pos_main = N_META + jnp.arange(SEQ, dtype=jnp.int32)
    pos_tail = jnp.concatenate([
        jnp.tile(PAST_LEN + jnp.arange(DEC_SEQ, dtype=jnp.int32), DEC_BATCH),
        jnp.arange(N_META, dtype=jnp.int32)])
    cos_m, sin_m = _rope_tables(pos_main)
    cos_t, sin_t = _rope_tables(pos_tail)

    sc = state_conv[0]
    grow = lambda a: jnp.pad(a, ((0, 0), (0, DEC_SEQ - a.shape[1]), (0, 0))).reshape(N_SAMPLE, D_MODEL)
    s1 = grow(sc[:, 1:])
    s2 = grow(sc)

    ffn_casts = lambda l, j: [(ffn_w_in, (l, j), N_TILES, None),
                              (ffn_w_out, (l, j), N_TILES // 2, None)]
    wo_cast = (w_o, (0,), N_HEADS, lambda b: (b % N_KV) * GROUP + b // N_KV)
    slot_minor = lambda a: a.transpose(0, 2, 3, 1)
    ck, cv = slot_minor(cache_k), slot_minor(cache_v)

    xm, xt, cw_in, cw_out, w_in, w_out, w_in_10, w_out_10 = _ffn_first_call(
        xm, xt, norm_g[0, 0], ffn_w_in, ffn_w_out, (0, 0),
        [(conv_w_in, (0,), N_TILES, None), (conv_w_out, (0,), N_TILES, None)]
        + ffn_casts(0, 1) + ffn_casts(1, 0), name="ffn_l0a")
    xm, xt, u_tail, sp = _conv_call(xm, xt, norm_g[0, 1], cw_in, conv_kernel[0], cw_out, s1, s2)
    xm, xt, wo, w_in_11, w_out_11 = _ffn_call(
        xm, xt, norm_g[0, 2], w_in, w_out, name="ffn_l0b", casts=[wo_cast] + ffn_casts(1, 1))

    xm, xt, km, vm, kt, vt, k_last, v_last, nk, nv = _ffn_call(
        xm, xt, norm_g[1, 0], w_in_10, w_out_10, name="ffn_l1a_kv",
        kv=(kv_norm_g, wkv, cos_m, sin_m, cos_t, sin_t, ck, cv))
    w_in, w_out = w_in_11, w_out_11
    xt = _attn_tail_call(sinks[0], xt, norm_g[1, 1], wq, wo, cos_t, sin_t, kt, vt, ck, cv)
    xm = _attn_main_call(sinks[0], xm, norm_g[1, 1], wq, wo, cos_m, sin_m, km, vm, kt, vt)
    ym, yt = _ffn_call(xm, xt, norm_g[1, 2], w_in, w_out, name="ffn_l1b_final",
                       final_g=final_norm_g, tm=FINAL_TM)

    y_prompt = ym.reshape(BATCH, SEQ, D_MODEL)
    y_sample = yt[:N_SAMPLE].reshape(DEC_BATCH, DEC_SEQ, D_MODEL)
    new_state_conv_p = sp[:, CARRY - (CONV_W - 1):][None]
    new_state_conv_s = u_tail[:N_SAMPLE].reshape(DEC_BATCH, DEC_SEQ, D_MODEL)[:, DEC_SEQ - (CONV_W - 1):][None]
    slot_major = lambda a: a.transpose(0, 3, 1, 2)
    new_k_p = slot_major(k_last.reshape(BATCH, N_KV, HEAD_DIM, WINDOW))
    new_v_p = slot_major(v_last.reshape(BATCH, N_KV, HEAD_DIM, WINDOW))
    return (y_prompt, y_sample, new_state_conv_p, new_state_conv_s,
            new_k_p, new_v_p, slot_major(nk), slot_major(nv))
```

```python
import jax
import jax.numpy as jnp
from jax import lax
from jax.experimental import pallas as pl
from jax.experimental.pallas import tpu as pltpu

D_MODEL = 1024
BATCH = 2
SEQ = 8192
DEC_BATCH = 128
DEC_SEQ = 4
PAST_LEN = 8192
N_META = 16
D_FF = 2816
CONV_W = 3
HEAD_DIM = 64
N_HEADS = 16
N_KV = 4
GROUP = 4
WINDOW = 128
ROPE_THETA = 10000.0
EPS = 1e-6
NEG = -1e30
LOG2E = 1.4426950408889634

KV_W = N_KV * HEAD_DIM
HALF_DIM = HEAD_DIM // 2
N_MAIN = BATCH * SEQ
N_SAMPLE = DEC_BATCH * DEC_SEQ
N_TAIL = N_SAMPLE + N_META
TM = 512
N_TILES = N_MAIN // TM
TILES_PER_SEQ = SEQ // TM
WIDE_TM = 1024
FF_CHUNK = 256
OUT_CHUNK = 256
BLK = WINDOW
ATT_TM = 1024
SAMPLE_GROUP = 8
N_GROUPS = DEC_BATCH // SAMPLE_GROUP
GROUP_ROWS = SAMPLE_GROUP * DEC_SEQ
LANES = 128
BF16_ROWS = 16
V7X_VMEM_LIMIT = 56 * 1024 * 1024

F32 = jnp.float32
BF16 = jnp.bfloat16


def _rms(x, g):
    return x * lax.rsqrt(jnp.mean(x * x, axis=-1, keepdims=True) + EPS) * g


def _rope(x, cos, sin):
    outs = []
    for c in range(x.shape[1] // KV_W):
        x1 = x[:, c * KV_W:c * KV_W + LANES]
        x2 = x[:, c * KV_W + LANES:(c + 1) * KV_W]
        outs += [x1 * cos - x2 * sin, x2 * cos + x1 * sin]
    return jnp.concatenate(outs, axis=1)


def _rows_split_to_heads(t):
    return jnp.concatenate([t[half * LANES + h * HALF_DIM:half * LANES + (h + 1) * HALF_DIM]
                            for h in range(N_KV) for half in range(2)], axis=0)


def _rows_heads_to_split(t):
    return jnp.concatenate([t[h * HEAD_DIM + half * HALF_DIM:h * HEAD_DIM + (half + 1) * HALF_DIM]
                            for half in range(2) for h in range(N_KV)], axis=0)


def _dot(a, b):
    return jnp.dot(a, b, preferred_element_type=F32)


def _dot_nt(a, b):
    return lax.dot_general(a, b, (((1,), (1,)), ((), ())), preferred_element_type=F32)


def _resident(shape):
    nd = len(shape)
    return pl.BlockSpec(shape, lambda i: (0,) * nd, pipeline_mode=pl.Buffered(1))


def _main_tile(i):
    return jnp.maximum(i - 1, 0)


def _params():
    return pltpu.CompilerParams(dimension_semantics=("arbitrary",),
                                vmem_limit_bytes=V7X_VMEM_LIMIT)


def _ffn_rows(x_ref, o_ref, g, w_in_ref, w_out_ref, a_ref, final_g=None):
    x = x_ref[...]
    m = x.shape[0]
    hn = _rms(x, g).astype(BF16)
    for c in range(D_FF // FF_CHUNK):
        lo = c * FF_CHUNK
        a_ref[0:m, lo:lo + FF_CHUNK] = _swiglu_chunk(
            hn, w_in_ref[:, lo:lo + FF_CHUNK], w_in_ref[:, D_FF + lo:D_FF + lo + FF_CHUNK])
    _ffn_out(x, o_ref, a_ref, w_out_ref, final_g)


def _swiglu_chunk(hn, w_gate, w_up):
    gate = _dot(hn, w_gate)
    return (gate * jax.nn.sigmoid(gate) * _dot(hn, w_up)).astype(BF16)


def _ffn_out(x, o_ref, a_ref, w_out_ref, final_g):
    m = x.shape[0]
    sq = None
    for c in range(D_MODEL // OUT_CHUNK):
        cols = slice(c * OUT_CHUNK, (c + 1) * OUT_CHUNK)
        y = x[:, cols] + 0.5 * _dot(a_ref[0:m, :], w_out_ref[:, cols])
        o_ref[:, cols] = y
        if final_g is not None:
            part = jnp.sum(y * y, axis=-1, keepdims=True)
            sq = part if sq is None else sq + part
    if final_g is not None:
        o_ref[...] = o_ref[...] * lax.rsqrt(sq * (1.0 / D_MODEL) + EPS) * final_g


N_CHUNKS = D_FF // FF_CHUNK


def _make_ffn_first_kernel(n_cast):
    def body(*refs):
        xm_ref, xt_ref, g_ref, wg_ref, wu_ref, wo_ref = refs[:6]
        cast_src = refs[6:6 + n_cast]
        om_ref, ot_ref = refs[6 + n_cast:8 + n_cast]
        cast_dst = refs[8 + n_cast:8 + 2 * n_cast]
        a_ref, w_in_ref, w_out_ref, hn_ref = refs[8 + 2 * n_cast:]
        i = pl.program_id(0)

        for src, dst in zip(cast_src, cast_dst):
            dst[...] = src[...].astype(BF16)

        @pl.when(i == 0)
        def _():
            hn_ref[...] = _rms(xt_ref[...], g_ref[...]).astype(BF16)

        for c in range(N_CHUNKS):
            @pl.when(i == c)
            def _(c=c):
                lo = c * FF_CHUNK
                w_gate = wg_ref[...].astype(BF16)
                w_up = wu_ref[...].astype(BF16)
                w_in_ref[:, lo:lo + FF_CHUNK] = w_gate
                w_in_ref[:, D_FF + lo:D_FF + lo + FF_CHUNK] = w_up
                w_out_ref[lo:lo + FF_CHUNK, :] = wo_ref[...].astype(BF16)
                a_ref[0:N_TAIL, lo:lo + FF_CHUNK] = _swiglu_chunk(hn_ref[...], w_gate, w_up)

        @pl.when(i == N_CHUNKS - 1)
        def _():
            _ffn_out(xt_ref[...], ot_ref, a_ref, w_out_ref, None)

        @pl.when(i >= N_CHUNKS)
        def _():
            _ffn_rows(xm_ref, om_ref, g_ref[...], w_in_ref, w_out_ref, a_ref)

    return body


def _ffn_first_call(xm, xt, g, ffn_w_in, ffn_w_out, lead, casts, *, name):
    chunk = lambda i: jnp.minimum(i, N_CHUNKS - 1)
    nl = (None,) * len(lead)
    main = pl.BlockSpec((TM, D_MODEL), lambda i: (jnp.maximum(i - N_CHUNKS, 0), 0))
    c_in, c_out, c_shape, c_args = _cast_specs(casts, N_CHUNKS)
    return pl.pallas_call(
        _make_ffn_first_kernel(len(casts)),
        grid=(N_CHUNKS + N_TILES,),
        in_specs=[main, _resident((N_TAIL, D_MODEL)), _resident((1, D_MODEL)),
                  pl.BlockSpec(nl + (D_MODEL, FF_CHUNK), lambda i: lead + (0, chunk(i))),
                  pl.BlockSpec(nl + (D_MODEL, FF_CHUNK), lambda i: lead + (0, N_CHUNKS + chunk(i))),
                  pl.BlockSpec(nl + (FF_CHUNK, D_MODEL), lambda i: lead + (chunk(i), 0))] + c_in,
        out_specs=[main, pl.BlockSpec((N_TAIL, D_MODEL), lambda i: (0, 0))] + c_out,
        out_shape=[jax.ShapeDtypeStruct((N_MAIN, D_MODEL), F32),
                   jax.ShapeDtypeStruct((N_TAIL, D_MODEL), F32)] + c_shape,
        scratch_shapes=[pltpu.VMEM((N_TAIL, D_FF), BF16),
                        pltpu.VMEM((D_MODEL, 2 * D_FF), BF16), pltpu.VMEM((D_FF, D_MODEL), BF16),
                        pltpu.VMEM((N_TAIL, D_MODEL), BF16)],
        compiler_params=_params(),
        name=name,
    )(xm, xt, g.reshape(1, D_MODEL), ffn_w_in, ffn_w_in, ffn_w_out, *c_args)


def _make_ffn_kernel(has_final, has_kv, n_cast, tm):
    cache_seqs = DEC_BATCH * tm // N_MAIN

    def body(*refs):
        refs = list(refs)
        xm_ref, xt_ref, g_ref, w_in_ref, w_out_ref = refs[:5]
        pos = 5
        if has_final:
            gf_ref = refs[pos]
            pos += 1
        if has_kv:
            gkv_ref, wkv_ref, cm_ref, sm_ref, ct_ref, st_ref, ck_ref, cv_ref = refs[pos:pos + 8]
            pos += 8
        cast_src = refs[pos:pos + n_cast]
        pos += n_cast
        om_ref, ot_ref = refs[pos:pos + 2]
        pos += 2
        if has_kv:
            km_ref, vm_ref, kt_ref, vt_ref, kl_ref, vl_ref, nk_ref, nv_ref = refs[pos:pos + 8]
            pos += 8
        cast_dst = refs[pos:pos + n_cast]
        pos += n_cast
        a_ref = refs[pos]
        if has_kv:
            newk_ref, newv_ref = refs[pos + 1:pos + 3]
        i = pl.program_id(0)

        for src, dst in zip(cast_src, cast_dst):
            dst[...] = src[...].astype(BF16)

        def rows(x_ref, o_ref):
            _ffn_rows(x_ref, o_ref, g_ref[...], w_in_ref, w_out_ref, a_ref,
                      gf_ref[...] if has_final else None)

        def project_kv(x_ref, cos_ref, sin_ref, k_ref, v_ref):
            kv = _dot(_rms(x_ref[...], gkv_ref[...]).astype(BF16), wkv_ref[...])
            k = _rope(kv[:, :KV_W], cos_ref[...], sin_ref[...])
            v = kv[:, KV_W:]
            k_ref[...] = k
            v_ref[...] = v
            return k, v

        @pl.when(i == 0)
        def _():
            if has_kv:
                k, v = project_kv(xt_ref, ct_ref, st_ref, kt_ref, vt_ref)
                for c in range(N_SAMPLE // LANES):
                    newk_ref[c] = _rows_split_to_heads(k[c * LANES:(c + 1) * LANES, :].T)
                    newv_ref[c] = v[c * LANES:(c + 1) * LANES, :].T
            rows(xt_ref, ot_ref)

        @pl.when(i > 0)
        def _():
            if has_kv:
                t = i - 1
                k, v = project_kv(xm_ref, cm_ref, sm_ref, km_ref, vm_ref)
                kl_ref[...] = _rows_split_to_heads(k[tm - WINDOW:, :].T)
                vl_ref[...] = v[tm - WINDOW:, :].T

                lane = lax.broadcasted_iota(jnp.int32, (1, WINDOW), 1)
                is_new = lane >= WINDOW - DEC_SEQ
                first_lane = (t * cache_seqs * DEC_SEQ) % LANES
                for new_ref, c_ref, n_ref in ((newk_ref, ck_ref, nk_ref), (newv_ref, cv_ref, nv_ref)):
                    new_t = new_ref[(t * cache_seqs * DEC_SEQ) // LANES]
                    for b in range(cache_seqs):
                        shifted = pltpu.roll(c_ref[b].reshape(KV_W, WINDOW), WINDOW - DEC_SEQ, 1)
                        amount = (WINDOW - DEC_SEQ - b * DEC_SEQ - first_lane) & (LANES - 1)
                        placed = pltpu.roll(new_t, amount, 1)
                        n_ref[b] = jnp.where(is_new, placed, shifted).reshape(N_KV, HEAD_DIM, WINDOW)
            rows(xm_ref, om_ref)

    return body


def _cast_specs(casts, first_main_step=1):
    in_specs, out_specs, out_shape, args = [], [], [], []
    for arr, lead, steps, src_block in casts:
        n_rows, n_cols = arr.shape[-2:]
        rows = n_rows // steps
        assert rows * steps == n_rows and rows % BF16_ROWS == 0
        blk = lambda i, steps=steps: jnp.clip(i - first_main_step, 0, steps - 1)
        src = blk if src_block is None else (lambda i, blk=blk, f=src_block: f(blk(i)))
        in_specs.append(pl.BlockSpec((None,) * len(lead) + (rows, n_cols),
                                     lambda i, lead=lead, src=src: lead + (src(i), 0)))
        out_specs.append(pl.BlockSpec((rows, n_cols), lambda i, blk=blk: (blk(i), 0)))
        out_shape.append(jax.ShapeDtypeStruct((n_rows, n_cols), BF16))
        args.append(arr)
    return in_specs, out_specs, out_shape, args


def _ffn_call(xm, xt, g, w_in, w_out, *, name, final_g=None, kv=None, casts=(), tm=TM):
    n_tiles = N_MAIN // tm
    assert all(c[2] <= n_tiles for c in casts)
    main = pl.BlockSpec((tm, D_MODEL), lambda i: (_main_tile(i), 0))
    in_specs = [main, _resident((N_TAIL, D_MODEL)), _resident((1, D_MODEL)),
                _resident((D_MODEL, 2 * D_FF)), _resident((D_FF, D_MODEL))]
    args = [xm, xt, g.reshape(1, D_MODEL), w_in, w_out]
    out_shape = [jax.ShapeDtypeStruct((N_MAIN, D_MODEL), F32),
                 jax.ShapeDtypeStruct((N_TAIL, D_MODEL), F32)]
    out_specs = [main, pl.BlockSpec((N_TAIL, D_MODEL), lambda i: (0, 0))]
    if final_g is not None:
        in_specs.append(_resident((1, D_MODEL)))
        args.append(final_g.reshape(1, D_MODEL))
    scratch = [pltpu.VMEM((max(N_TAIL, tm), D_FF), BF16)]
    if kv is not None:
        g_kv, w_kv, cos_m, sin_m, cos_t, sin_t, ck, cv = kv
        cached = pl.BlockSpec((DEC_BATCH // n_tiles, N_KV, HEAD_DIM, WINDOW),
                              lambda i: (_main_tile(i), 0, 0, 0))
        pos = pl.BlockSpec((tm, LANES), lambda i: (_main_tile(i) % (SEQ // tm), 0))
        kv_main = pl.BlockSpec((tm, KV_W), lambda i: (_main_tile(i), 0))
        in_specs += [_resident((1, D_MODEL)), _resident((D_MODEL, 2 * KV_W)), pos, pos,
                     _resident((N_TAIL, LANES)), _resident((N_TAIL, LANES)), cached, cached]
        args += [g_kv.reshape(1, D_MODEL), w_kv, cos_m, sin_m, cos_t, sin_t, ck, cv]
        out_shape += [jax.ShapeDtypeStruct((N_MAIN, KV_W), F32)] * 2
        out_shape += [jax.ShapeDtypeStruct((N_TAIL, KV_W), F32)] * 2
        out_shape += [jax.ShapeDtypeStruct((BATCH, KV_W, WINDOW), F32)] * 2
        out_shape += [jax.ShapeDtypeStruct((DEC_BATCH, N_KV, HEAD_DIM, WINDOW), F32)] * 2
        last = pl.BlockSpec((None, KV_W, WINDOW), lambda i: (_main_tile(i) // (SEQ // tm), 0, 0))
        out_specs += [kv_main, kv_main,
                      pl.BlockSpec((N_TAIL, KV_W), lambda i: (0, 0)),
                      pl.BlockSpec((N_TAIL, KV_W), lambda i: (0, 0)), last, last, cached, cached]
        scratch += [pltpu.VMEM((N_SAMPLE // LANES, KV_W, LANES), F32)] * 2
    c_in, c_out, c_shape, c_args = _cast_specs(casts)
    return pl.pallas_call(
        _make_ffn_kernel(final_g is not None, kv is not None, len(casts), tm),
        grid=(1 + n_tiles,),
        in_specs=in_specs + c_in,
        out_specs=out_specs + c_out,
        out_shape=out_shape + c_shape,
        scratch_shapes=scratch,
        compiler_params=_params(),
        name=name,
    )(*args, *c_args)


CARRY = 8
CONV_TM = 1024


def _conv_rows(x_ref, o_ref, g, w_in_ref, kern_ref, w_out_ref, ubuf_ref, mbuf_ref, fix):
    x = x_ref[...]
    m = x.shape[0]
    hn = _rms(x, g).astype(BF16)
    for c in range(D_MODEL // OUT_CHUNK):
        cols = slice(c * OUT_CHUNK, (c + 1) * OUT_CHUNK)
        b, cc, z = (_dot(hn, w_in_ref[:, j * D_MODEL + c * OUT_CHUNK:j * D_MODEL + (c + 1) * OUT_CHUNK])
                    for j in range(3))
        u = cc * z
        ubuf_ref[CARRY:CARRY + m, cols] = u
        prev1 = ubuf_ref[CARRY - 1:CARRY - 1 + m, cols]
        prev2 = ubuf_ref[CARRY - 2:CARRY - 2 + m, cols]
        if fix is not None:
            prev1, prev2 = fix(prev1, prev2, cols)
        conv = kern_ref[0:1, cols] * prev2 + kern_ref[1:2, cols] * prev1 + kern_ref[2:3, cols] * u
        mbuf_ref[0:m, cols] = (b * conv).astype(BF16)
    for c in range(D_MODEL // OUT_CHUNK):
        cols = slice(c * OUT_CHUNK, (c + 1) * OUT_CHUNK)
        o_ref[:, cols] = x[:, cols] + _dot(mbuf_ref[0:m, :], w_out_ref[:, cols])


def _conv_kernel(xm_ref, xt_ref, g_ref, w_in_ref, kern_ref, w_out_ref, s1_ref, s2_ref,
                 om_ref, ot_ref, ut_ref, sp_ref, ubuf_ref, meta_ref, mbuf_ref):
    i = pl.program_id(0)

    @pl.when(i == 0)
    def _():
        ubuf_ref[0:CARRY, :] = jnp.zeros((CARRY, D_MODEL), F32)

        def fix(prev1, prev2, cols):
            step = lax.broadcasted_iota(jnp.int32, (N_SAMPLE, 1), 0) & (DEC_SEQ - 1)
            meta = lax.broadcasted_iota(jnp.int32, (N_META, 1), 0)
            p1 = [jnp.where(step >= 1, prev1[:N_SAMPLE], s1_ref[:, cols]),
                  jnp.where(meta >= 1, prev1[N_SAMPLE:], 0.0)]
            p2 = [jnp.where(step >= 2, prev2[:N_SAMPLE], s2_ref[:, cols]),
                  jnp.where(meta >= 2, prev2[N_SAMPLE:], 0.0)]
            return jnp.concatenate(p1, axis=0), jnp.concatenate(p2, axis=0)

        _conv_rows(xt_ref, ot_ref, g_ref[...], w_in_ref, kern_ref, w_out_ref, ubuf_ref, mbuf_ref, fix)
        ut_ref[...] = ubuf_ref[CARRY:CARRY + N_TAIL, :]
        meta_ref[...] = ubuf_ref[N_TAIL:N_TAIL + CARRY, :]

    @pl.when(i > 0)
    def _():
        t = i - 1
        per_seq = SEQ // CONV_TM

        @pl.when(t % per_seq == 0)
        def _():
            ubuf_ref[0:CARRY, :] = meta_ref[...]

        _conv_rows(xm_ref, om_ref, g_ref[...], w_in_ref, kern_ref, w_out_ref, ubuf_ref, mbuf_ref, None)
        last = ubuf_ref[CONV_TM:CONV_TM + CARRY, :]
        ubuf_ref[0:CARRY, :] = last

        @pl.when(t % per_seq == per_seq - 1)
        def _():
            sp_ref[t // per_seq] = last


def _conv_call(xm, xt, g, w_in, kern, w_out, s1, s2):
    main = pl.BlockSpec((CONV_TM, D_MODEL), lambda i: (_main_tile(i), 0))
    return pl.pallas_call(
        _conv_kernel,
        grid=(1 + N_MAIN // CONV_TM,),
        in_specs=[main, _resident((N_TAIL, D_MODEL)), _resident((1, D_MODEL)),
                  _resident((D_MODEL, 3 * D_MODEL)), _resident((CONV_W, D_MODEL)),
                  _resident((D_MODEL, D_MODEL)),
                  _resident((N_SAMPLE, D_MODEL)), _resident((N_SAMPLE, D_MODEL))],
        out_specs=[main,
                   pl.BlockSpec((N_TAIL, D_MODEL), lambda i: (0, 0)),
                   pl.BlockSpec((N_TAIL, D_MODEL), lambda i: (0, 0)),
                   pl.BlockSpec((BATCH, CARRY, D_MODEL), lambda i: (0, 0, 0))],
        out_shape=[jax.ShapeDtypeStruct((N_MAIN, D_MODEL), F32),
                   jax.ShapeDtypeStruct((N_TAIL, D_MODEL), F32),
                   jax.ShapeDtypeStruct((N_TAIL, D_MODEL), F32),
                   jax.ShapeDtypeStruct((BATCH, CARRY, D_MODEL), F32)],
        scratch_shapes=[pltpu.VMEM((CARRY + max(N_TAIL, CONV_TM), D_MODEL), F32),
                        pltpu.VMEM((CARRY, D_MODEL), F32),
                        pltpu.VMEM((max(N_TAIL, CONV_TM), D_MODEL), BF16)],
        compiler_params=_params(),
        name="conv_mixer",
    )(xm, xt, g.reshape(1, D_MODEL), w_in, kern, w_out, s1, s2)


def _attend(qp, segs, sinks_ref, heads_per_pass=1):
    mq = qp.shape[0]
    rows = GROUP * mq
    lane = lax.broadcasted_iota(jnp.int32, (1, KV_W), 1)
    qk_head = (lane & (LANES - 1)) // HALF_DIM
    pair_head = lax.broadcasted_iota(jnp.int32, (1, LANES), 1) // HEAD_DIM
    row_head = lax.broadcasted_iota(jnp.int32, (heads_per_pass * rows, 1), 0) // mq
    masks = [jnp.concatenate([seg[2]] * (heads_per_pass * GROUP), axis=0) for seg in segs]
    outs = []
    for k0 in range(0, N_KV, heads_per_pass):
        heads = range(k0, k0 + heads_per_pass)
        qs = jnp.concatenate(
            [jnp.where(qk_head == k, qp[:, g * KV_W:(g + 1) * KV_W], jnp.zeros((), BF16))
             for k in heads for g in range(GROUP)], axis=0)
        sink = jnp.full((heads_per_pass * rows, 1), sinks_ref[k0 * GROUP] * LOG2E, F32)
        for j in range(1, heads_per_pass * GROUP):
            sink = jnp.where(row_head == j, sinks_ref[k0 * GROUP + j] * LOG2E, sink)
        scores = []
        for (kb, _, _, k_transposed), msk in zip(segs, masks):
            s = _dot(qs, kb) if k_transposed else _dot_nt(qs, kb)
            scores.append(jnp.where(msk, s, NEG))
        mx = sink
        for s in scores:
            mx = jnp.maximum(mx, jnp.max(s, axis=-1, keepdims=True))
        denom = jnp.exp2(sink - mx)
        probs = []
        for s in scores:
            p = jnp.exp2(s - mx)
            denom = denom + jnp.sum(p, axis=-1, keepdims=True)
            probs.append(p.astype(BF16))
        inv = 1.0 / denom
        for n, k in enumerate(heads):
            pair = slice((k // 2) * LANES, (k // 2 + 1) * LANES)
            part = slice(n * rows, (n + 1) * rows)
            acc = None
            for p, (_, vb, _, _) in zip(probs, segs):
                pv = _dot(p[part], jnp.where(pair_head == k % 2, vb[:, pair], jnp.zeros((), BF16)))
                acc = pv if acc is None else acc + pv
            outs.append(acc * inv[part])
    out = jnp.concatenate([outs[k] + outs[k + 1] for k in range(0, N_KV, 2)], axis=1)
    return jnp.concatenate([out[g * mq:(g + 1) * mq, :] for g in range(GROUP)], axis=1)


def _queries(x, g, wq_ref, cos, sin):
    q = _dot(_rms(x, g).astype(BF16), wq_ref[...])
    return (_rope(q, cos, sin) * (HEAD_DIM ** -0.5 * LOG2E)).astype(BF16)


def _attn_main_kernel(sinks_ref, xm_ref, g_ref, wq_ref, wo_ref, cos_ref, sin_ref,
                      km_ref, vm_ref, kt_ref, vt_ref,
                      om_ref, q_ref, ao_ref, kbuf_ref, vbuf_ref):
    i = pl.program_id(0)
    start = i % (SEQ // ATT_TM) == 0

    x = xm_ref[...]
    q_ref[...] = _queries(x, g_ref[...], wq_ref, cos_ref[...], sin_ref[...])

    @pl.when(start)
    def _():
        pad = jnp.zeros((BLK - N_META, KV_W), BF16)
        kbuf_ref[0:BLK - N_META, :] = pad
        vbuf_ref[0:BLK - N_META, :] = pad
        kbuf_ref[BLK - N_META:BLK, :] = kt_ref[N_SAMPLE:N_TAIL, :].astype(BF16)
        vbuf_ref[BLK - N_META:BLK, :] = vt_ref[N_SAMPLE:N_TAIL, :].astype(BF16)

    @pl.when(jnp.logical_not(start))
    def _():
        kbuf_ref[0:BLK, :] = kbuf_ref[ATT_TM:ATT_TM + BLK, :]
        vbuf_ref[0:BLK, :] = vbuf_ref[ATT_TM:ATT_TM + BLK, :]

    kbuf_ref[BLK:BLK + ATT_TM, :] = km_ref[...].astype(BF16)
    vbuf_ref[BLK:BLK + ATT_TM, :] = vm_ref[...].astype(BF16)

    qi = lax.broadcasted_iota(jnp.int32, (BLK, 2 * BLK), 0)
    kj = lax.broadcasted_iota(jnp.int32, (BLK, 2 * BLK), 1)
    band = (kj >= qi) & (kj <= qi + WINDOW)
    first_key = jnp.where(start, BLK - N_META, 0)
    for blk in range(ATT_TM // BLK):
        lo = blk * BLK
        mask = band & (kj >= first_key) if blk == 0 else band
        att = _attend(q_ref[lo:lo + BLK, :],
                      [(kbuf_ref[lo:lo + 2 * BLK, :], vbuf_ref[lo:lo + 2 * BLK, :], mask, False)],
                      sinks_ref)
        ao_ref[lo:lo + BLK, :] = att.astype(BF16)
    om_ref[...] = x + _dot(ao_ref[...], wo_ref[...])


def _attn_main_call(sinks, xm, g, wq, wo, cos_m, sin_m, km, vm, kt, vt):
    tile = lambda w: pl.BlockSpec((ATT_TM, w), lambda i: (i, 0))
    pos = pl.BlockSpec((ATT_TM, LANES), lambda i: (i % (SEQ // ATT_TM), 0))
    return pl.pallas_call(
        _attn_main_kernel,
        grid=(N_MAIN // ATT_TM,),
        in_specs=[pl.BlockSpec(memory_space=pltpu.SMEM),
                  tile(D_MODEL), _resident((1, D_MODEL)),
                  _resident((D_MODEL, D_MODEL)), _resident((D_MODEL, D_MODEL)),
                  pos, pos, tile(KV_W), tile(KV_W),
                  _resident((N_TAIL, KV_W)), _resident((N_TAIL, KV_W))],
        out_specs=tile(D_MODEL),
        out_shape=jax.ShapeDtypeStruct((N_MAIN, D_MODEL), F32),
        scratch_shapes=[pltpu.VMEM((ATT_TM, D_MODEL), BF16), pltpu.VMEM((ATT_TM, D_MODEL), BF16),
                        pltpu.VMEM((BLK + ATT_TM, KV_W), BF16),
                        pltpu.VMEM((BLK + ATT_TM, KV_W), BF16)],
        compiler_params=_params(),
        name="attn_main",
    )(sinks, xm, g.reshape(1, D_MODEL), wq, wo, cos_m, sin_m, km, vm, kt, vt)


def _attn_tail_kernel(sinks_ref, xt_ref, g_ref, wq_ref, wo_ref, cos_ref, sin_ref,
                      kt_ref, vt_ref, ck_ref, cv_ref, ot_ref, q_ref, ao_ref):
    s = pl.program_id(0)

    @pl.when(s == 0)
    def _():
        q_ref[...] = _queries(xt_ref[...], g_ref[...], wq_ref, cos_ref[...], sin_ref[...])
        qi = lax.broadcasted_iota(jnp.int32, (N_META, N_META), 0)
        kj = lax.broadcasted_iota(jnp.int32, (N_META, N_META), 1)
        km = kt_ref[N_SAMPLE:N_TAIL, :].astype(BF16)
        vm = vt_ref[N_SAMPLE:N_TAIL, :].astype(BF16)
        att = _attend(q_ref[N_SAMPLE:N_TAIL, :], [(km, vm, kj <= qi, False)], sinks_ref, N_KV)
        ao_ref[N_SAMPLE:N_TAIL, :] = att.astype(BF16)

    rows = pl.ds(pl.multiple_of(s * GROUP_ROWS, GROUP_ROWS), GROUP_ROWS)
    k_new = kt_ref[rows, :]
    v_new = vt_ref[rows, :]
    ck = [ck_ref[b].reshape(KV_W, WINDOW) for b in range(SAMPLE_GROUP)]
    cv = [cv_ref[b].reshape(KV_W, WINDOW) for b in range(SAMPLE_GROUP)]
    k_cached_t = jnp.concatenate([_rows_heads_to_split(c) for c in ck], axis=1).astype(BF16)
    v_cached = jnp.concatenate([c.T for c in cv], axis=0).astype(BF16)

    n_cached = SAMPLE_GROUP * WINDOW
    qr = lax.broadcasted_iota(jnp.int32, (GROUP_ROWS, n_cached), 0)
    kc = lax.broadcasted_iota(jnp.int32, (GROUP_ROWS, n_cached), 1)
    mask_cached = ((kc // WINDOW) == (qr // DEC_SEQ)) & ((kc % WINDOW) >= (qr % DEC_SEQ))
    qn = lax.broadcasted_iota(jnp.int32, (GROUP_ROWS, GROUP_ROWS), 0)
    kn = lax.broadcasted_iota(jnp.int32, (GROUP_ROWS, GROUP_ROWS), 1)
    mask_new = ((kn // DEC_SEQ) == (qn // DEC_SEQ)) & ((kn % DEC_SEQ) <= (qn % DEC_SEQ))
    att = _attend(q_ref[rows, :],
                  [(k_cached_t, v_cached, mask_cached, True),
                   (k_new.astype(BF16), v_new.astype(BF16), mask_new, False)],
                  sinks_ref, N_KV)
    ao_ref[rows, :] = att.astype(BF16)

    @pl.when(s == N_GROUPS - 1)
    def _():
        ot_ref[...] = xt_ref[...] + _dot(ao_ref[...], wo_ref[...])


def _attn_tail_call(sinks, xt, g, wq, wo, cos_t, sin_t, kt, vt, ck, cv):
    cached = pl.BlockSpec((SAMPLE_GROUP, N_KV, HEAD_DIM, WINDOW), lambda s: (s, 0, 0, 0))
    return pl.pallas_call(
        _attn_tail_kernel,
        grid=(N_GROUPS,),
        in_specs=[pl.BlockSpec(memory_space=pltpu.SMEM),
                  _resident((N_TAIL, D_MODEL)), _resident((1, D_MODEL)),
                  _resident((D_MODEL, D_MODEL)), _resident((D_MODEL, D_MODEL)),
                  _resident((N_TAIL, LANES)), _resident((N_TAIL, LANES)),
                  _resident((N_TAIL, KV_W)), _resident((N_TAIL, KV_W)),
                  cached, cached],
        out_specs=pl.BlockSpec((N_TAIL, D_MODEL), lambda s: (0, 0)),
        out_shape=jax.ShapeDtypeStruct((N_TAIL, D_MODEL), F32),
        scratch_shapes=[pltpu.VMEM((N_TAIL, D_MODEL), BF16), pltpu.VMEM((N_TAIL, D_MODEL), BF16)],
        compiler_params=_params(),
        name="attn_tail",
    )(sinks, xt, g.reshape(1, D_MODEL), wq, wo, cos_t, sin_t, kt, vt, ck, cv)


def _rope_tables(pos):
    inv = 1.0 / (ROPE_THETA ** (jnp.arange(0, HEAD_DIM, 2, dtype=F32) / HEAD_DIM))
    ang = pos.astype(F32)[:, None] * inv[None, :]
    cos, sin = jnp.cos(ang), jnp.sin(ang)
    return jnp.tile(cos, (1, N_KV)), jnp.tile(sin, (1, N_KV))


def kernel(x_prompt, x_sample, state_conv, cache_k, cache_v, meta_tokens, norm_g, ffn_w_in,
           ffn_w_out, conv_w_in, conv_kernel, conv_w_out, kv_norm_g, w_kv, w_q, w_o, sinks,
           final_norm_g):
    assert x_prompt.shape == (BATCH, SEQ, D_MODEL) and x_sample.shape == (DEC_BATCH, DEC_SEQ, D_MODEL)
    assert cache_k.shape == (DEC_BATCH, WINDOW, N_KV, HEAD_DIM)
    assert norm_g.shape[0] == 2 and conv_w_in.shape[0] == 1 and w_q.shape[0] == 1

    xm = x_prompt.reshape(N_MAIN, D_MODEL)
    xt = jnp.concatenate([x_sample.reshape(N_SAMPLE, D_MODEL), meta_tokens], axis=0)

    wq = w_q[0].reshape(D_MODEL, N_KV, GROUP, 2, HALF_DIM).transpose(0, 2, 3, 1, 4)
    wq = wq.reshape(D_MODEL, D_MODEL).astype(BF16)
    wk =w_kv[:, :KV_W].reshape(D_MODEL, N_KV, 2, HALF_DIM).transpose(0, 2, 1, 3)
    wkv = jnp.concatenate([wk.reshape(D_MODEL, KV_W), w_kv[:, KV_W:]], axis=1).astype(BF16)

    pos_main = N_META + jnp.arange(SEQ, dtype=jnp.int32)
    pos_tail = jnp.concatenate([
        jnp.tile(PAST_LEN + jnp.arange(DEC_SEQ, dtype=jnp.int32), DEC_BATCH),
        jnp.arange(N_META, dtype=jnp.int32)])
    cos_m, sin_m = _rope_tables(pos_main)
    cos_t, sin_t = _rope_tables(pos_tail)

    sc = state_conv[0]
    grow = lambda a: jnp.pad(a, ((0, 0), (0, DEC_SEQ - a.shape[1]), (0, 0))).reshape(N_SAMPLE, D_MODEL)
    s1 = grow(sc[:, 1:])
    s2 = grow(sc)

    ffn_casts = lambda l, j, steps=N_TILES: [(ffn_w_in, (l, j), steps, None),
                                             (ffn_w_out, (l, j), steps // 2, None)]
    wo_cast = (w_o, (0,), N_HEADS, lambda b: (b % N_KV) * GROUP + b // N_KV)
    slot_minor = lambda a: a.transpose(0, 2, 3, 1)
    ck, cv = slot_minor(cache_k), slot_minor(cache_v)

    xm, xt, cw_in, cw_out, w_in, w_out, w_in_10, w_out_10 = _ffn_first_call(
        xm, xt, norm_g[0, 0], ffn_w_in, ffn_w_out, (0, 0),
        [(conv_w_in, (0,), N_TILES, None), (conv_w_out, (0,), N_TILES, None)]
        + ffn_casts(0, 1) + ffn_casts(1, 0), name="ffn_l0a")
    xm, xt, u_tail, sp = _conv_call(xm, xt, norm_g[0, 1], cw_in, conv_kernel[0], cw_out, s1, s2)
    xm, xt, wo, w_in_11, w_out_11 = _ffn_call(
        xm, xt, norm_g[0, 2], w_in, w_out, name="ffn_l0b", tm=WIDE_TM,
        casts=[wo_cast] + ffn_casts(1, 1, N_MAIN // WIDE_TM))

    xm, xt, km, vm, kt, vt, k_last, v_last, nk, nv = _ffn_call(
        xm, xt, norm_g[1, 0], w_in_10, w_out_10, name="ffn_l1a_kv",
        kv=(kv_norm_g, wkv, cos_m, sin_m, cos_t, sin_t, ck, cv))
    w_in, w_out = w_in_11, w_out_11
    xt = _attn_tail_call(sinks[0], xt, norm_g[1, 1], wq, wo, cos_t, sin_t, kt, vt, ck, cv)
    xm = _attn_main_call(sinks[0], xm, norm_g[1, 1], wq, wo, cos_m, sin_m, km, vm, kt, vt)
    ym, yt = _ffn_call(xm, xt, norm_g[1, 2], w_in, w_out, name="ffn_l1b_final",
                       final_g=final_norm_g, tm=WIDE_TM)

    y_prompt = ym.reshape(BATCH, SEQ, D_MODEL)
    y_sample = yt[:N_SAMPLE].reshape(DEC_BATCH, DEC_SEQ, D_MODEL)
    new_state_conv_p = sp[:, CARRY - (CONV_W - 1):][None]
    new_state_conv_s = u_tail[:N_SAMPLE].reshape(DEC_BATCH, DEC_SEQ, D_MODEL)[:, DEC_SEQ - (CONV_W - 1):][None]
    slot_major = lambda a: a.transpose(0, 3, 1, 2)
    new_k_p = slot_major(k_last.reshape(BATCH, N_KV, HEAD_DIM, WINDOW))
    new_v_p = slot_major(v_last.reshape(BATCH, N_KV, HEAD_DIM, WINDOW))
    return (y_prompt, y_sample, new_state_conv_p, new_state_conv_s,
            new_k_p, new_v_p, slot_major(nk), slot_major(nv))
```

```python
import jax
import jax.numpy as jnp
from jax import lax
from jax.experimental import pallas as pl
from jax.experimental.pallas import tpu as pltpu

D_MODEL = 1024
BATCH = 2
SEQ = 8192
DEC_BATCH = 128
DEC_SEQ = 4
PAST_LEN = 8192
N_META = 16
D_FF = 2816
CONV_W = 3
HEAD_DIM = 64
N_HEADS = 16
N_KV = 4
GROUP = 4
WINDOW = 128
ROPE_THETA = 10000.0
EPS = 1e-6
NEG = -1e30
LOG2E = 1.4426950408889634

KV_W = N_KV * HEAD_DIM
HALF_DIM = HEAD_DIM // 2
N_MAIN = BATCH * SEQ
N_SAMPLE = DEC_BATCH * DEC_SEQ
N_TAIL = N_SAMPLE + N_META
TM = 512
N_TILES = N_MAIN // TM
TILES_PER_SEQ = SEQ // TM
WIDE_TM = 1024
FF_CHUNK = 256
OUT_CHUNK = 256
BLK = WINDOW
ATT_TM = 1024
SAMPLE_GROUP = 8
N_GROUPS = DEC_BATCH // SAMPLE_GROUP
GROUPS_PER_STEP = 2
GROUP_ROWS = SAMPLE_GROUP * DEC_SEQ
LANES = 128
BF16_ROWS = 16
V7X_VMEM_LIMIT = 56 * 1024 * 1024

F32 = jnp.float32
BF16 = jnp.bfloat16


def _rms(x, g):
    return x * lax.rsqrt(jnp.mean(x * x, axis=-1, keepdims=True) + EPS) * g


def _rope(x, cos, sin):
    outs = []
    for c in range(x.shape[1] // KV_W):
        x1 = x[:, c * KV_W:c * KV_W + LANES]
        x2 = x[:, c * KV_W + LANES:(c + 1) * KV_W]
        outs += [x1 * cos - x2 * sin, x2 * cos + x1 * sin]
    return jnp.concatenate(outs, axis=1)


def _rows_split_to_heads(t):
    return jnp.concatenate([t[half * LANES + h * HALF_DIM:half * LANES + (h + 1) * HALF_DIM]
                            for h in range(N_KV) for half in range(2)], axis=0)


def _rows_heads_to_split(t):
    return jnp.concatenate([t[h * HEAD_DIM + half * HALF_DIM:h * HEAD_DIM + (half + 1) * HALF_DIM]
                            for half in range(2) for h in range(N_KV)], axis=0)


def _dot(a, b):
    return jnp.dot(a, b, preferred_element_type=F32)


def _dot_nt(a, b):
    return lax.dot_general(a, b, (((1,), (1,)), ((), ())), preferred_element_type=F32)


def _resident(shape):
    nd = len(shape)
    return pl.BlockSpec(shape, lambda i: (0,) * nd, pipeline_mode=pl.Buffered(1))


def _main_tile(i):
    return jnp.maximum(i - 1, 0)


def _params():
    return pltpu.CompilerParams(dimension_semantics=("arbitrary",),
                                vmem_limit_bytes=V7X_VMEM_LIMIT)


def _ffn_rows(x_ref, o_ref, g, w_in_ref, w_out_ref, a_ref, final_g=None):
    x = x_ref[...]
    m = x.shape[0]
    hn = _rms(x, g).astype(BF16)
    for c in range(D_FF // FF_CHUNK):
        lo = c * FF_CHUNK
        a_ref[0:m, lo:lo + FF_CHUNK] = _swiglu_chunk(
            hn, w_in_ref[:, lo:lo + FF_CHUNK], w_in_ref[:, D_FF + lo:D_FF + lo + FF_CHUNK])
    _ffn_out(x, o_ref, a_ref, w_out_ref, final_g)


def _swiglu_chunk(hn, w_gate, w_up):
    gate = _dot(hn, w_gate)
    return (gate * jax.nn.sigmoid(gate) * _dot(hn, w_up)).astype(BF16)


def _ffn_out(x, o_ref, a_ref, w_out_ref, final_g):
    m = x.shape[0]
    sq = None
    for c in range(D_MODEL // OUT_CHUNK):
        cols = slice(c * OUT_CHUNK, (c + 1) * OUT_CHUNK)
        y = x[:, cols] + 0.5 * _dot(a_ref[0:m, :], w_out_ref[:, cols])
        o_ref[:, cols] = y
        if final_g is not None:
            part = jnp.sum(y * y, axis=-1, keepdims=True)
            sq = part if sq is None else sq + part
    if final_g is not None:
        o_ref[...] = o_ref[...] * lax.rsqrt(sq * (1.0 / D_MODEL) + EPS) * final_g


N_CHUNKS = D_FF // FF_CHUNK


def _make_ffn_first_kernel(n_cast):
    def body(*refs):
        xm_ref, xt_ref, g_ref, wg_ref, wu_ref, wo_ref = refs[:6]
        cast_src = refs[6:6 + n_cast]
        om_ref, ot_ref = refs[6 + n_cast:8 + n_cast]
        cast_dst = refs[8 + n_cast:8 + 2 * n_cast]
        a_ref, w_in_ref, w_out_ref, hn_ref = refs[8 + 2 * n_cast:]
        i = pl.program_id(0)

        for src, dst in zip(cast_src, cast_dst):
            dst[...] = src[...].astype(BF16)

        @pl.when(i == 0)
        def _():
            hn_ref[...] = _rms(xt_ref[...], g_ref[...]).astype(BF16)

        for c in range(N_CHUNKS):
            @pl.when(i == c)
            def _(c=c):
                lo = c * FF_CHUNK
                w_gate = wg_ref[...].astype(BF16)
                w_up = wu_ref[...].astype(BF16)
                w_in_ref[:, lo:lo + FF_CHUNK] = w_gate
                w_in_ref[:, D_FF + lo:D_FF + lo + FF_CHUNK] = w_up
                w_out_ref[lo:lo + FF_CHUNK, :] = wo_ref[...].astype(BF16)
                a_ref[0:N_TAIL, lo:lo + FF_CHUNK] = _swiglu_chunk(hn_ref[...], w_gate, w_up)

        @pl.when(i == N_CHUNKS - 1)
        def _():
            _ffn_out(xt_ref[...], ot_ref, a_ref, w_out_ref, None)

        @pl.when(i >= N_CHUNKS)
        def _():
            _ffn_rows(xm_ref, om_ref, g_ref[...], w_in_ref, w_out_ref, a_ref)

    return body


def _ffn_first_call(xm, xt, g, ffn_w_in, ffn_w_out, lead, casts, *, name):
    chunk = lambda i: jnp.minimum(i, N_CHUNKS - 1)
    nl = (None,) * len(lead)
    main = pl.BlockSpec((TM, D_MODEL), lambda i: (jnp.maximum(i - N_CHUNKS, 0), 0))
    c_in, c_out, c_shape, c_args = _cast_specs(casts, N_CHUNKS)
    return pl.pallas_call(
        _make_ffn_first_kernel(len(casts)),
        grid=(N_CHUNKS + N_TILES,),
        in_specs=[main, _resident((N_TAIL, D_MODEL)), _resident((1, D_MODEL)),
                  pl.BlockSpec(nl + (D_MODEL, FF_CHUNK), lambda i: lead + (0, chunk(i))),
                  pl.BlockSpec(nl + (D_MODEL, FF_CHUNK), lambda i: lead + (0, N_CHUNKS + chunk(i))),
                  pl.BlockSpec(nl + (FF_CHUNK, D_MODEL), lambda i: lead + (chunk(i), 0))] + c_in,
        out_specs=[main, pl.BlockSpec((N_TAIL, D_MODEL), lambda i: (0, 0))] + c_out,
        out_shape=[jax.ShapeDtypeStruct((N_MAIN, D_MODEL), F32),
                   jax.ShapeDtypeStruct((N_TAIL, D_MODEL), F32)] + c_shape,
        scratch_shapes=[pltpu.VMEM((N_TAIL, D_FF), BF16),
                        pltpu.VMEM((D_MODEL, 2 * D_FF), BF16), pltpu.VMEM((D_FF, D_MODEL), BF16),
                        pltpu.VMEM((N_TAIL, D_MODEL), BF16)],
        compiler_params=_params(),
        name=name,
    )(xm, xt, g.reshape(1, D_MODEL), ffn_w_in, ffn_w_in, ffn_w_out, *c_args)


def _make_ffn_kernel(has_final, has_kv, n_cast, tm):
    cache_seqs = DEC_BATCH * tm // N_MAIN

    def body(*refs):
        refs = list(refs)
        xm_ref, xt_ref, g_ref, w_in_ref, w_out_ref = refs[:5]
        pos = 5
        if has_final:
            gf_ref = refs[pos]
            pos += 1
        if has_kv:
            gkv_ref, wkv_ref, cm_ref, sm_ref, ct_ref, st_ref, ck_ref, cv_ref = refs[pos:pos + 8]
            pos += 8
        cast_src = refs[pos:pos + n_cast]
        pos += n_cast
        om_ref, ot_ref = refs[pos:pos + 2]
        pos += 2
        if has_kv:
            km_ref, vm_ref, kt_ref, vt_ref, kl_ref, vl_ref, nk_ref, nv_ref = refs[pos:pos + 8]
            pos += 8
        cast_dst = refs[pos:pos + n_cast]
        pos += n_cast
        a_ref = refs[pos]
        if has_kv:
            newk_ref, newv_ref = refs[pos + 1:pos + 3]
        i = pl.program_id(0)

        for src, dst in zip(cast_src, cast_dst):
            dst[...] = src[...].astype(BF16)

        def rows(x_ref, o_ref):
            _ffn_rows(x_ref, o_ref, g_ref[...], w_in_ref, w_out_ref, a_ref,
                      gf_ref[...] if has_final else None)

        def project_kv(x_ref, cos_ref, sin_ref, k_ref, v_ref):
            kv = _dot(_rms(x_ref[...], gkv_ref[...]).astype(BF16), wkv_ref[...])
            k = _rope(kv[:, :KV_W], cos_ref[...], sin_ref[...])
            v = kv[:, KV_W:]
            k_ref[...] = k
            v_ref[...] = v
            return k, v

        @pl.when(i == 0)
        def _():
            if has_kv:
                k, v = project_kv(xt_ref, ct_ref, st_ref, kt_ref, vt_ref)
                for c in range(N_SAMPLE // LANES):
                    newk_ref[c] = _rows_split_to_heads(k[c * LANES:(c + 1) * LANES, :].T)
                    newv_ref[c] = v[c * LANES:(c + 1) * LANES, :].T
            rows(xt_ref, ot_ref)

        @pl.when(i > 0)
        def _():
            if has_kv:
                t = i - 1
                k, v = project_kv(xm_ref, cm_ref, sm_ref, km_ref, vm_ref)
                kl_ref[...] = _rows_split_to_heads(k[tm - WINDOW:, :].T)
                vl_ref[...] = v[tm - WINDOW:, :].T

                lane = lax.broadcasted_iota(jnp.int32, (1, WINDOW), 1)
                is_new = lane >= WINDOW - DEC_SEQ
                first_lane = (t * cache_seqs * DEC_SEQ) % LANES
                for new_ref, c_ref, n_ref in ((newk_ref, ck_ref, nk_ref), (newv_ref, cv_ref, nv_ref)):
                    new_t = new_ref[(t * cache_seqs * DEC_SEQ) // LANES]
                    for b in range(cache_seqs):
                        shifted = pltpu.roll(c_ref[b].reshape(KV_W, WINDOW), WINDOW - DEC_SEQ, 1)
                        amount = (WINDOW - DEC_SEQ - b * DEC_SEQ - first_lane) & (LANES - 1)
                        placed = pltpu.roll(new_t, amount, 1)
                        n_ref[b] = jnp.where(is_new, placed, shifted).reshape(N_KV, HEAD_DIM, WINDOW)
            rows(xm_ref, om_ref)

    return body


def _cast_specs(casts, first_main_step=1):
    in_specs, out_specs, out_shape, args = [], [], [], []
    for arr, lead, steps, src_block in casts:
        n_rows, n_cols = arr.shape[-2:]
        rows = n_rows // steps
        assert rows * steps == n_rows and rows % BF16_ROWS == 0
        blk = lambda i, steps=steps: jnp.clip(i - first_main_step, 0, steps - 1)
        src = blk if src_block is None else (lambda i, blk=blk, f=src_block: f(blk(i)))
        in_specs.append(pl.BlockSpec((None,) * len(lead) + (rows, n_cols),
                                     lambda i, lead=lead, src=src: lead + (src(i), 0)))
        out_specs.append(pl.BlockSpec((rows, n_cols), lambda i, blk=blk: (blk(i), 0)))
        out_shape.append(jax.ShapeDtypeStruct((n_rows, n_cols), BF16))
        args.append(arr)
    return in_specs, out_specs, out_shape, args


def _ffn_call(xm, xt, g, w_in, w_out, *, name, final_g=None, kv=None, casts=(), tm=TM):
    n_tiles = N_MAIN // tm
    assert all(c[2] <= n_tiles for c in casts)
    main = pl.BlockSpec((tm, D_MODEL), lambda i: (_main_tile(i), 0))
    in_specs = [main, _resident((N_TAIL, D_MODEL)), _resident((1, D_MODEL)),
                _resident((D_MODEL, 2 * D_FF)), _resident((D_FF, D_MODEL))]
    args = [xm, xt, g.reshape(1, D_MODEL), w_in, w_out]
    out_shape = [jax.ShapeDtypeStruct((N_MAIN, D_MODEL), F32),
                 jax.ShapeDtypeStruct((N_TAIL, D_MODEL), F32)]
    out_specs = [main, _resident((N_TAIL, D_MODEL))]
    if final_g is not None:
        in_specs.append(_resident((1, D_MODEL)))
        args.append(final_g.reshape(1, D_MODEL))
    scratch = [pltpu.VMEM((max(N_TAIL, tm), D_FF), BF16)]
    if kv is not None:
        g_kv, w_kv, cos_m, sin_m, cos_t, sin_t, ck, cv = kv
        cached = pl.BlockSpec((DEC_BATCH // n_tiles, N_KV, HEAD_DIM, WINDOW),
                              lambda i: (_main_tile(i), 0, 0, 0))
        pos = pl.BlockSpec((tm, LANES), lambda i: (_main_tile(i) % (SEQ // tm), 0))
        kv_main = pl.BlockSpec((tm, KV_W), lambda i: (_main_tile(i), 0))
        in_specs += [_resident((1, D_MODEL)), _resident((D_MODEL, 2 * KV_W)), pos, pos,
                     _resident((N_TAIL, LANES)), _resident((N_TAIL, LANES)), cached, cached]
        args += [g_kv.reshape(1, D_MODEL), w_kv, cos_m, sin_m, cos_t, sin_t, ck, cv]
        out_shape += [jax.ShapeDtypeStruct((N_MAIN, KV_W), F32)] * 2
        out_shape += [jax.ShapeDtypeStruct((N_TAIL, KV_W), F32)] * 2
        out_shape += [jax.ShapeDtypeStruct((BATCH, KV_W, WINDOW), F32)] * 2
        out_shape += [jax.ShapeDtypeStruct((DEC_BATCH, N_KV, HEAD_DIM, WINDOW), F32)] * 2
        last = pl.BlockSpec((None, KV_W, WINDOW), lambda i: (_main_tile(i) // (SEQ // tm), 0, 0))
        out_specs += [kv_main, kv_main, _resident((N_TAIL, KV_W)), _resident((N_TAIL, KV_W)),
                      last, last, cached, cached]
        scratch += [pltpu.VMEM((N_SAMPLE // LANES, KV_W, LANES), F32)] * 2
    c_in, c_out, c_shape, c_args = _cast_specs(casts)
    return pl.pallas_call(
        _make_ffn_kernel(final_g is not None, kv is not None, len(casts), tm),
        grid=(1 + n_tiles,),
        in_specs=in_specs + c_in,
        out_specs=out_specs + c_out,
        out_shape=out_shape + c_shape,
        scratch_shapes=scratch,
        compiler_params=_params(),
        name=name,
    )(*args, *c_args)


CARRY = 8
CONV_TM = 1024


def _conv_rows(x_ref, o_ref, g, w_in_ref, kern_ref, w_out_ref, ubuf_ref, mbuf_ref, fix):
    x = x_ref[...]
    m = x.shape[0]
    hn = _rms(x, g).astype(BF16)
    for c in range(D_MODEL // OUT_CHUNK):
        cols = slice(c * OUT_CHUNK, (c + 1) * OUT_CHUNK)
        b, cc, z = (_dot(hn, w_in_ref[:, j * D_MODEL + c * OUT_CHUNK:j * D_MODEL + (c + 1) * OUT_CHUNK])
                    for j in range(3))
        u = cc * z
        ubuf_ref[CARRY:CARRY + m, cols] = u
        prev1 = ubuf_ref[CARRY - 1:CARRY - 1 + m, cols]
        prev2 = ubuf_ref[CARRY - 2:CARRY - 2 + m, cols]
        if fix is not None:
            prev1, prev2 = fix(prev1, prev2, cols)
        conv = kern_ref[0:1, cols] * prev2 + kern_ref[1:2, cols] * prev1 + kern_ref[2:3, cols] * u
        mbuf_ref[0:m, cols] = (b * conv).astype(BF16)
    for c in range(D_MODEL // OUT_CHUNK):
        cols = slice(c * OUT_CHUNK, (c + 1) * OUT_CHUNK)
        o_ref[:, cols] = x[:, cols] + _dot(mbuf_ref[0:m, :], w_out_ref[:, cols])


def _conv_kernel(xm_ref, xt_ref, g_ref, w_in_ref, kern_ref, w_out_ref, s1_ref, s2_ref,
                 om_ref, ot_ref, ut_ref, sp_ref, ubuf_ref, meta_ref, mbuf_ref):
    i = pl.program_id(0)

    @pl.when(i == 0)
    def _():
        ubuf_ref[0:CARRY, :] = jnp.zeros((CARRY, D_MODEL), F32)

        def fix(prev1, prev2, cols):
            step = lax.broadcasted_iota(jnp.int32, (N_SAMPLE, 1), 0) & (DEC_SEQ - 1)
            meta = lax.broadcasted_iota(jnp.int32, (N_META, 1), 0)
            p1 = [jnp.where(step >= 1, prev1[:N_SAMPLE], s1_ref[:, cols]),
                  jnp.where(meta >= 1, prev1[N_SAMPLE:], 0.0)]
            p2 = [jnp.where(step >= 2, prev2[:N_SAMPLE], s2_ref[:, cols]),
                  jnp.where(meta >= 2, prev2[N_SAMPLE:], 0.0)]
            return jnp.concatenate(p1, axis=0), jnp.concatenate(p2, axis=0)

        _conv_rows(xt_ref, ot_ref, g_ref[...], w_in_ref, kern_ref, w_out_ref, ubuf_ref, mbuf_ref, fix)
        ut_ref[...] = ubuf_ref[CARRY:CARRY + N_TAIL, :]
        meta_ref[...] = ubuf_ref[N_TAIL:N_TAIL + CARRY, :]

    @pl.when(i > 0)
    def _():
        t = i - 1
        per_seq = SEQ // CONV_TM

        @pl.when(t % per_seq == 0)
        def _():
            ubuf_ref[0:CARRY, :] = meta_ref[...]

        _conv_rows(xm_ref, om_ref, g_ref[...], w_in_ref, kern_ref, w_out_ref, ubuf_ref, mbuf_ref, None)
        last = ubuf_ref[CONV_TM:CONV_TM + CARRY, :]
        ubuf_ref[0:CARRY, :] = last

        @pl.when(t % per_seq == per_seq - 1)
        def _():
            sp_ref[t // per_seq] = last


def _conv_call(xm, xt, g, w_in, kern, w_out, s1, s2):
    main = pl.BlockSpec((CONV_TM, D_MODEL), lambda i: (_main_tile(i), 0))
    return pl.pallas_call(
        _conv_kernel,
        grid=(1 + N_MAIN // CONV_TM,),
        in_specs=[main, _resident((N_TAIL, D_MODEL)), _resident((1, D_MODEL)),
                  _resident((D_MODEL, 3 * D_MODEL)), _resident((CONV_W, D_MODEL)),
                  _resident((D_MODEL, D_MODEL)),
                  _resident((N_SAMPLE, D_MODEL)), _resident((N_SAMPLE, D_MODEL))],
        out_specs=[main,
                   pl.BlockSpec((N_TAIL, D_MODEL), lambda i: (0, 0)),
                   pl.BlockSpec((N_TAIL, D_MODEL), lambda i: (0, 0)),
                   pl.BlockSpec((BATCH, CARRY, D_MODEL), lambda i: (0, 0, 0))],
        out_shape=[jax.ShapeDtypeStruct((N_MAIN, D_MODEL), F32),
                   jax.ShapeDtypeStruct((N_TAIL, D_MODEL), F32),
                   jax.ShapeDtypeStruct((N_TAIL, D_MODEL), F32),
                   jax.ShapeDtypeStruct((BATCH, CARRY, D_MODEL), F32)],
        scratch_shapes=[pltpu.VMEM((CARRY + max(N_TAIL, CONV_TM), D_MODEL), F32),
                        pltpu.VMEM((CARRY, D_MODEL), F32),
                        pltpu.VMEM((max(N_TAIL, CONV_TM), D_MODEL), BF16)],
        compiler_params=_params(),
        name="conv_mixer",
    )(xm, xt, g.reshape(1, D_MODEL), w_in, kern, w_out, s1, s2)


def _attend(qp, segs, sinks_ref, heads_per_pass=1):
    mq = qp.shape[0]
    rows = GROUP * mq
    lane = lax.broadcasted_iota(jnp.int32, (1, KV_W), 1)
    qk_head = (lane & (LANES - 1)) // HALF_DIM
    pair_head = lax.broadcasted_iota(jnp.int32, (1, LANES), 1) // HEAD_DIM
    row_head = lax.broadcasted_iota(jnp.int32, (heads_per_pass * rows, 1), 0) // mq
    masks = [jnp.concatenate([seg[2]] * (heads_per_pass * GROUP), axis=0) for seg in segs]
    outs = []
    for k0 in range(0, N_KV, heads_per_pass):
        heads = range(k0, k0 + heads_per_pass)
        qs = jnp.concatenate(
            [jnp.where(qk_head == k, qp[:, g * KV_W:(g + 1) * KV_W], jnp.zeros((), BF16))
             for k in heads for g in range(GROUP)], axis=0)
        sink = jnp.full((heads_per_pass * rows, 1), sinks_ref[k0 * GROUP] * LOG2E, F32)
        for j in range(1, heads_per_pass * GROUP):
            sink = jnp.where(row_head == j, sinks_ref[k0 * GROUP + j] * LOG2E, sink)
        scores = []
        for (kb, _, _, k_transposed), msk in zip(segs, masks):
            s = _dot(qs, kb) if k_transposed else _dot_nt(qs, kb)
            scores.append(jnp.where(msk, s, NEG))
        mx = sink
        for s in scores:
            mx = jnp.maximum(mx, jnp.max(s, axis=-1, keepdims=True))
        denom = jnp.exp2(sink - mx)
        probs = []
        for s in scores:
            p = jnp.exp2(s - mx)
            denom = denom + jnp.sum(p, axis=-1, keepdims=True)
            probs.append(p.astype(BF16))
        inv = 1.0 / denom
        for n, k in enumerate(heads):
            pair = slice((k // 2) * LANES, (k // 2 + 1) * LANES)
            part = slice(n * rows, (n + 1) * rows)
            acc = None
            for p, (_, vb, _, _) in zip(probs, segs):
                pv = _dot(p[part], jnp.where(pair_head == k % 2, vb[:, pair], jnp.zeros((), BF16)))
                acc = pv if acc is None else acc + pv
            outs.append(acc * inv[part])
    out = jnp.concatenate([outs[k] + outs[k + 1] for k in range(0, N_KV, 2)], axis=1)
    return jnp.concatenate([out[g * mq:(g + 1) * mq, :] for g in range(GROUP)], axis=1)


def _queries(x, g, wq_ref, cos, sin):
    q = _dot(_rms(x, g).astype(BF16), wq_ref[...])
    return (_rope(q, cos, sin) * (HEAD_DIM ** -0.5 * LOG2E)).astype(BF16)


def _attn_main_kernel(sinks_ref, xm_ref, g_ref, wq_ref, wo_ref, cos_ref, sin_ref,
                      km_ref, vm_ref, kt_ref, vt_ref,
                      om_ref, q_ref, ao_ref, kbuf_ref, vbuf_ref):
    i = pl.program_id(0)
    start = i % (SEQ // ATT_TM) == 0

    x = xm_ref[...]
    q_ref[...] = _queries(x, g_ref[...], wq_ref, cos_ref[...], sin_ref[...])

    @pl.when(start)
    def _():
        pad = jnp.zeros((BLK - N_META, KV_W), BF16)
        kbuf_ref[0:BLK - N_META, :] = pad
        vbuf_ref[0:BLK - N_META, :] = pad
        kbuf_ref[BLK - N_META:BLK, :] = kt_ref[N_SAMPLE:N_TAIL, :].astype(BF16)
        vbuf_ref[BLK - N_META:BLK, :] = vt_ref[N_SAMPLE:N_TAIL, :].astype(BF16)

    @pl.when(jnp.logical_not(start))
    def _():
        kbuf_ref[0:BLK, :] = kbuf_ref[ATT_TM:ATT_TM + BLK, :]
        vbuf_ref[0:BLK, :] = vbuf_ref[ATT_TM:ATT_TM + BLK, :]

    kbuf_ref[BLK:BLK + ATT_TM, :] = km_ref[...].astype(BF16)
    vbuf_ref[BLK:BLK + ATT_TM, :] = vm_ref[...].astype(BF16)

    qi = lax.broadcasted_iota(jnp.int32, (BLK, 2 * BLK), 0)
    kj = lax.broadcasted_iota(jnp.int32, (BLK, 2 * BLK), 1)
    band = (kj >= qi) & (kj <= qi + WINDOW)
    first_key = jnp.where(start, BLK - N_META, 0)
    for blk in range(ATT_TM // BLK):
        lo = blk * BLK
        mask = band & (kj >= first_key) if blk == 0 else band
        att = _attend(q_ref[lo:lo + BLK, :],
                      [(kbuf_ref[lo:lo + 2 * BLK, :], vbuf_ref[lo:lo + 2 * BLK, :], mask, False)],
                      sinks_ref)
        ao_ref[lo:lo + BLK, :] = att.astype(BF16)
    om_ref[...] = x + _dot(ao_ref[...], wo_ref[...])


def _attn_main_call(sinks, xm, g, wq, wo, cos_m, sin_m, km, vm, kt, vt):
    tile = lambda w: pl.BlockSpec((ATT_TM, w), lambda i: (i, 0))
    pos = pl.BlockSpec((ATT_TM, LANES), lambda i: (i % (SEQ // ATT_TM), 0))
    return pl.pallas_call(
        _attn_main_kernel,
        grid=(N_MAIN // ATT_TM,),
        in_specs=[pl.BlockSpec(memory_space=pltpu.SMEM),
                  tile(D_MODEL), _resident((1, D_MODEL)),
                  _resident((D_MODEL, D_MODEL)), _resident((D_MODEL, D_MODEL)),
                  pos, pos, tile(KV_W), tile(KV_W),
                  _resident((N_TAIL, KV_W)), _resident((N_TAIL, KV_W))],
        out_specs=tile(D_MODEL),
        out_shape=jax.ShapeDtypeStruct((N_MAIN, D_MODEL), F32),
        scratch_shapes=[pltpu.VMEM((ATT_TM, D_MODEL), BF16), pltpu.VMEM((ATT_TM, D_MODEL), BF16),
                        pltpu.VMEM((BLK + ATT_TM, KV_W), BF16),
                        pltpu.VMEM((BLK + ATT_TM, KV_W), BF16)],
        compiler_params=_params(),
        name="attn_main",
    )(sinks, xm, g.reshape(1, D_MODEL), wq, wo, cos_m, sin_m, km, vm, kt, vt)


def _attn_tail_kernel(sinks_ref, xt_ref, g_ref, wq_ref, wo_ref, cos_ref, sin_ref,
                      kt_ref, vt_ref, ck_ref, cv_ref, ot_ref, q_ref, ao_ref):
    s = pl.program_id(0)

    @pl.when(s == 0)
    def _():
        q_ref[...] = _queries(xt_ref[...], g_ref[...], wq_ref, cos_ref[...], sin_ref[...])
        qi = lax.broadcasted_iota(jnp.int32, (N_META, N_META), 0)
        kj = lax.broadcasted_iota(jnp.int32, (N_META, N_META), 1)
        km = kt_ref[N_SAMPLE:N_TAIL, :].astype(BF16)
        vm = vt_ref[N_SAMPLE:N_TAIL, :].astype(BF16)
        att = _attend(q_ref[N_SAMPLE:N_TAIL, :], [(km, vm, kj <= qi, False)], sinks_ref, N_KV)
        ao_ref[N_SAMPLE:N_TAIL, :] = att.astype(BF16)

    n_cached = SAMPLE_GROUP * WINDOW
    qr = lax.broadcasted_iota(jnp.int32, (GROUP_ROWS, n_cached), 0)
    kc = lax.broadcasted_iota(jnp.int32, (GROUP_ROWS, n_cached), 1)
    mask_cached = ((kc // WINDOW) == (qr // DEC_SEQ)) & ((kc % WINDOW) >= (qr % DEC_SEQ))
    qn = lax.broadcasted_iota(jnp.int32, (GROUP_ROWS, GROUP_ROWS), 0)
    kn = lax.broadcasted_iota(jnp.int32, (GROUP_ROWS, GROUP_ROWS), 1)
    mask_new = ((kn // DEC_SEQ) == (qn // DEC_SEQ)) & ((kn % DEC_SEQ) <= (qn % DEC_SEQ))
    for grp in range(GROUPS_PER_STEP):
        first = (s * GROUPS_PER_STEP + grp) * GROUP_ROWS
        rows = pl.ds(pl.multiple_of(first, GROUP_ROWS), GROUP_ROWS)
        k_new = kt_ref[rows, :]
        v_new = vt_ref[rows, :]
        seqs = range(grp * SAMPLE_GROUP, (grp + 1) * SAMPLE_GROUP)
        k_cached_t = jnp.concatenate(
            [_rows_heads_to_split(ck_ref[b].reshape(KV_W, WINDOW)) for b in seqs], axis=1)
        v_cached = jnp.concatenate([cv_ref[b].reshape(KV_W, WINDOW).T for b in seqs], axis=0)
        att = _attend(q_ref[rows, :],
                      [(k_cached_t.astype(BF16), v_cached.astype(BF16), mask_cached, True),
                       (k_new.astype(BF16), v_new.astype(BF16), mask_new, False)],
                      sinks_ref, N_KV)
        ao_ref[rows, :] = att.astype(BF16)

    @pl.when(s == N_GROUPS // GROUPS_PER_STEP - 1)
    def _():
        ot_ref[...] = xt_ref[...] + _dot(ao_ref[...], wo_ref[...])


def _attn_tail_call(sinks, xt, g, wq, wo, cos_t, sin_t, kt, vt, ck, cv):
    cached = pl.BlockSpec((GROUPS_PER_STEP * SAMPLE_GROUP, N_KV, HEAD_DIM, WINDOW),
                          lambda s: (s, 0, 0, 0))
    return pl.pallas_call(
        _attn_tail_kernel,
        grid=(N_GROUPS // GROUPS_PER_STEP,),
        in_specs=[pl.BlockSpec(memory_space=pltpu.SMEM),
                  _resident((N_TAIL, D_MODEL)), _resident((1, D_MODEL)),
                  _resident((D_MODEL, D_MODEL)), _resident((D_MODEL, D_MODEL)),
                  _resident((N_TAIL, LANES)), _resident((N_TAIL, LANES)),
                  _resident((N_TAIL, KV_W)), _resident((N_TAIL, KV_W)),
                  cached, cached],
        out_specs=pl.BlockSpec((N_TAIL, D_MODEL), lambda s: (0, 0)),
        out_shape=jax.ShapeDtypeStruct((N_TAIL, D_MODEL), F32),
        scratch_shapes=[pltpu.VMEM((N_TAIL, D_MODEL), BF16), pltpu.VMEM((N_TAIL, D_MODEL), BF16)],
        compiler_params=_params(),
        name="attn_tail",
    )(sinks, xt, g.reshape(1, D_MODEL), wq, wo, cos_t, sin_t, kt, vt, ck, cv)


def _rope_tables(pos):
    inv = 1.0 / (ROPE_THETA ** (jnp.arange(0, HEAD_DIM, 2, dtype=F32) / HEAD_DIM))
    ang = pos.astype(F32)[:, None] * inv[None, :]
    cos, sin = jnp.cos(ang), jnp.sin(ang)
    return jnp.tile(cos, (1, N_KV)), jnp.tile(sin, (1, N_KV))


def kernel(x_prompt, x_sample, state_conv, cache_k, cache_v, meta_tokens, norm_g, ffn_w_in,
           ffn_w_out, conv_w_in, conv_kernel, conv_w_out, kv_norm_g, w_kv, w_q, w_o, sinks,
           final_norm_g):
    assert x_prompt.shape == (BATCH, SEQ, D_MODEL) and x_sample.shape == (DEC_BATCH, DEC_SEQ, D_MODEL)
    assert cache_k.shape == (DEC_BATCH, WINDOW, N_KV, HEAD_DIM)
    assert norm_g.shape[0] == 2 and conv_w_in.shape[0] == 1 and w_q.shape[0] == 1

    xm = x_prompt.reshape(N_MAIN, D_MODEL)
    xt = jnp.concatenate([x_sample.reshape(N_SAMPLE, D_MODEL), meta_tokens], axis=0)

    wq = w_q[0].reshape(D_MODEL, N_KV, GROUP, 2, HALF_DIM).transpose(0, 2, 3, 1, 4)
    wq = wq.reshape(D_MODEL, D_MODEL).astype(BF16)
    wk =w_kv[:, :KV_W].reshape(D_MODEL, N_KV, 2, HALF_DIM).transpose(0, 2, 1, 3)
    wkv = jnp.concatenate([wk.reshape(D_MODEL, KV_W), w_kv[:, KV_W:]], axis=1).astype(BF16)

    pos_main = N_META + jnp.arange(SEQ, dtype=jnp.int32)
    pos_tail = jnp.concatenate([
        jnp.tile(PAST_LEN + jnp.arange(DEC_SEQ, dtype=jnp.int32), DEC_BATCH),
        jnp.arange(N_META, dtype=jnp.int32)])
    cos_m, sin_m = _rope_tables(pos_main)
    cos_t, sin_t = _rope_tables(pos_tail)

    sc = state_conv[0]
    grow = lambda a: jnp.pad(a, ((0, 0), (0, DEC_SEQ - a.shape[1]), (0, 0))).reshape(N_SAMPLE, D_MODEL)
    s1 = grow(sc[:, 1:])
    s2 = grow(sc)

    ffn_casts = lambda l, j, steps=N_TILES: [(ffn_w_in, (l, j), steps, None),
                                             (ffn_w_out, (l, j), steps // 2, None)]
    wo_cast = (w_o, (0,), N_HEADS, lambda b: (b % N_KV) * GROUP + b // N_KV)
    slot_minor = lambda a: a.transpose(0, 2, 3, 1)
    ck, cv = slot_minor(cache_k), slot_minor(cache_v)

    xm, xt, cw_in, cw_out, w_in, w_out, w_in_10, w_out_10 = _ffn_first_call(
        xm, xt, norm_g[0, 0], ffn_w_in, ffn_w_out, (0, 0),
        [(conv_w_in, (0,), N_TILES, None), (conv_w_out, (0,), N_TILES, None)]
        + ffn_casts(0, 1) + ffn_casts(1, 0), name="ffn_l0a")
    xm, xt, u_tail, sp = _conv_call(xm, xt, norm_g[0, 1], cw_in, conv_kernel[0], cw_out, s1, s2)
    xm, xt, wo, w_in_11, w_out_11 = _ffn_call(
        xm, xt, norm_g[0, 2], w_in, w_out, name="ffn_l0b", tm=WIDE_TM,
        casts=[wo_cast] + ffn_casts(1, 1, N_MAIN // WIDE_TM))

    xm, xt, km, vm, kt, vt, k_last, v_last, nk, nv = _ffn_call(
        xm, xt, norm_g[1, 0], w_in_10, w_out_10, name="ffn_l1a_kv",
        kv=(kv_norm_g, wkv, cos_m, sin_m, cos_t, sin_t, ck, cv))
    w_in, w_out = w_in_11, w_out_11
    xt = _attn_tail_call(sinks[0], xt, norm_g[1, 1], wq, wo, cos_t, sin_t, kt, vt, ck, cv)
    xm = _attn_main_call(sinks[0], xm, norm_g[1, 1], wq, wo, cos_m, sin_m, km, vm, kt, vt)
    ym, yt = _ffn_call(xm, xt, norm_g[1, 2], w_in, w_out, name="ffn_l1b_final",
                       final_g=final_norm_g, tm=WIDE_TM)

    y_prompt = ym.reshape(BATCH, SEQ, D_MODEL)
    y_sample = yt[:N_SAMPLE].reshape(DEC_BATCH, DEC_SEQ, D_MODEL)
    new_state_conv_p = sp[:, CARRY - (CONV_W - 1):][None]
    new_state_conv_s = u_tail[:N_SAMPLE].reshape(DEC_BATCH, DEC_SEQ, D_MODEL)[:, DEC_SEQ - (CONV_W - 1):][None]
    slot_major = lambda a: a.transpose(0, 3, 1, 2)
    new_k_p = slot_major(k_last.reshape(BATCH, N_KV, HEAD_DIM, WINDOW))
    new_v_p = slot_major(v_last.reshape(BATCH, N_KV, HEAD_DIM, WINDOW))
    return (y_prompt, y_sample, new_state_conv_p, new_state_conv_s,
            new_k_p, new_v_p, slot_major(nk), slot_major(nv))
```

```python
import jax
import jax.numpy as jnp
from jax import lax
from jax.experimental import pallas as pl
from jax.experimental.pallas import tpu as pltpu

D_MODEL = 1024
BATCH = 2
SEQ = 8192
DEC_BATCH = 128
DEC_SEQ = 4
PAST_LEN = 8192
N_META = 16
D_FF = 2816
CONV_W = 3
HEAD_DIM = 64
N_HEADS = 16
N_KV = 4
GROUP = 4
WINDOW = 128
ROPE_THETA = 10000.0
EPS = 1e-6
NEG = -1e30
LOG2E = 1.4426950408889634

KV_W = N_KV * HEAD_DIM
HALF_DIM = HEAD_DIM // 2
N_MAIN = BATCH * SEQ
N_SAMPLE = DEC_BATCH * DEC_SEQ
N_TAIL = N_SAMPLE + N_META
TM = 512
N_TILES = N_MAIN // TM
TILES_PER_SEQ = SEQ // TM
WIDE_TM = 1024
FF_CHUNK = 256
OUT_CHUNK = 256
BLK = WINDOW
ATT_TM = 1024
SAMPLE_GROUP = 8
N_GROUPS = DEC_BATCH // SAMPLE_GROUP
GROUPS_PER_STEP = 2
GROUP_ROWS = SAMPLE_GROUP * DEC_SEQ
LANES = 128
BF16_ROWS = 16
V7X_VMEM_LIMIT = 56 * 1024 * 1024

F32 = jnp.float32
BF16 = jnp.bfloat16


def _rms(x, g):
    return x * lax.rsqrt(jnp.mean(x * x, axis=-1, keepdims=True) + EPS) * g


def _rope(x, cos, sin):
    outs = []
    for c in range(x.shape[1] // KV_W):
        x1 = x[:, c * KV_W:c * KV_W + LANES]
        x2 = x[:, c * KV_W + LANES:(c + 1) * KV_W]
        outs += [x1 * cos - x2 * sin, x2 * cos + x1 * sin]
    return jnp.concatenate(outs, axis=1)


def _rows_split_to_heads(t):
    return jnp.concatenate([t[half * LANES + h * HALF_DIM:half * LANES + (h + 1) * HALF_DIM]
                            for h in range(N_KV) for half in range(2)], axis=0)


def _rows_heads_to_split(t):
    return jnp.concatenate([t[h * HEAD_DIM + half * HALF_DIM:h * HEAD_DIM + (half + 1) * HALF_DIM]
                            for half in range(2) for h in range(N_KV)], axis=0)


def _dot(a, b):
    return jnp.dot(a, b, preferred_element_type=F32)


def _dot_nt(a, b):
    return lax.dot_general(a, b, (((1,), (1,)), ((), ())), preferred_element_type=F32)


def _resident(shape):
    nd = len(shape)
    return pl.BlockSpec(shape, lambda i: (0,) * nd, pipeline_mode=pl.Buffered(1))


def _main_tile(i):
    return jnp.maximum(i - 1, 0)


def _params():
    return pltpu.CompilerParams(dimension_semantics=("arbitrary",),
                                vmem_limit_bytes=V7X_VMEM_LIMIT)


def _ffn_rows(x_ref, o_ref, g, w_in_ref, w_out_ref, a_ref, final_g=None):
    x = x_ref[...]
    m = x.shape[0]
    hn = _rms(x, g).astype(BF16)
    for c in range(D_FF // FF_CHUNK):
        lo = c * FF_CHUNK
        a_ref[0:m, lo:lo + FF_CHUNK] = _swiglu_chunk(
            hn, w_in_ref[:, lo:lo + FF_CHUNK], w_in_ref[:, D_FF + lo:D_FF + lo + FF_CHUNK])
    _ffn_out(x, o_ref, a_ref, w_out_ref, final_g)


def _swiglu_chunk(hn, w_gate, w_up):
    gate = _dot(hn, w_gate)
    return (gate * jax.nn.sigmoid(gate) * _dot(hn, w_up)).astype(BF16)


def _ffn_out(x, o_ref, a_ref, w_out_ref, final_g):
    m = x.shape[0]
    sq = None
    for c in range(D_MODEL // OUT_CHUNK):
        cols = slice(c * OUT_CHUNK, (c + 1) * OUT_CHUNK)
        y = x[:, cols] + 0.5 * _dot(a_ref[0:m, :], w_out_ref[:, cols])
        o_ref[:, cols] = y
        if final_g is not None:
            part = jnp.sum(y * y, axis=-1, keepdims=True)
            sq = part if sq is None else sq + part
    if final_g is not None:
        o_ref[...] = o_ref[...] * lax.rsqrt(sq * (1.0 / D_MODEL) + EPS) * final_g


N_LOAD = 8


def _make_ffn_first_kernel(n_cast):
    in_rows, out_rows = D_MODEL // N_LOAD, D_FF // N_LOAD

    def body(*refs):
        xm_ref, xt_ref, g_ref, wi_ref, wo_ref = refs[:5]
        cast_src = refs[5:5 + n_cast]
        om_ref, ot_ref = refs[5 + n_cast:7 + n_cast]
        cast_dst = refs[7 + n_cast:7 + 2 * n_cast]
        a_ref, w_in_ref, w_out_ref = refs[7 + 2 * n_cast:]
        i = pl.program_id(0)

        for src, dst in zip(cast_src, cast_dst):
            dst[...] = src[...].astype(BF16)

        @pl.when(i < N_LOAD)
        def _():
            w_in_ref[pl.ds(pl.multiple_of(i * in_rows, in_rows), in_rows), :] = wi_ref[...].astype(BF16)
            w_out_ref[pl.ds(pl.multiple_of(i * out_rows, out_rows), out_rows), :] = wo_ref[...].astype(BF16)

        @pl.when(i == N_LOAD)
        def _():
            _ffn_rows(xt_ref, ot_ref, g_ref[...], w_in_ref, w_out_ref, a_ref)

        @pl.when(i > N_LOAD)
        def _():
            _ffn_rows(xm_ref, om_ref, g_ref[...], w_in_ref, w_out_ref, a_ref)

    return body


def _ffn_first_call(xm, xt, g, ffn_w_in, ffn_w_out, lead, casts, *, name):
    slab = lambda i: jnp.minimum(i, N_LOAD - 1)
    nl = (None,) * len(lead)
    first_main = N_LOAD + 1
    main = pl.BlockSpec((TM, D_MODEL), lambda i: (jnp.maximum(i - first_main, 0), 0))
    c_in, c_out, c_shape, c_args = _cast_specs(casts, first_main)
    assert D_MODEL % (N_LOAD * BF16_ROWS) == 0 and D_FF % (N_LOAD * BF16_ROWS) == 0
    return pl.pallas_call(
        _make_ffn_first_kernel(len(casts)),
        grid=(first_main + N_TILES,),
        in_specs=[main, _resident((N_TAIL, D_MODEL)), _resident((1, D_MODEL)),
                  pl.BlockSpec(nl + (D_MODEL // N_LOAD, 2 * D_FF), lambda i: lead + (slab(i), 0)),
                  pl.BlockSpec(nl + (D_FF // N_LOAD, D_MODEL), lambda i: lead + (slab(i), 0))] + c_in,
        out_specs=[main, _resident((N_TAIL, D_MODEL))] + c_out,
        out_shape=[jax.ShapeDtypeStruct((N_MAIN, D_MODEL), F32),
                   jax.ShapeDtypeStruct((N_TAIL, D_MODEL), F32)] + c_shape,
        scratch_shapes=[pltpu.VMEM((N_TAIL, D_FF), BF16),
                        pltpu.VMEM((D_MODEL, 2 * D_FF), BF16), pltpu.VMEM((D_FF, D_MODEL), BF16)],
        compiler_params=_params(),
        name=name,
    )(xm, xt, g.reshape(1, D_MODEL), ffn_w_in, ffn_w_out, *c_args)


def _make_ffn_kernel(has_final, has_kv, has_roll, n_cast, tm):
    cache_seqs = DEC_BATCH * tm // N_MAIN

    def body(*refs):
        refs = list(refs)
        xm_ref, xt_ref, g_ref, w_in_ref, w_out_ref = refs[:5]
        pos = 5
        if has_final:
            gf_ref = refs[pos]
            pos += 1
        if has_kv:
            gkv_ref, wkv_ref, cm_ref, sm_ref, ct_ref, st_ref = refs[pos:pos + 6]
            pos += 6
        if has_roll:
            newk_ref, newv_ref, ck_ref, cv_ref = refs[pos:pos + 4]
            pos += 4
        cast_src = refs[pos:pos + n_cast]
        pos += n_cast
        om_ref, ot_ref = refs[pos:pos + 2]
        pos += 2
        if has_kv:
            km_ref, vm_ref, kt_ref, vt_ref, kl_ref, vl_ref, newk_ref, newv_ref = refs[pos:pos + 8]
            pos += 8
        if has_roll:
            nk_ref, nv_ref = refs[pos:pos + 2]
            pos += 2
        cast_dst = refs[pos:pos + n_cast]
        pos += n_cast
        a_ref = refs[pos]
        i = pl.program_id(0)

        for src, dst in zip(cast_src, cast_dst):
            dst[...] = src[...].astype(BF16)

        def rows(x_ref, o_ref):
            _ffn_rows(x_ref, o_ref, g_ref[...], w_in_ref, w_out_ref, a_ref,
                      gf_ref[...] if has_final else None)

        def project_kv(x_ref, cos_ref, sin_ref, k_ref, v_ref):
            kv = _dot(_rms(x_ref[...], gkv_ref[...]).astype(BF16), wkv_ref[...])
            k = _rope(kv[:, :KV_W], cos_ref[...], sin_ref[...])
            v = kv[:, KV_W:]
            k_ref[...] = k
            v_ref[...] = v
            return k, v

        @pl.when(i == 0)
        def _():
            if has_kv:
                k, v = project_kv(xt_ref, ct_ref, st_ref, kt_ref, vt_ref)
                for c in range(N_SAMPLE // LANES):
                    newk_ref[c] = _rows_split_to_heads(k[c * LANES:(c + 1) * LANES, :].T)
                    newv_ref[c] = v[c * LANES:(c + 1) * LANES, :].T
            rows(xt_ref, ot_ref)

        @pl.when(i > 0)
        def _():
            if has_kv:
                k, v = project_kv(xm_ref, cm_ref, sm_ref, km_ref, vm_ref)
                kl_ref[...] = _rows_split_to_heads(k[tm - WINDOW:, :].T)
                vl_ref[...] = v[tm - WINDOW:, :].T
            if has_roll:
                t = i - 1
                lane = lax.broadcasted_iota(jnp.int32, (1, WINDOW), 1)
                is_new = lane >= WINDOW - DEC_SEQ
                first_lane = (t * cache_seqs * DEC_SEQ) % LANES
                for new_ref, c_ref, n_ref in ((newk_ref, ck_ref, nk_ref), (newv_ref, cv_ref, nv_ref)):
                    new_t = new_ref[(t * cache_seqs * DEC_SEQ) // LANES]
                    for b in range(cache_seqs):
                        shifted = pltpu.roll(c_ref[b].reshape(KV_W, WINDOW), WINDOW - DEC_SEQ, 1)
                        amount = (WINDOW - DEC_SEQ - b * DEC_SEQ - first_lane) & (LANES - 1)
                        placed = pltpu.roll(new_t, amount, 1)
                        n_ref[b] = jnp.where(is_new, placed, shifted).reshape(N_KV, HEAD_DIM, WINDOW)
            rows(xm_ref, om_ref)

    return body


def _cast_specs(casts, first_main_step=1):
    in_specs, out_specs, out_shape, args = [], [], [], []
    for arr, lead, steps, src_block in casts:
        n_rows, n_cols = arr.shape[-2:]
        rows = n_rows // steps
        assert rows * steps == n_rows and rows % BF16_ROWS == 0
        blk = lambda i, steps=steps: jnp.clip(i - first_main_step, 0, steps - 1)
        src = blk if src_block is None else (lambda i, blk=blk, f=src_block: f(blk(i)))
        in_specs.append(pl.BlockSpec((None,) * len(lead) + (rows, n_cols),
                                     lambda i, lead=lead, src=src: lead + (src(i), 0)))
        out_specs.append(pl.BlockSpec((rows, n_cols), lambda i, blk=blk: (blk(i), 0)))
        out_shape.append(jax.ShapeDtypeStruct((n_rows, n_cols), BF16))
        args.append(arr)
    return in_specs, out_specs, out_shape, args


def _ffn_call(xm, xt, g, w_in, w_out, *, name, final_g=None, kv=None, roll=None, casts=(), tm=TM):
    new_t_shape = (N_SAMPLE // LANES, KV_W, LANES)
    n_tiles = N_MAIN // tm
    assert all(c[2] <= n_tiles for c in casts)
    main = pl.BlockSpec((tm, D_MODEL), lambda i: (_main_tile(i), 0))
    in_specs = [main, _resident((N_TAIL, D_MODEL)), _resident((1, D_MODEL)),
                _resident((D_MODEL, 2 * D_FF)), _resident((D_FF, D_MODEL))]
    args = [xm, xt, g.reshape(1, D_MODEL), w_in, w_out]
    out_shape = [jax.ShapeDtypeStruct((N_MAIN, D_MODEL), F32),
                 jax.ShapeDtypeStruct((N_TAIL, D_MODEL), F32)]
    out_specs = [main, _resident((N_TAIL, D_MODEL))]
    if final_g is not None:
        in_specs.append(_resident((1, D_MODEL)))
        args.append(final_g.reshape(1, D_MODEL))
    scratch = [pltpu.VMEM((max(N_TAIL, tm), D_FF), BF16)]
    if kv is not None:
        g_kv, w_kv, cos_m, sin_m, cos_t, sin_t = kv
        pos = pl.BlockSpec((tm, LANES), lambda i: (_main_tile(i) % (SEQ // tm), 0))
        kv_main = pl.BlockSpec((tm, KV_W), lambda i: (_main_tile(i), 0))
        in_specs += [_resident((1, D_MODEL)), _resident((D_MODEL, 2 * KV_W)), pos, pos,
                     _resident((N_TAIL, LANES)), _resident((N_TAIL, LANES))]
        args += [g_kv.reshape(1, D_MODEL), w_kv, cos_m, sin_m, cos_t, sin_t]
        out_shape += [jax.ShapeDtypeStruct((N_MAIN, KV_W), F32)] * 2
        out_shape += [jax.ShapeDtypeStruct((N_TAIL, KV_W), F32)] * 2
        out_shape += [jax.ShapeDtypeStruct((BATCH, KV_W, WINDOW), F32)] * 2
        out_shape += [jax.ShapeDtypeStruct(new_t_shape, F32)] * 2
        last = pl.BlockSpec((None, KV_W, WINDOW), lambda i: (_main_tile(i) // (SEQ // tm), 0, 0))
        out_specs += [kv_main, kv_main, _resident((N_TAIL, KV_W)), _resident((N_TAIL, KV_W)),
                      last, last, _resident(new_t_shape), _resident(new_t_shape)]
    if roll is not None:
        cached = pl.BlockSpec((DEC_BATCH // n_tiles, N_KV, HEAD_DIM, WINDOW),
                              lambda i: (_main_tile(i), 0, 0, 0))
        in_specs += [_resident(new_t_shape), _resident(new_t_shape), cached, cached]
        args += list(roll)
        out_shape += [jax.ShapeDtypeStruct((DEC_BATCH, N_KV, HEAD_DIM, WINDOW), F32)] * 2
        out_specs += [cached, cached]
    c_in, c_out, c_shape, c_args = _cast_specs(casts)
    return pl.pallas_call(
        _make_ffn_kernel(final_g is not None, kv is not None, roll is not None, len(casts), tm),
        grid=(1 + n_tiles,),
        in_specs=in_specs + c_in,
        out_specs=out_specs + c_out,
        out_shape=out_shape + c_shape,
        scratch_shapes=scratch,
        input_output_aliases={0: 0},
        compiler_params=_params(),
        name=name,
    )(*args, *c_args)


CARRY = 8
CONV_TM = 1024


def _conv_rows(x_ref, o_ref, g, w_in_ref, kern_ref, w_out_ref, ubuf_ref, mbuf_ref, fix):
    x = x_ref[...]
    m = x.shape[0]
    hn = _rms(x, g).astype(BF16)
    for c in range(D_MODEL // OUT_CHUNK):
        cols = slice(c * OUT_CHUNK, (c + 1) * OUT_CHUNK)
        b, cc, z = (_dot(hn, w_in_ref[:, j * D_MODEL + c * OUT_CHUNK:j * D_MODEL + (c + 1) * OUT_CHUNK])
                    for j in range(3))
        u = cc * z
        ubuf_ref[CARRY:CARRY + m, cols] = u
        prev1 = ubuf_ref[CARRY - 1:CARRY - 1 + m, cols]
        prev2 = ubuf_ref[CARRY - 2:CARRY - 2 + m, cols]
        if fix is not None:
            prev1, prev2 = fix(prev1, prev2, cols)
        conv = kern_ref[0:1, cols] * prev2 + kern_ref[1:2, cols] * prev1 + kern_ref[2:3, cols] * u
        mbuf_ref[0:m, cols] = (b * conv).astype(BF16)
    for c in range(D_MODEL // OUT_CHUNK):
        cols = slice(c * OUT_CHUNK, (c + 1) * OUT_CHUNK)
        o_ref[:, cols] = x[:, cols] + _dot(mbuf_ref[0:m, :], w_out_ref[:, cols])


def _conv_kernel(xm_ref, xt_ref, g_ref, w_in_ref, kern_ref, w_out_ref, s1_ref, s2_ref,
                 om_ref, ot_ref, ut_ref, sp_ref, ubuf_ref, meta_ref, mbuf_ref):
    i = pl.program_id(0)

    @pl.when(i == 0)
    def _():
        ubuf_ref[0:CARRY, :] = jnp.zeros((CARRY, D_MODEL), F32)

        def fix(prev1, prev2, cols):
            step = lax.broadcasted_iota(jnp.int32, (N_SAMPLE, 1), 0) & (DEC_SEQ - 1)
            meta = lax.broadcasted_iota(jnp.int32, (N_META, 1), 0)
            p1 = [jnp.where(step >= 1, prev1[:N_SAMPLE], s1_ref[:, cols]),
                  jnp.where(meta >= 1, prev1[N_SAMPLE:], 0.0)]
            p2 = [jnp.where(step >= 2, prev2[:N_SAMPLE], s2_ref[:, cols]),
                  jnp.where(meta >= 2, prev2[N_SAMPLE:], 0.0)]
            return jnp.concatenate(p1, axis=0), jnp.concatenate(p2, axis=0)

        _conv_rows(xt_ref, ot_ref, g_ref[...], w_in_ref, kern_ref, w_out_ref, ubuf_ref, mbuf_ref, fix)
        ut_ref[...] = ubuf_ref[CARRY:CARRY + N_TAIL, :]
        meta_ref[...] = ubuf_ref[N_TAIL:N_TAIL + CARRY, :]

    @pl.when(i > 0)
    def _():
        t = i - 1
        per_seq = SEQ // CONV_TM

        @pl.when(t % per_seq == 0)
        def _():
            ubuf_ref[0:CARRY, :] = meta_ref[...]

        _conv_rows(xm_ref, om_ref, g_ref[...], w_in_ref, kern_ref, w_out_ref, ubuf_ref, mbuf_ref, None)
        last = ubuf_ref[CONV_TM:CONV_TM + CARRY, :]
        ubuf_ref[0:CARRY, :] = last

        @pl.when(t % per_seq == per_seq - 1)
        def _():
            sp_ref[t // per_seq] = last


def _conv_call(xm, xt, g, w_in, kern, w_out, s1, s2):
    main = pl.BlockSpec((CONV_TM, D_MODEL), lambda i: (_main_tile(i), 0))
    return pl.pallas_call(
        _conv_kernel,
        grid=(1 + N_MAIN // CONV_TM,),
        in_specs=[main, _resident((N_TAIL, D_MODEL)), _resident((1, D_MODEL)),
                  _resident((D_MODEL, 3 * D_MODEL)), _resident((CONV_W, D_MODEL)),
                  _resident((D_MODEL, D_MODEL)),
                  _resident((N_SAMPLE, D_MODEL)), _resident((N_SAMPLE, D_MODEL))],
        out_specs=[main,
                   pl.BlockSpec((N_TAIL, D_MODEL), lambda i: (0, 0)),
                   pl.BlockSpec((N_TAIL, D_MODEL), lambda i: (0, 0)),
                   pl.BlockSpec((BATCH, CARRY, D_MODEL), lambda i: (0, 0, 0))],
        out_shape=[jax.ShapeDtypeStruct((N_MAIN, D_MODEL), F32),
                   jax.ShapeDtypeStruct((N_TAIL, D_MODEL), F32),
                   jax.ShapeDtypeStruct((N_TAIL, D_MODEL), F32),
                   jax.ShapeDtypeStruct((BATCH, CARRY, D_MODEL), F32)],
        scratch_shapes=[pltpu.VMEM((CARRY + max(N_TAIL, CONV_TM), D_MODEL), F32),
                        pltpu.VMEM((CARRY, D_MODEL), F32),
                        pltpu.VMEM((max(N_TAIL, CONV_TM), D_MODEL), BF16)],
        input_output_aliases={0: 0},
        compiler_params=_params(),
        name="conv_mixer",
    )(xm, xt, g.reshape(1, D_MODEL), w_in, kern, w_out, s1, s2)


def _attend(qp, segs, sinks_ref, heads_per_pass=1):
    mq = qp.shape[0]
    rows = GROUP * mq
    lane = lax.broadcasted_iota(jnp.int32, (1, KV_W), 1)
    qk_head = (lane & (LANES - 1)) // HALF_DIM
    pair_head = lax.broadcasted_iota(jnp.int32, (1, LANES), 1) // HEAD_DIM
    row_head = lax.broadcasted_iota(jnp.int32, (heads_per_pass * rows, 1), 0) // mq
    masks = [jnp.concatenate([seg[2]] * (heads_per_pass * GROUP), axis=0) for seg in segs]
    outs = []
    for k0 in range(0, N_KV, heads_per_pass):
        heads = range(k0, k0 + heads_per_pass)
        qs = jnp.concatenate(
            [jnp.where(qk_head == k, qp[:, g * KV_W:(g + 1) * KV_W], jnp.zeros((), BF16))
             for k in heads for g in range(GROUP)], axis=0)
        sink = jnp.full((heads_per_pass * rows, 1), sinks_ref[k0 * GROUP] * LOG2E, F32)
        for j in range(1, heads_per_pass * GROUP):
            sink = jnp.where(row_head == j, sinks_ref[k0 * GROUP + j] * LOG2E, sink)
        scores = []
        for (kb, _, _, k_transposed), msk in zip(segs, masks):
            s = _dot(qs, kb) if k_transposed else _dot_nt(qs, kb)
            scores.append(jnp.where(msk, s, NEG))
        mx = sink
        for s in scores:
            mx = jnp.maximum(mx, jnp.max(s, axis=-1, keepdims=True))
        denom = jnp.exp2(sink - mx)
        probs = []
        for s in scores:
            p = jnp.exp2(s - mx)
            denom = denom + jnp.sum(p, axis=-1, keepdims=True)
            probs.append(p.astype(BF16))
        inv = 1.0 / denom
        for n, k in enumerate(heads):
            pair = slice((k // 2) * LANES, (k // 2 + 1) * LANES)
            part = slice(n * rows, (n + 1) * rows)
            acc = None
            for p, (_, vb, _, _) in zip(probs, segs):
                pv = _dot(p[part], jnp.where(pair_head == k % 2, vb[:, pair], jnp.zeros((), BF16)))
                acc = pv if acc is None else acc + pv
            outs.append(acc * inv[part])
    out = jnp.concatenate([outs[k] + outs[k + 1] for k in range(0, N_KV, 2)], axis=1)
    return jnp.concatenate([out[g * mq:(g + 1) * mq, :] for g in range(GROUP)], axis=1)


def _queries(x, g, wq_ref, cos, sin):
    q = _dot(_rms(x, g).astype(BF16), wq_ref[...])
    return (_rope(q, cos, sin) * (HEAD_DIM ** -0.5 * LOG2E)).astype(BF16)


def _attn_main_kernel(sinks_ref, xm_ref, g_ref, wq_ref, wo_ref, cos_ref, sin_ref,
                      km_ref, vm_ref, kt_ref, vt_ref,
                      om_ref, q_ref, ao_ref, kbuf_ref, vbuf_ref):
    i = pl.program_id(0)
    start = i % (SEQ // ATT_TM) == 0

    x = xm_ref[...]
    q_ref[...] = _queries(x, g_ref[...], wq_ref, cos_ref[...], sin_ref[...])

    @pl.when(start)
    def _():
        pad = jnp.zeros((BLK - N_META, KV_W), BF16)
        kbuf_ref[0:BLK - N_META, :] = pad
        vbuf_ref[0:BLK - N_META, :] = pad
        kbuf_ref[BLK - N_META:BLK, :] = kt_ref[N_SAMPLE:N_TAIL, :].astype(BF16)
        vbuf_ref[BLK - N_META:BLK, :] = vt_ref[N_SAMPLE:N_TAIL, :].astype(BF16)

    @pl.when(jnp.logical_not(start))
    def _():
        kbuf_ref[0:BLK, :] = kbuf_ref[ATT_TM:ATT_TM + BLK, :]
        vbuf_ref[0:BLK, :] = vbuf_ref[ATT_TM:ATT_TM + BLK, :]

    kbuf_ref[BLK:BLK + ATT_TM, :] = km_ref[...].astype(BF16)
    vbuf_ref[BLK:BLK + ATT_TM, :] = vm_ref[...].astype(BF16)

    qi = lax.broadcasted_iota(jnp.int32, (BLK, 2 * BLK), 0)
    kj = lax.broadcasted_iota(jnp.int32, (BLK, 2 * BLK), 1)
    band = (kj >= qi) & (kj <= qi + WINDOW)
    first_key = jnp.where(start, BLK - N_META, 0)
    for blk in range(ATT_TM // BLK):
        lo = blk * BLK
        mask = band & (kj >= first_key) if blk == 0 else band
        att = _attend(q_ref[lo:lo + BLK, :],
                      [(kbuf_ref[lo:lo + 2 * BLK, :], vbuf_ref[lo:lo + 2 * BLK, :], mask, False)],
                      sinks_ref)
        ao_ref[lo:lo + BLK, :] = att.astype(BF16)
    om_ref[...] = x + _dot(ao_ref[...], wo_ref[...])


def _attn_main_call(sinks, xm, g, wq, wo, cos_m, sin_m, km, vm, kt, vt):
    tile = lambda w: pl.BlockSpec((ATT_TM, w), lambda i: (i, 0))
    pos = pl.BlockSpec((ATT_TM, LANES), lambda i: (i % (SEQ // ATT_TM), 0))
    return pl.pallas_call(
        _attn_main_kernel,
        grid=(N_MAIN // ATT_TM,),
        in_specs=[pl.BlockSpec(memory_space=pltpu.SMEM),
                  tile(D_MODEL), _resident((1, D_MODEL)),
                  _resident((D_MODEL, D_MODEL)), _resident((D_MODEL, D_MODEL)),
                  pos, pos, tile(KV_W), tile(KV_W),
                  _resident((N_TAIL, KV_W)), _resident((N_TAIL, KV_W))],
        out_specs=tile(D_MODEL),
        out_shape=jax.ShapeDtypeStruct((N_MAIN, D_MODEL), F32),
        scratch_shapes=[pltpu.VMEM((ATT_TM, D_MODEL), BF16), pltpu.VMEM((ATT_TM, D_MODEL), BF16),
                        pltpu.VMEM((BLK + ATT_TM, KV_W), BF16),
                        pltpu.VMEM((BLK + ATT_TM, KV_W), BF16)],
        input_output_aliases={1: 0},
        compiler_params=_params(),
        name="attn_main",
    )(sinks, xm, g.reshape(1, D_MODEL), wq, wo, cos_m, sin_m, km, vm, kt, vt)


def _attn_tail_kernel(sinks_ref, xt_ref, g_ref, wq_ref, wo_ref, cos_ref, sin_ref,
                      kt_ref, vt_ref, ck_ref, cv_ref, ot_ref, q_ref, ao_ref):
    s = pl.program_id(0)

    @pl.when(s == 0)
    def _():
        q_ref[...] = _queries(xt_ref[...], g_ref[...], wq_ref, cos_ref[...], sin_ref[...])
        qi = lax.broadcasted_iota(jnp.int32, (N_META, N_META), 0)
        kj = lax.broadcasted_iota(jnp.int32, (N_META, N_META), 1)
        km = kt_ref[N_SAMPLE:N_TAIL, :].astype(BF16)
        vm = vt_ref[N_SAMPLE:N_TAIL, :].astype(BF16)
        att = _attend(q_ref[N_SAMPLE:N_TAIL, :], [(km, vm, kj <= qi, False)], sinks_ref, N_KV)
        ao_ref[N_SAMPLE:N_TAIL, :] = att.astype(BF16)

    n_cached = SAMPLE_GROUP * WINDOW
    qr = lax.broadcasted_iota(jnp.int32, (GROUP_ROWS, n_cached), 0)
    kc = lax.broadcasted_iota(jnp.int32, (GROUP_ROWS, n_cached), 1)
    mask_cached = ((kc // WINDOW) == (qr // DEC_SEQ)) & ((kc % WINDOW) >= (qr % DEC_SEQ))
    qn = lax.broadcasted_iota(jnp.int32, (GROUP_ROWS, GROUP_ROWS), 0)
    kn = lax.broadcasted_iota(jnp.int32, (GROUP_ROWS, GROUP_ROWS), 1)
    mask_new = ((kn // DEC_SEQ) == (qn // DEC_SEQ)) & ((kn % DEC_SEQ) <= (qn % DEC_SEQ))
    for grp in range(GROUPS_PER_STEP):
        first = (s * GROUPS_PER_STEP + grp) * GROUP_ROWS
        rows = pl.ds(pl.multiple_of(first, GROUP_ROWS), GROUP_ROWS)
        k_new = kt_ref[rows, :]
        v_new = vt_ref[rows, :]
        seqs = range(grp * SAMPLE_GROUP, (grp + 1) * SAMPLE_GROUP)
        k_cached_t = jnp.concatenate(
            [_rows_heads_to_split(ck_ref[b].reshape(KV_W, WINDOW)) for b in seqs], axis=1)
        v_cached = jnp.concatenate([cv_ref[b].reshape(KV_W, WINDOW).T for b in seqs], axis=0)
        att = _attend(q_ref[rows, :],
                      [(k_cached_t.astype(BF16), v_cached.astype(BF16), mask_cached, True),
                       (k_new.astype(BF16), v_new.astype(BF16), mask_new, False)],
                      sinks_ref, N_KV)
        ao_ref[rows, :] = att.astype(BF16)

    @pl.when(s == N_GROUPS // GROUPS_PER_STEP - 1)
    def _():
        ot_ref[...] = xt_ref[...] + _dot(ao_ref[...], wo_ref[...])


def _attn_tail_call(sinks, xt, g, wq, wo, cos_t, sin_t, kt, vt, ck, cv):
    cached = pl.BlockSpec((GROUPS_PER_STEP * SAMPLE_GROUP, N_KV, HEAD_DIM, WINDOW),
                          lambda s: (s, 0, 0, 0))
    return pl.pallas_call(
        _attn_tail_kernel,
        grid=(N_GROUPS // GROUPS_PER_STEP,),
        in_specs=[pl.BlockSpec(memory_space=pltpu.SMEM),
                  _resident((N_TAIL, D_MODEL)), _resident((1, D_MODEL)),
                  _resident((D_MODEL, D_MODEL)), _resident((D_MODEL, D_MODEL)),
                  _resident((N_TAIL, LANES)), _resident((N_TAIL, LANES)),
                  _resident((N_TAIL, KV_W)), _resident((N_TAIL, KV_W)),
                  cached, cached],
        out_specs=pl.BlockSpec((N_TAIL, D_MODEL), lambda s: (0, 0)),
        out_shape=jax.ShapeDtypeStruct((N_TAIL, D_MODEL), F32),
        scratch_shapes=[pltpu.VMEM((N_TAIL, D_MODEL), BF16), pltpu.VMEM((N_TAIL, D_MODEL), BF16)],
        compiler_params=_params(),
        name="attn_tail",
    )(sinks, xt, g.reshape(1, D_MODEL), wq, wo, cos_t, sin_t, kt, vt, ck, cv)


def _rope_tables(pos):
    inv = 1.0 / (ROPE_THETA ** (jnp.arange(0, HEAD_DIM, 2, dtype=F32) / HEAD_DIM))
    ang = pos.astype(F32)[:, None] * inv[None, :]
    cos, sin = jnp.cos(ang), jnp.sin(ang)
    return jnp.tile(cos, (1, N_KV)), jnp.tile(sin, (1, N_KV))


def kernel(x_prompt, x_sample, state_conv, cache_k, cache_v, meta_tokens, norm_g, ffn_w_in,
           ffn_w_out, conv_w_in, conv_kernel, conv_w_out, kv_norm_g, w_kv, w_q, w_o, sinks,
           final_norm_g):
    assert x_prompt.shape == (BATCH, SEQ, D_MODEL) and x_sample.shape == (DEC_BATCH, DEC_SEQ, D_MODEL)
    assert cache_k.shape == (DEC_BATCH, WINDOW, N_KV, HEAD_DIM)
    assert norm_g.shape[0] == 2 and conv_w_in.shape[0] == 1 and w_q.shape[0] == 1

    xm = x_prompt.reshape(N_MAIN, D_MODEL)
    xt = jnp.concatenate([x_sample.reshape(N_SAMPLE, D_MODEL), meta_tokens], axis=0)

    wq = w_q[0].reshape(D_MODEL, N_KV, GROUP, 2, HALF_DIM).transpose(0, 2, 3, 1, 4)
    wq = wq.reshape(D_MODEL, D_MODEL).astype(BF16)
    wk =w_kv[:, :KV_W].reshape(D_MODEL, N_KV, 2, HALF_DIM).transpose(0, 2, 1, 3)
    wkv = jnp.concatenate([wk.reshape(D_MODEL, KV_W), w_kv[:, KV_W:]], axis=1).astype(BF16)

    pos_main = N_META + jnp.arange(SEQ, dtype=jnp.int32)
    pos_tail = jnp.concatenate([
        jnp.tile(PAST_LEN + jnp.arange(DEC_SEQ, dtype=jnp.int32), DEC_BATCH),
        jnp.arange(N_META, dtype=jnp.int32)])
    cos_m, sin_m = _rope_tables(pos_main)
    cos_t, sin_t = _rope_tables(pos_tail)

    sc = state_conv[0]
    grow = lambda a: jnp.pad(a, ((0, 0), (0, DEC_SEQ - a.shape[1]), (0, 0))).reshape(N_SAMPLE, D_MODEL)
    s1 = grow(sc[:, 1:])
    s2 = grow(sc)

    ffn_casts = lambda l, j, steps=N_TILES: [(ffn_w_in, (l, j), steps, None),
                                             (ffn_w_out, (l, j), steps // 2, None)]
    wo_cast = (w_o, (0,), N_HEADS, lambda b: (b % N_KV) * GROUP + b // N_KV)
    slot_minor = lambda a: a.transpose(0, 2, 3, 1)
    ck, cv = slot_minor(cache_k), slot_minor(cache_v)

    xm, xt, cw_in, cw_out, w_in, w_out, w_in_10, w_out_10 = _ffn_first_call(
        xm, xt, norm_g[0, 0], ffn_w_in, ffn_w_out, (0, 0),
        [(conv_w_in, (0,), N_TILES, None), (conv_w_out, (0,), N_TILES, None)]
        + ffn_casts(0, 1) + ffn_casts(1, 0), name="ffn_l0a")
    xm, xt, u_tail, sp = _conv_call(xm, xt, norm_g[0, 1], cw_in, conv_kernel[0], cw_out, s1, s2)
    xm, xt, wo, w_in_11, w_out_11 = _ffn_call(
        xm, xt, norm_g[0, 2], w_in, w_out, name="ffn_l0b", tm=WIDE_TM,
        casts=[wo_cast] + ffn_casts(1, 1, N_MAIN // WIDE_TM))

    xm, xt, km, vm, kt, vt, k_last, v_last, new_k_t, new_v_t = _ffn_call(
        xm, xt, norm_g[1, 0], w_in_10, w_out_10, name="ffn_l1a_kv", tm=WIDE_TM,
        kv=(kv_norm_g, wkv, cos_m, sin_m, cos_t, sin_t))
    xt = _attn_tail_call(sinks[0], xt, norm_g[1, 1], wq, wo, cos_t, sin_t, kt, vt, ck, cv)
    xm = _attn_main_call(sinks[0], xm, norm_g[1, 1], wq, wo, cos_m, sin_m, km, vm, kt, vt)
    ym, yt, nk, nv = _ffn_call(xm, xt, norm_g[1, 2], w_in_11, w_out_11, name="ffn_l1b_final",
                               final_g=final_norm_g, roll=(new_k_t, new_v_t, ck, cv), tm=WIDE_TM)

    y_prompt = ym.reshape(BATCH, SEQ, D_MODEL)
    y_sample = yt[:N_SAMPLE].reshape(DEC_BATCH, DEC_SEQ, D_MODEL)
    new_state_conv_p = sp[:, CARRY - (CONV_W - 1):][None]
    new_state_conv_s = u_tail[:N_SAMPLE].reshape(DEC_BATCH, DEC_SEQ, D_MODEL)[:, DEC_SEQ - (CONV_W - 1):][None]
    slot_major = lambda a: a.transpose(0, 3, 1, 2)
    new_k_p = slot_major(k_last.reshape(BATCH, N_KV, HEAD_DIM, WINDOW))
    new_v_p = slot_major(v_last.reshape(BATCH, N_KV, HEAD_DIM, WINDOW))
    return (y_prompt, y_sample, new_state_conv_p, new_state_conv_s,
            new_k_p, new_v_p, slot_major(nk), slot_major(nv))
```

```python
import jax
import jax.numpy as jnp
from jax import lax
from jax.experimental import pallas as pl
from jax.experimental.pallas import tpu as pltpu

D_MODEL = 1024
BATCH = 2
SEQ = 8192
DEC_BATCH = 128
DEC_SEQ = 4
PAST_LEN = 8192
N_META = 16
D_FF = 2816
CONV_W = 3
HEAD_DIM = 64
N_HEADS = 16
N_KV = 4
GROUP = 4
WINDOW = 128
ROPE_THETA = 10000.0
EPS = 1e-6
NEG = -1e30
LOG2E = 1.4426950408889634

KV_W = N_KV * HEAD_DIM
HALF_DIM = HEAD_DIM // 2
N_MAIN = BATCH * SEQ
N_SAMPLE = DEC_BATCH * DEC_SEQ
N_TAIL = N_SAMPLE + N_META
TM = 512
N_TILES = N_MAIN // TM
TILES_PER_SEQ = SEQ // TM
WIDE_TM = 1024
FF_CHUNK = 256
OUT_CHUNK = 256
BLK = WINDOW
ATT_TM = 1024
SAMPLE_GROUP = 8
N_GROUPS = DEC_BATCH // SAMPLE_GROUP
GROUPS_PER_STEP = 2
GROUP_ROWS = SAMPLE_GROUP * DEC_SEQ
LANES = 128
BF16_ROWS = 16
V7X_VMEM_LIMIT = 56 * 1024 * 1024

F32 = jnp.float32
BF16 = jnp.bfloat16


def _rms(x, g):
    return x * lax.rsqrt(jnp.mean(x * x, axis=-1, keepdims=True) + EPS) * g


def _rope(x, cos, sin):
    outs = []
    for c in range(x.shape[1] // KV_W):
        x1 = x[:, c * KV_W:c * KV_W + LANES]
        x2 = x[:, c * KV_W + LANES:(c + 1) * KV_W]
        outs += [x1 * cos - x2 * sin, x2 * cos + x1 * sin]
    return jnp.concatenate(outs, axis=1)


def _rows_split_to_heads(t):
    return jnp.concatenate([t[half * LANES + h * HALF_DIM:half * LANES + (h + 1) * HALF_DIM]
                            for h in range(N_KV) for half in range(2)], axis=0)


def _rows_heads_to_split(t):
    return jnp.concatenate([t[h * HEAD_DIM + half * HALF_DIM:h * HEAD_DIM + (half + 1) * HALF_DIM]
                            for half in range(2) for h in range(N_KV)], axis=0)


def _dot(a, b):
    return jnp.dot(a, b, preferred_element_type=F32)


def _dot_nt(a, b):
    return lax.dot_general(a, b, (((1,), (1,)), ((), ())), preferred_element_type=F32)


def _resident(shape):
    nd = len(shape)
    return pl.BlockSpec(shape, lambda i: (0,) * nd, pipeline_mode=pl.Buffered(1))


def _main_tile(i):
    return jnp.maximum(i - 1, 0)


def _params():
    return pltpu.CompilerParams(dimension_semantics=("arbitrary",),
                                vmem_limit_bytes=V7X_VMEM_LIMIT)


def _ffn_rows(x_ref, o_ref, g, w_in_ref, w_out_ref, a_ref, final_g=None):
    x = x_ref[...]
    m = x.shape[0]
    hn = _rms(x, g).astype(BF16)
    for c in range(D_FF // FF_CHUNK):
        lo = c * FF_CHUNK
        a_ref[0:m, lo:lo + FF_CHUNK] = _swiglu_chunk(
            hn, w_in_ref[:, lo:lo + FF_CHUNK], w_in_ref[:, D_FF + lo:D_FF + lo + FF_CHUNK])
    _ffn_out(x, o_ref, a_ref, w_out_ref, final_g)


def _swiglu_chunk(hn, w_gate, w_up):
    gate = _dot(hn, w_gate)
    return (gate * jax.nn.sigmoid(gate) * _dot(hn, w_up)).astype(BF16)


def _ffn_out(x, o_ref, a_ref, w_out_ref, final_g):
    m = x.shape[0]
    sq = None
    for c in range(D_MODEL // OUT_CHUNK):
        cols = slice(c * OUT_CHUNK, (c + 1) * OUT_CHUNK)
        y = x[:, cols] + 0.5 * _dot(a_ref[0:m, :], w_out_ref[:, cols])
        o_ref[:, cols] = y
        if final_g is not None:
            part = jnp.sum(y * y, axis=-1, keepdims=True)
            sq = part if sq is None else sq + part
    if final_g is not None:
        o_ref[...] = o_ref[...] * lax.rsqrt(sq * (1.0 / D_MODEL) + EPS) * final_g


N_LOAD = 8


def _make_ffn_first_kernel(n_cast):
    in_rows, out_rows = D_MODEL // N_LOAD, D_FF // N_LOAD

    def body(*refs):
        xm_ref, xt_ref, g_ref, wg_ref, wu_ref, wo_ref = refs[:6]
        cast_src = refs[6:6 + n_cast]
        om_ref, ot_ref = refs[6 + n_cast:8 + n_cast]
        cast_dst = refs[8 + n_cast:8 + 2 * n_cast]
        a_ref, w_in_ref, w_out_ref = refs[8 + 2 * n_cast:]
        i = pl.program_id(0)

        for src, dst in zip(cast_src, cast_dst):
            dst[...] = src[...].astype(BF16)

        @pl.when(i < N_LOAD)
        def _():
            rows_in = pl.ds(pl.multiple_of(i * in_rows, in_rows), in_rows)
            w_in_ref[rows_in, 0:D_FF] = wg_ref[...].astype(BF16)
            w_in_ref[rows_in, D_FF:2 * D_FF] = wu_ref[...].astype(BF16)
            w_out_ref[pl.ds(pl.multiple_of(i * out_rows, out_rows), out_rows), :] = wo_ref[...].astype(BF16)

        @pl.when(i == N_LOAD)
        def _():
            _ffn_rows(xt_ref, ot_ref, g_ref[...], w_in_ref, w_out_ref, a_ref)

        @pl.when(i > N_LOAD)
        def _():
            _ffn_rows(xm_ref, om_ref, g_ref[...], w_in_ref, w_out_ref, a_ref)

    return body


def _ffn_first_call(xm, xt, g, ffn_w_in, ffn_w_out, lead, casts, *, name):
    slab = lambda i: jnp.minimum(i, N_LOAD - 1)
    nl = (None,) * len(lead)
    first_main = N_LOAD + 1
    main = pl.BlockSpec((TM, D_MODEL), lambda i: (jnp.maximum(i - first_main, 0), 0))
    c_in, c_out, c_shape, c_args = _cast_specs(casts, first_main)
    assert D_MODEL % (N_LOAD * BF16_ROWS) == 0 and D_FF % (N_LOAD * BF16_ROWS) == 0
    return pl.pallas_call(
        _make_ffn_first_kernel(len(casts)),
        grid=(first_main + N_TILES,),
        in_specs=[main, _resident((N_TAIL, D_MODEL)), _resident((1, D_MODEL)),
                  pl.BlockSpec(nl + (D_MODEL // N_LOAD, D_FF), lambda i: lead + (slab(i), 0)),
                  pl.BlockSpec(nl + (D_MODEL // N_LOAD, D_FF), lambda i: lead + (slab(i), 1)),
                  pl.BlockSpec(nl + (D_FF // N_LOAD, D_MODEL), lambda i: lead + (slab(i), 0))] + c_in,
        out_specs=[main, _resident((N_TAIL, D_MODEL))] + c_out,
        out_shape=[jax.ShapeDtypeStruct((N_MAIN, D_MODEL), F32),
                   jax.ShapeDtypeStruct((N_TAIL, D_MODEL), F32)] + c_shape,
        scratch_shapes=[pltpu.VMEM((N_TAIL, D_FF), BF16),
                        pltpu.VMEM((D_MODEL, 2 * D_FF), BF16), pltpu.VMEM((D_FF, D_MODEL), BF16)],
        compiler_params=_params(),
        name=name,
    )(xm, xt, g.reshape(1, D_MODEL), ffn_w_in, ffn_w_in, ffn_w_out, *c_args)


def _make_ffn_kernel(has_final, has_kv, has_roll, n_cast, tm):
    cache_seqs = DEC_BATCH * tm // N_MAIN

    def body(*refs):
        refs = list(refs)
        xm_ref, xt_ref, g_ref, w_in_ref, w_out_ref = refs[:5]
        pos = 5
        if has_final:
            gf_ref = refs[pos]
            pos += 1
        if has_kv:
            gkv_ref, wkv_ref, cm_ref, sm_ref, ct_ref, st_ref = refs[pos:pos + 6]
            pos += 6
        if has_roll:
            newk_ref, newv_ref, ck_ref, cv_ref = refs[pos:pos + 4]
            pos += 4
        cast_src = refs[pos:pos + n_cast]
        pos += n_cast
        om_ref, ot_ref = refs[pos:pos + 2]
        pos += 2
        if has_kv:
            km_ref, vm_ref, kt_ref, vt_ref, kl_ref, vl_ref, newk_ref, newv_ref = refs[pos:pos + 8]
            pos += 8
        if has_roll:
            nk_ref, nv_ref = refs[pos:pos + 2]
            pos += 2
        cast_dst = refs[pos:pos + n_cast]
        pos += n_cast
        a_ref = refs[pos]
        i = pl.program_id(0)

        for src, dst in zip(cast_src, cast_dst):
            dst[...] = src[...].astype(BF16)

        def rows(x_ref, o_ref):
            _ffn_rows(x_ref, o_ref, g_ref[...], w_in_ref, w_out_ref, a_ref,
                      gf_ref[...] if has_final else None)

        def project_kv(x_ref, cos_ref, sin_ref, k_ref, v_ref):
            kv = _dot(_rms(x_ref[...], gkv_ref[...]).astype(BF16), wkv_ref[...])
            k = _rope(kv[:, :KV_W], cos_ref[...], sin_ref[...])
            v = kv[:, KV_W:]
            k_ref[...] = k.astype(k_ref.dtype)
            v_ref[...] = v.astype(v_ref.dtype)
            return k, v

        @pl.when(i == 0)
        def _():
            if has_kv:
                k, v = project_kv(xt_ref, ct_ref, st_ref, kt_ref, vt_ref)
                for c in range(N_SAMPLE // LANES):
                    newk_ref[c] = _rows_split_to_heads(k[c * LANES:(c + 1) * LANES, :].T)
                    newv_ref[c] = v[c * LANES:(c + 1) * LANES, :].T
            rows(xt_ref, ot_ref)

        @pl.when(i > 0)
        def _():
            if has_kv:
                k, v = project_kv(xm_ref, cm_ref, sm_ref, km_ref, vm_ref)
                kl_ref[...] = _rows_split_to_heads(k[tm - WINDOW:, :].T)
                vl_ref[...] = v[tm - WINDOW:, :].T
            if has_roll:
                t = i - 1
                lane = lax.broadcasted_iota(jnp.int32, (1, WINDOW), 1)
                is_new = lane >= WINDOW - DEC_SEQ
                first_lane = (t * cache_seqs * DEC_SEQ) % LANES
                for new_ref, c_ref, n_ref in ((newk_ref, ck_ref, nk_ref), (newv_ref, cv_ref, nv_ref)):
                    new_t = new_ref[(t * cache_seqs * DEC_SEQ) // LANES]
                    for b in range(cache_seqs):
                        shifted = pltpu.roll(c_ref[b].reshape(KV_W, WINDOW), WINDOW - DEC_SEQ, 1)
                        amount = (WINDOW - DEC_SEQ - b * DEC_SEQ - first_lane) & (LANES - 1)
                        placed = pltpu.roll(new_t, amount, 1)
                        n_ref[b] = jnp.where(is_new, placed, shifted).reshape(N_KV, HEAD_DIM, WINDOW)
            rows(xm_ref, om_ref)

    return body


def _cast_specs(casts, first_main_step=1):
    in_specs, out_specs, out_shape, args = [], [], [], []
    for arr, lead, steps, src_block in casts:
        n_rows, n_cols = arr.shape[-2:]
        rows = n_rows // steps
        assert rows * steps == n_rows and rows % BF16_ROWS == 0
        blk = lambda i, steps=steps: jnp.clip(i - first_main_step, 0, steps - 1)
        src = blk if src_block is None else (lambda i, blk=blk, f=src_block: f(blk(i)))
        in_specs.append(pl.BlockSpec((None,) * len(lead) + (rows, n_cols),
                                     lambda i, lead=lead, src=src: lead + (src(i), 0)))
        out_specs.append(pl.BlockSpec((rows, n_cols), lambda i, blk=blk: (blk(i), 0)))
        out_shape.append(jax.ShapeDtypeStruct((n_rows, n_cols), BF16))
        args.append(arr)
    return in_specs, out_specs, out_shape, args


def _ffn_call(xm, xt, g, w_in, w_out, *, name, final_g=None, kv=None, roll=None, casts=(), tm=TM):
    new_t_shape = (N_SAMPLE // LANES, KV_W, LANES)
    n_tiles = N_MAIN // tm
    assert all(c[2] <= n_tiles for c in casts)
    main = pl.BlockSpec((tm, D_MODEL), lambda i: (_main_tile(i), 0))
    in_specs = [main, _resident((N_TAIL, D_MODEL)), _resident((1, D_MODEL)),
                _resident((D_MODEL, 2 * D_FF)), _resident((D_FF, D_MODEL))]
    args = [xm, xt, g.reshape(1, D_MODEL), w_in, w_out]
    out_shape = [jax.ShapeDtypeStruct((N_MAIN, D_MODEL), F32),
                 jax.ShapeDtypeStruct((N_TAIL, D_MODEL), F32)]
    out_specs = [main, _resident((N_TAIL, D_MODEL))]
    if final_g is not None:
        in_specs.append(_resident((1, D_MODEL)))
        args.append(final_g.reshape(1, D_MODEL))
    scratch = [pltpu.VMEM((max(N_TAIL, tm), D_FF), BF16)]
    if kv is not None:
        g_kv, w_kv, cos_m, sin_m, cos_t, sin_t = kv
        pos = pl.BlockSpec((tm, LANES), lambda i: (_main_tile(i) % (SEQ // tm), 0))
        kv_main = pl.BlockSpec((tm, KV_W), lambda i: (_main_tile(i), 0))
        in_specs += [_resident((1, D_MODEL)), _resident((D_MODEL, 2 * KV_W)), pos, pos,
                     _resident((N_TAIL, LANES)), _resident((N_TAIL, LANES))]
        args += [g_kv.reshape(1, D_MODEL), w_kv, cos_m, sin_m, cos_t, sin_t]
        out_shape += [jax.ShapeDtypeStruct((N_MAIN, KV_W), BF16)] * 2
        out_shape += [jax.ShapeDtypeStruct((N_TAIL, KV_W), F32)] * 2
        out_shape += [jax.ShapeDtypeStruct((BATCH, KV_W, WINDOW), F32)] * 2
        out_shape += [jax.ShapeDtypeStruct(new_t_shape, F32)] * 2
        last = pl.BlockSpec((None, KV_W, WINDOW), lambda i: (_main_tile(i) // (SEQ // tm), 0, 0))
        out_specs += [kv_main, kv_main, _resident((N_TAIL, KV_W)), _resident((N_TAIL, KV_W)),
                      last, last, _resident(new_t_shape), _resident(new_t_shape)]
    if roll is not None:
        cached = pl.BlockSpec((DEC_BATCH // n_tiles, N_KV, HEAD_DIM, WINDOW),
                              lambda i: (_main_tile(i), 0, 0, 0))
        in_specs += [_resident(new_t_shape), _resident(new_t_shape), cached, cached]
        args += list(roll)
        out_shape += [jax.ShapeDtypeStruct((DEC_BATCH, N_KV, HEAD_DIM, WINDOW), F32)] * 2
        out_specs += [cached, cached]
    c_in, c_out, c_shape, c_args = _cast_specs(casts)
    return pl.pallas_call(
        _make_ffn_kernel(final_g is not None, kv is not None, roll is not None, len(casts), tm),
        grid=(1 + n_tiles,),
        in_specs=in_specs + c_in,
        out_specs=out_specs + c_out,
        out_shape=out_shape + c_shape,
        scratch_shapes=scratch,
        input_output_aliases={0: 0},
        compiler_params=_params(),
        name=name,
    )(*args, *c_args)


CARRY = 8
CONV_TM = 1024


def _conv_rows(x_ref, o_ref, g, w_in_ref, kern_ref, w_out_ref, ubuf_ref, mbuf_ref, fix):
    x = x_ref[...]
    m = x.shape[0]
    hn = _rms(x, g).astype(BF16)
    for c in range(D_MODEL // OUT_CHUNK):
        cols = slice(c * OUT_CHUNK, (c + 1) * OUT_CHUNK)
        part = lambda j: _dot(hn, w_in_ref[:, j * D_MODEL + c * OUT_CHUNK:j * D_MODEL + (c + 1) * OUT_CHUNK])
        u = part(1) * part(2)
        ubuf_ref[CARRY:CARRY + m, cols] = u
        prev1 = ubuf_ref[CARRY - 1:CARRY - 1 + m, cols]
        prev2 = ubuf_ref[CARRY - 2:CARRY - 2 + m, cols]
        if fix is not None:
            prev1, prev2 = fix(prev1, prev2, cols)
        conv = kern_ref[0:1, cols] * prev2 + kern_ref[1:2, cols] * prev1 + kern_ref[2:3, cols] * u
        mbuf_ref[0:m, cols] = (part(0) * conv).astype(BF16)
    for c in range(D_MODEL // OUT_CHUNK):
        cols = slice(c * OUT_CHUNK, (c + 1) * OUT_CHUNK)
        o_ref[:, cols] = x[:, cols] + _dot(mbuf_ref[0:m, :], w_out_ref[:, cols])


def _conv_kernel(xm_ref, xt_ref, g_ref, w_in_ref, kern_ref, w_out_ref, s1_ref, s2_ref,
                 om_ref, ot_ref, ut_ref, sp_ref, ubuf_ref, meta_ref, mbuf_ref):
    i = pl.program_id(0)

    @pl.when(i == 0)
    def _():
        ubuf_ref[0:CARRY, :] = jnp.zeros((CARRY, D_MODEL), F32)

        def fix(prev1, prev2, cols):
            step = lax.broadcasted_iota(jnp.int32, (N_SAMPLE, 1), 0) & (DEC_SEQ - 1)
            meta = lax.broadcasted_iota(jnp.int32, (N_META, 1), 0)
            p1 = [jnp.where(step >= 1, prev1[:N_SAMPLE], s1_ref[:, cols]),
                  jnp.where(meta >= 1, prev1[N_SAMPLE:], 0.0)]
            p2 = [jnp.where(step >= 2, prev2[:N_SAMPLE], s2_ref[:, cols]),
                  jnp.where(meta >= 2, prev2[N_SAMPLE:], 0.0)]
            return jnp.concatenate(p1, axis=0), jnp.concatenate(p2, axis=0)

        _conv_rows(xt_ref, ot_ref, g_ref[...], w_in_ref, kern_ref, w_out_ref, ubuf_ref, mbuf_ref, fix)
        ut_ref[...] = ubuf_ref[CARRY:CARRY + N_TAIL, :]
        meta_ref[...] = ubuf_ref[N_TAIL:N_TAIL + CARRY, :]

    @pl.when(i > 0)
    def _():
        t = i - 1
        per_seq = SEQ // CONV_TM

        @pl.when(t % per_seq == 0)
        def _():
            ubuf_ref[0:CARRY, :] = meta_ref[...]

        _conv_rows(xm_ref, om_ref, g_ref[...], w_in_ref, kern_ref, w_out_ref, ubuf_ref, mbuf_ref, None)
        last = ubuf_ref[CONV_TM:CONV_TM + CARRY, :]
        ubuf_ref[0:CARRY, :] = last

        @pl.when(t % per_seq == per_seq - 1)
        def _():
            sp_ref[t // per_seq] = last


def _conv_call(xm, xt, g, w_in, kern, w_out, s1, s2):
    main = pl.BlockSpec((CONV_TM, D_MODEL), lambda i: (_main_tile(i), 0))
    return pl.pallas_call(
        _conv_kernel,
        grid=(1 + N_MAIN // CONV_TM,),
        in_specs=[main, _resident((N_TAIL, D_MODEL)), _resident((1, D_MODEL)),
                  _resident((D_MODEL, 3 * D_MODEL)), _resident((CONV_W, D_MODEL)),
                  _resident((D_MODEL, D_MODEL)),
                  _resident((N_SAMPLE, D_MODEL)), _resident((N_SAMPLE, D_MODEL))],
        out_specs=[main,
                   pl.BlockSpec((N_TAIL, D_MODEL), lambda i: (0, 0)),
                   pl.BlockSpec((N_TAIL, D_MODEL), lambda i: (0, 0)),
                   pl.BlockSpec((BATCH, CARRY, D_MODEL), lambda i: (0, 0, 0))],
        out_shape=[jax.ShapeDtypeStruct((N_MAIN, D_MODEL), F32),
                   jax.ShapeDtypeStruct((N_TAIL, D_MODEL), F32),
                   jax.ShapeDtypeStruct((N_TAIL, D_MODEL), F32),
                   jax.ShapeDtypeStruct((BATCH, CARRY, D_MODEL), F32)],
        scratch_shapes=[pltpu.VMEM((CARRY + max(N_TAIL, CONV_TM), D_MODEL), F32),
                        pltpu.VMEM((CARRY, D_MODEL), F32),
                        pltpu.VMEM((max(N_TAIL, CONV_TM), D_MODEL), BF16)],
        input_output_aliases={0: 0},
        compiler_params=_params(),
        name="conv_mixer",
    )(xm, xt, g.reshape(1, D_MODEL), w_in, kern, w_out, s1, s2)


def _attend(qp, segs, sinks_ref, heads_per_pass=1):
    mq = qp.shape[0]
    rows = GROUP * mq
    lane = lax.broadcasted_iota(jnp.int32, (1, KV_W), 1)
    qk_head = (lane & (LANES - 1)) // HALF_DIM
    pair_head = lax.broadcasted_iota(jnp.int32, (1, LANES), 1) // HEAD_DIM
    row_head = lax.broadcasted_iota(jnp.int32, (heads_per_pass * rows, 1), 0) // mq
    masks = [jnp.concatenate([seg[2]] * (heads_per_pass * GROUP), axis=0) for seg in segs]
    outs = []
    for k0 in range(0, N_KV, heads_per_pass):
        heads = range(k0, k0 + heads_per_pass)
        qs = jnp.concatenate(
            [jnp.where(qk_head == k, qp[:, g * KV_W:(g + 1) * KV_W], jnp.zeros((), BF16))
             for k in heads for g in range(GROUP)], axis=0)
        sink = jnp.full((heads_per_pass * rows, 1), sinks_ref[k0 * GROUP] * LOG2E, F32)
        for j in range(1, heads_per_pass * GROUP):
            sink = jnp.where(row_head == j, sinks_ref[k0 * GROUP + j] * LOG2E, sink)
        scores = []
        for (kb, _, _, k_transposed), msk in zip(segs, masks):
            s = _dot(qs, kb) if k_transposed else _dot_nt(qs, kb)
            scores.append(jnp.where(msk, s, NEG))
        mx = sink
        for s in scores:
            mx = jnp.maximum(mx, jnp.max(s, axis=-1, keepdims=True))
        denom = jnp.exp2(sink - mx)
        probs = []
        for s in scores:
            p = jnp.exp2(s - mx)
            denom = denom + jnp.sum(p, axis=-1, keepdims=True)
            probs.append(p.astype(BF16))
        inv = 1.0 / denom
        for n, k in enumerate(heads):
            pair = slice((k // 2) * LANES, (k // 2 + 1) * LANES)
            part = slice(n * rows, (n + 1) * rows)
            acc = None
            for p, (_, vb, _, _) in zip(probs, segs):
                pv = _dot(p[part], jnp.where(pair_head == k % 2, vb[:, pair], jnp.zeros((), BF16)))
                acc = pv if acc is None else acc + pv
            outs.append(acc * inv[part])
    out = jnp.concatenate([outs[k] + outs[k + 1] for k in range(0, N_KV, 2)], axis=1)
    return jnp.concatenate([out[g * mq:(g + 1) * mq, :] for g in range(GROUP)], axis=1)


def _queries(x, g, wq_ref, cos, sin):
    q = _dot(_rms(x, g).astype(BF16), wq_ref[...])
    return (_rope(q, cos, sin) * (HEAD_DIM ** -0.5 * LOG2E)).astype(BF16)


def _attn_main_kernel(sinks_ref, xm_ref, g_ref, wq_ref, wo_ref, cos_ref, sin_ref,
                      km_ref, vm_ref, kt_ref, vt_ref,
                      om_ref, q_ref, ao_ref, kbuf_ref, vbuf_ref):
    i = pl.program_id(0)
    start = i % (SEQ // ATT_TM) == 0

    x = xm_ref[...]
    q_ref[...] = _queries(x, g_ref[...], wq_ref, cos_ref[...], sin_ref[...])

    @pl.when(start)
    def _():
        pad = jnp.zeros((BLK - N_META, KV_W), BF16)
        kbuf_ref[0:BLK - N_META, :] = pad
        vbuf_ref[0:BLK - N_META, :] = pad
        kbuf_ref[BLK - N_META:BLK, :] = kt_ref[N_SAMPLE:N_TAIL, :].astype(BF16)
        vbuf_ref[BLK - N_META:BLK, :] = vt_ref[N_SAMPLE:N_TAIL, :].astype(BF16)

    @pl.when(jnp.logical_not(start))
    def _():
        kbuf_ref[0:BLK, :] = kbuf_ref[ATT_TM:ATT_TM + BLK, :]
        vbuf_ref[0:BLK, :] = vbuf_ref[ATT_TM:ATT_TM + BLK, :]

    kbuf_ref[BLK:BLK + ATT_TM, :] = km_ref[...]
    vbuf_ref[BLK:BLK + ATT_TM, :] = vm_ref[...]

    qi = lax.broadcasted_iota(jnp.int32, (BLK, 2 * BLK), 0)
    kj = lax.broadcasted_iota(jnp.int32, (BLK, 2 * BLK), 1)
    band = (kj >= qi) & (kj <= qi + WINDOW)
    first_key = jnp.where(start, BLK - N_META, 0)
    for blk in range(ATT_TM // BLK):
        lo = blk * BLK
        mask = band & (kj >= first_key) if blk == 0 else band
        att = _attend(q_ref[lo:lo + BLK, :],
                      [(kbuf_ref[lo:lo + 2 * BLK, :], vbuf_ref[lo:lo + 2 * BLK, :], mask, False)],
                      sinks_ref)
        ao_ref[lo:lo + BLK, :] = att.astype(BF16)
    om_ref[...] = x + _dot(ao_ref[...], wo_ref[...])


def _attn_main_call(sinks, xm, g, wq, wo, cos_m, sin_m, km, vm, kt, vt):
    tile = lambda w: pl.BlockSpec((ATT_TM, w), lambda i: (i, 0))
    pos = pl.BlockSpec((ATT_TM, LANES), lambda i: (i % (SEQ // ATT_TM), 0))
    return pl.pallas_call(
        _attn_main_kernel,
        grid=(N_MAIN // ATT_TM,),
        in_specs=[pl.BlockSpec(memory_space=pltpu.SMEM),
                  tile(D_MODEL), _resident((1, D_MODEL)),
                  _resident((D_MODEL, D_MODEL)), _resident((D_MODEL, D_MODEL)),
                  pos, pos, tile(KV_W), tile(KV_W),
                  _resident((N_TAIL, KV_W)), _resident((N_TAIL, KV_W))],
        out_specs=tile(D_MODEL),
        out_shape=jax.ShapeDtypeStruct((N_MAIN, D_MODEL), F32),
        scratch_shapes=[pltpu.VMEM((ATT_TM, D_MODEL), BF16), pltpu.VMEM((ATT_TM, D_MODEL), BF16),
                        pltpu.VMEM((BLK + ATT_TM, KV_W), BF16),
                        pltpu.VMEM((BLK + ATT_TM, KV_W), BF16)],
        input_output_aliases={1: 0},
        compiler_params=_params(),
        name="attn_main",
    )(sinks, xm, g.reshape(1, D_MODEL), wq, wo, cos_m, sin_m, km, vm, kt, vt)


def _attn_tail_kernel(sinks_ref, xt_ref, g_ref, wq_ref, wo_ref, cos_ref, sin_ref,
                      kt_ref, vt_ref, ck_ref, cv_ref, ot_ref, q_ref, ao_ref):
    s = pl.program_id(0)

    @pl.when(s == 0)
    def _():
        q_ref[...] = _queries(xt_ref[...], g_ref[...], wq_ref, cos_ref[...], sin_ref[...])
        qi = lax.broadcasted_iota(jnp.int32, (N_META, N_META), 0)
        kj = lax.broadcasted_iota(jnp.int32, (N_META, N_META), 1)
        km = kt_ref[N_SAMPLE:N_TAIL, :].astype(BF16)
        vm = vt_ref[N_SAMPLE:N_TAIL, :].astype(BF16)
        att = _attend(q_ref[N_SAMPLE:N_TAIL, :], [(km, vm, kj <= qi, False)], sinks_ref, N_KV)
        ao_ref[N_SAMPLE:N_TAIL, :] = att.astype(BF16)

    n_cached = SAMPLE_GROUP * WINDOW
    qr = lax.broadcasted_iota(jnp.int32, (GROUP_ROWS, n_cached), 0)
    kc = lax.broadcasted_iota(jnp.int32, (GROUP_ROWS, n_cached), 1)
    mask_cached = ((kc // WINDOW) == (qr // DEC_SEQ)) & ((kc % WINDOW) >= (qr % DEC_SEQ))
    qn = lax.broadcasted_iota(jnp.int32, (GROUP_ROWS, GROUP_ROWS), 0)
    kn = lax.broadcasted_iota(jnp.int32, (GROUP_ROWS, GROUP_ROWS), 1)
    mask_new = ((kn // DEC_SEQ) == (qn // DEC_SEQ)) & ((kn % DEC_SEQ) <= (qn % DEC_SEQ))
    for grp in range(GROUPS_PER_STEP):
        first = (s * GROUPS_PER_STEP + grp) * GROUP_ROWS
        rows = pl.ds(pl.multiple_of(first, GROUP_ROWS), GROUP_ROWS)
        k_new = kt_ref[rows, :]
        v_new = vt_ref[rows, :]
        seqs = range(grp * SAMPLE_GROUP, (grp + 1) * SAMPLE_GROUP)
        k_cached_t = jnp.concatenate(
            [_rows_heads_to_split(ck_ref[b].reshape(KV_W, WINDOW)) for b in seqs], axis=1)
        v_cached = jnp.concatenate([cv_ref[b].reshape(KV_W, WINDOW).T for b in seqs], axis=0)
        att = _attend(q_ref[rows, :],
                      [(k_cached_t.astype(BF16), v_cached.astype(BF16), mask_cached, True),
                       (k_new.astype(BF16), v_new.astype(BF16), mask_new, False)],
                      sinks_ref, N_KV)
        ao_ref[rows, :] = att.astype(BF16)

    @pl.when(s == N_GROUPS // GROUPS_PER_STEP - 1)
    def _():
        ot_ref[...] = xt_ref[...] + _dot(ao_ref[...], wo_ref[...])


def _attn_tail_call(sinks, xt, g, wq, wo, cos_t, sin_t, kt, vt, ck, cv):
    cached = pl.BlockSpec((GROUPS_PER_STEP * SAMPLE_GROUP, N_KV, HEAD_DIM, WINDOW),
                          lambda s: (s, 0, 0, 0))
    return pl.pallas_call(
        _attn_tail_kernel,
        grid=(N_GROUPS // GROUPS_PER_STEP,),
        in_specs=[pl.BlockSpec(memory_space=pltpu.SMEM),
                  _resident((N_TAIL, D_MODEL)), _resident((1, D_MODEL)),
                  _resident((D_MODEL, D_MODEL)), _resident((D_MODEL, D_MODEL)),
                  _resident((N_TAIL, LANES)), _resident((N_TAIL, LANES)),
                  _resident((N_TAIL, KV_W)), _resident((N_TAIL, KV_W)),
                  cached, cached],
        out_specs=pl.BlockSpec((N_TAIL, D_MODEL), lambda s: (0, 0)),
        out_shape=jax.ShapeDtypeStruct((N_TAIL, D_MODEL), F32),
        scratch_shapes=[pltpu.VMEM((N_TAIL, D_MODEL), BF16), pltpu.VMEM((N_TAIL, D_MODEL), BF16)],
        compiler_params=_params(),
        name="attn_tail",
    )(sinks, xt, g.reshape(1, D_MODEL), wq, wo, cos_t, sin_t, kt, vt, ck, cv)


def _rope_tables(pos):
    inv = 1.0 / (ROPE_THETA ** (jnp.arange(0, HEAD_DIM, 2, dtype=F32) / HEAD_DIM))
    ang = pos.astype(F32)[:, None] * inv[None, :]
    cos, sin = jnp.cos(ang), jnp.sin(ang)
    return jnp.tile(cos, (1, N_KV)), jnp.tile(sin, (1, N_KV))


def kernel(x_prompt, x_sample, state_conv, cache_k, cache_v, meta_tokens, norm_g, ffn_w_in,
           ffn_w_out, conv_w_in, conv_kernel, conv_w_out, kv_norm_g, w_kv, w_q, w_o, sinks,
           final_norm_g):
    assert x_prompt.shape == (BATCH, SEQ, D_MODEL) and x_sample.shape == (DEC_BATCH, DEC_SEQ, D_MODEL)
    assert cache_k.shape == (DEC_BATCH, WINDOW, N_KV, HEAD_DIM)
    assert norm_g.shape[0] == 2 and conv_w_in.shape[0] == 1 and w_q.shape[0] == 1

    xm = x_prompt.reshape(N_MAIN, D_MODEL)
    xt = jnp.concatenate([x_sample.reshape(N_SAMPLE, D_MODEL), meta_tokens], axis=0)

    wq = w_q[0].reshape(D_MODEL, N_KV, GROUP, 2, HALF_DIM).transpose(0, 2, 3, 1, 4)
    wq = wq.reshape(D_MODEL, D_MODEL).astype(BF16)
    wk =w_kv[:, :KV_W].reshape(D_MODEL, N_KV, 2, HALF_DIM).transpose(0, 2, 1, 3)
    wkv = jnp.concatenate([wk.reshape(D_MODEL, KV_W), w_kv[:, KV_W:]], axis=1).astype(BF16)

    pos_main = N_META + jnp.arange(SEQ, dtype=jnp.int32)
    pos_tail = jnp.concatenate([
        jnp.tile(PAST_LEN + jnp.arange(DEC_SEQ, dtype=jnp.int32), DEC_BATCH),
        jnp.arange(N_META, dtype=jnp.int32)])
    cos_m, sin_m = _rope_tables(pos_main)
    cos_t, sin_t = _rope_tables(pos_tail)

    sc = state_conv[0]
    grow = lambda a: jnp.pad(a, ((0, 0), (0, DEC_SEQ - a.shape[1]), (0, 0))).reshape(N_SAMPLE, D_MODEL)
    s1 = grow(sc[:, 1:])
    s2 = grow(sc)

    ffn_casts = lambda l, j, steps=N_TILES: [(ffn_w_in, (l, j), steps, None),
                                             (ffn_w_out, (l, j), steps // 2, None)]
    wo_cast = (w_o, (0,), N_HEADS, lambda b: (b % N_KV) * GROUP + b // N_KV)
    slot_minor = lambda a: a.transpose(0, 2, 3, 1)
    ck, cv = slot_minor(cache_k), slot_minor(cache_v)

    xm, xt, cw_in, cw_out, w_in, w_out, w_in_10, w_out_10 = _ffn_first_call(
        xm, xt, norm_g[0, 0], ffn_w_in, ffn_w_out, (0, 0),
        [(conv_w_in, (0,), N_TILES, None), (conv_w_out, (0,), N_TILES, None)]
        + ffn_casts(0, 1) + ffn_casts(1, 0), name="ffn_l0a")
    xm, xt, u_tail, sp = _conv_call(xm, xt, norm_g[0, 1], cw_in, conv_kernel[0], cw_out, s1, s2)
    xm, xt, wo, w_in_11, w_out_11 = _ffn_call(
        xm, xt, norm_g[0, 2], w_in, w_out, name="ffn_l0b", tm=WIDE_TM,
        casts=[wo_cast] + ffn_casts(1, 1, N_MAIN // WIDE_TM))

    xm, xt, km, vm, kt, vt, k_last, v_last, new_k_t, new_v_t = _ffn_call(
        xm, xt, norm_g[1, 0], w_in_10, w_out_10, name="ffn_l1a_kv", tm=WIDE_TM,
        kv=(kv_norm_g, wkv, cos_m, sin_m, cos_t, sin_t))
    xt = _attn_tail_call(sinks[0], xt, norm_g[1, 1], wq, wo, cos_t, sin_t, kt, vt, ck, cv)
    xm = _attn_main_call(sinks[0], xm, norm_g[1, 1], wq, wo, cos_m, sin_m, km, vm, kt, vt)
    ym, yt, nk, nv = _ffn_call(xm, xt, norm_g[1, 2], w_in_11, w_out_11, name="ffn_l1b_final",
                               final_g=final_norm_g, roll=(new_k_t, new_v_t, ck, cv), tm=WIDE_TM)

    y_prompt = ym.reshape(BATCH, SEQ, D_MODEL)
    y_sample = yt[:N_SAMPLE].reshape(DEC_BATCH, DEC_SEQ, D_MODEL)
    new_state_conv_p = sp[:, CARRY - (CONV_W - 1):][None]
    new_state_conv_s = u_tail[:N_SAMPLE].reshape(DEC_BATCH, DEC_SEQ, D_MODEL)[:, DEC_SEQ - (CONV_W - 1):][None]
    slot_major = lambda a: a.transpose(0, 3, 1, 2)
    new_k_p = slot_major(k_last.reshape(BATCH, N_KV, HEAD_DIM, WINDOW))
    new_v_p = slot_major(v_last.reshape(BATCH, N_KV, HEAD_DIM, WINDOW))
    return (y_prompt, y_sample, new_state_conv_p, new_state_conv_s,
            new_k_p, new_v_p, slot_major(nk), slot_major(nv))
```

```python
import jax
import jax.numpy as jnp
from jax import lax
from jax.experimental import pallas as pl
from jax.experimental.pallas import tpu as pltpu

D_MODEL = 1024
BATCH = 2
SEQ = 8192
DEC_BATCH = 128
DEC_SEQ = 4
PAST_LEN = 8192
N_META = 16
D_FF = 2816
CONV_W = 3
HEAD_DIM = 64
N_HEADS = 16
N_KV = 4
GROUP = 4
WINDOW = 128
ROPE_THETA = 10000.0
EPS = 1e-6
NEG = -1e30
LOG2E = 1.4426950408889634

KV_W = N_KV * HEAD_DIM
HALF_DIM = HEAD_DIM // 2
N_MAIN = BATCH * SEQ
N_SAMPLE = DEC_BATCH * DEC_SEQ
N_TAIL = N_SAMPLE + N_META
TM = 512
N_TILES = N_MAIN // TM
TILES_PER_SEQ = SEQ // TM
WIDE_TM = 1024
FF_CHUNK = 256
OUT_CHUNK = 256
BLK = WINDOW
ATT_TM = 1024
SAMPLE_GROUP = 8
N_GROUPS = DEC_BATCH // SAMPLE_GROUP
GROUPS_PER_STEP = 2
GROUP_ROWS = SAMPLE_GROUP * DEC_SEQ
LANES = 128
BF16_ROWS = 16
V7X_VMEM_LIMIT = 56 * 1024 * 1024

F32 = jnp.float32
BF16 = jnp.bfloat16


def _rms(x, g):
    return x * lax.rsqrt(jnp.mean(x * x, axis=-1, keepdims=True) + EPS) * g


def _rope(x, cos, sin):
    outs = []
    for c in range(x.shape[1] // KV_W):
        x1 = x[:, c * KV_W:c * KV_W + LANES]
        x2 = x[:, c * KV_W + LANES:(c + 1) * KV_W]
        outs += [x1 * cos - x2 * sin, x2 * cos + x1 * sin]
    return jnp.concatenate(outs, axis=1)


def _rows_split_to_heads(t):
    return jnp.concatenate([t[half * LANES + h * HALF_DIM:half * LANES + (h + 1) * HALF_DIM]
                            for h in range(N_KV) for half in range(2)], axis=0)


def _rows_heads_to_split(t):
    return jnp.concatenate([t[h * HEAD_DIM + half * HALF_DIM:h * HEAD_DIM + (half + 1) * HALF_DIM]
                            for half in range(2) for h in range(N_KV)], axis=0)


def _dot(a, b):
    return jnp.dot(a, b, preferred_element_type=F32)


def _dot_nt(a, b):
    return lax.dot_general(a, b, (((1,), (1,)), ((), ())), preferred_element_type=F32)


def _resident(shape):
    nd = len(shape)
    return pl.BlockSpec(shape, lambda i: (0,) * nd, pipeline_mode=pl.Buffered(1))


def _main_tile(i):
    return jnp.maximum(i - 1, 0)


def _params():
    return pltpu.CompilerParams(dimension_semantics=("arbitrary",),
                                vmem_limit_bytes=V7X_VMEM_LIMIT)


def _ffn_rows(x_ref, o_ref, g, w_in_ref, w_out_ref, a_ref, final_g=None):
    x = x_ref[...]
    m = x.shape[0]
    hn = _rms(x, g).astype(BF16)
    for c in range(D_FF // FF_CHUNK):
        lo = c * FF_CHUNK
        a_ref[0:m, lo:lo + FF_CHUNK] = _swiglu_chunk(
            hn, w_in_ref[:, lo:lo + FF_CHUNK], w_in_ref[:, D_FF + lo:D_FF + lo + FF_CHUNK])
    _ffn_out(x, o_ref, a_ref, w_out_ref, final_g)


def _swiglu_chunk(hn, w_gate, w_up):
    gate = _dot(hn, w_gate)
    return (gate * jax.nn.sigmoid(gate) * _dot(hn, w_up)).astype(BF16)


def _ffn_out(x, o_ref, a_ref, w_out_ref, final_g):
    m = x.shape[0]
    sq = None
    for c in range(D_MODEL // OUT_CHUNK):
        cols = slice(c * OUT_CHUNK, (c + 1) * OUT_CHUNK)
        y = x[:, cols] + 0.5 * _dot(a_ref[0:m, :], w_out_ref[:, cols])
        o_ref[:, cols] = y
        if final_g is not None:
            part = jnp.sum(y * y, axis=-1, keepdims=True)
            sq = part if sq is None else sq + part
    if final_g is not None:
        o_ref[...] = o_ref[...] * lax.rsqrt(sq * (1.0 / D_MODEL) + EPS) * final_g


N_LOAD = 8


def _make_ffn_first_kernel(n_cast):
    in_rows, out_rows = D_MODEL // N_LOAD, D_FF // N_LOAD

    def body(*refs):
        xm_ref, xt_ref, g_ref, wg_ref, wu_ref, wo_ref, cr_ref, sr_ref, ct_ref, st_ref = refs[:10]
        cast_src = refs[10:10 + n_cast]
        om_ref, ot_ref, cos_ref, sin_ref = refs[10 + n_cast:14 + n_cast]
        cast_dst = refs[14 + n_cast:14 + 2 * n_cast]
        a_ref, w_in_ref, w_out_ref = refs[14 + 2 * n_cast:]
        i = pl.program_id(0)

        for src, dst in zip(cast_src, cast_dst):
            dst[...] = src[...].astype(BF16)

        cos_ref[...] = cr_ref[...] * ct_ref[...] - sr_ref[...] * st_ref[...]
        sin_ref[...] = sr_ref[...] * ct_ref[...] + cr_ref[...] * st_ref[...]

        @pl.when(i < N_LOAD)
        def _():
            rows_in = pl.ds(pl.multiple_of(i * in_rows, in_rows), in_rows)
            w_in_ref[rows_in, 0:D_FF] = wg_ref[...].astype(BF16)
            w_in_ref[rows_in, D_FF:2 * D_FF] = wu_ref[...].astype(BF16)
            w_out_ref[pl.ds(pl.multiple_of(i * out_rows, out_rows), out_rows), :] = wo_ref[...].astype(BF16)

        @pl.when(i == N_LOAD)
        def _():
            _ffn_rows(xt_ref, ot_ref, g_ref[...], w_in_ref, w_out_ref, a_ref)

        @pl.when(i > N_LOAD)
        def _():
            _ffn_rows(xm_ref, om_ref, g_ref[...], w_in_ref, w_out_ref, a_ref)

    return body


def _ffn_first_call(xm, xt, g, ffn_w_in, ffn_w_out, lead, casts, rope_seeds, *, name):
    slab = lambda i: jnp.minimum(i, N_LOAD - 1)
    nl = (None,) * len(lead)
    first_main = N_LOAD + 1
    main = pl.BlockSpec((TM, D_MODEL), lambda i: (jnp.maximum(i - first_main, 0), 0))
    seq_tile = lambda i: jnp.clip(i - first_main, 0, TILES_PER_SEQ - 1)
    tile_start = pl.BlockSpec((None, 1, LANES), lambda i: (seq_tile(i), 0, 0))
    table = pl.BlockSpec((TM, LANES), lambda i: (seq_tile(i), 0))
    c_in, c_out, c_shape, c_args = _cast_specs(casts, first_main)
    assert D_MODEL % (N_LOAD * BF16_ROWS) == 0 and D_FF % (N_LOAD * BF16_ROWS) == 0
    return pl.pallas_call(
        _make_ffn_first_kernel(len(casts)),
        grid=(first_main + N_TILES,),
        in_specs=[main, _resident((N_TAIL, D_MODEL)), _resident((1, D_MODEL)),
                  pl.BlockSpec(nl + (D_MODEL // N_LOAD, D_FF), lambda i: lead + (slab(i), 0)),
                  pl.BlockSpec(nl + (D_MODEL // N_LOAD, D_FF), lambda i: lead + (slab(i), 1)),
                  pl.BlockSpec(nl + (D_FF // N_LOAD, D_MODEL), lambda i: lead + (slab(i), 0)),
                  _resident((TM, LANES)), _resident((TM, LANES)), tile_start, tile_start] + c_in,
        out_specs=[main, _resident((N_TAIL, D_MODEL)), table, table] + c_out,
        out_shape=[jax.ShapeDtypeStruct((N_MAIN, D_MODEL), F32),
                   jax.ShapeDtypeStruct((N_TAIL, D_MODEL), F32),
                   jax.ShapeDtypeStruct((SEQ, LANES), F32),
                   jax.ShapeDtypeStruct((SEQ, LANES), F32)] + c_shape,
        scratch_shapes=[pltpu.VMEM((N_TAIL, D_FF), BF16),
                        pltpu.VMEM((D_MODEL, 2 * D_FF), BF16), pltpu.VMEM((D_FF, D_MODEL), BF16)],
        compiler_params=_params(),
        name=name,
    )(xm, xt, g.reshape(1, D_MODEL), ffn_w_in, ffn_w_in, ffn_w_out, *rope_seeds, *c_args)


def _make_ffn_kernel(has_final, has_kv, has_roll, n_cast, tm):
    cache_seqs = DEC_BATCH * tm // N_MAIN

    def body(*refs):
        refs = list(refs)
        xm_ref, xt_ref, g_ref, w_in_ref, w_out_ref = refs[:5]
        pos = 5
        if has_final:
            gf_ref = refs[pos]
            pos += 1
        if has_kv:
            gkv_ref, wkv_ref, cm_ref, sm_ref, ct_ref, st_ref = refs[pos:pos + 6]
            pos += 6
        if has_roll:
            newk_ref, newv_ref, ck_ref, cv_ref = refs[pos:pos + 4]
            pos += 4
        cast_src = refs[pos:pos + n_cast]
        pos += n_cast
        om_ref, ot_ref = refs[pos:pos + 2]
        pos += 2
        if has_kv:
            km_ref, vm_ref, kt_ref, vt_ref, kl_ref, vl_ref, newk_ref, newv_ref = refs[pos:pos + 8]
            pos += 8
        if has_roll:
            nk_ref, nv_ref = refs[pos:pos + 2]
            pos += 2
        cast_dst = refs[pos:pos + n_cast]
        pos += n_cast
        a_ref = refs[pos]
        i = pl.program_id(0)

        for src, dst in zip(cast_src, cast_dst):
            dst[...] = src[...].astype(BF16)

        def rows(x_ref, o_ref):
            _ffn_rows(x_ref, o_ref, g_ref[...], w_in_ref, w_out_ref, a_ref,
                      gf_ref[...] if has_final else None)

        def project_kv(x_ref, cos_ref, sin_ref, k_ref, v_ref):
            kv = _dot(_rms(x_ref[...], gkv_ref[...]).astype(BF16), wkv_ref[...])
            k = _rope(kv[:, :KV_W], cos_ref[...], sin_ref[...])
            v = kv[:, KV_W:]
            k_ref[...] = k.astype(k_ref.dtype)
            v_ref[...] = v.astype(v_ref.dtype)
            return k, v

        @pl.when(i == 0)
        def _():
            if has_kv:
                k, v = project_kv(xt_ref, ct_ref, st_ref, kt_ref, vt_ref)
                for c in range(N_SAMPLE // LANES):
                    newk_ref[c] = _rows_split_to_heads(k[c * LANES:(c + 1) * LANES, :].T)
                    newv_ref[c] = v[c * LANES:(c + 1) * LANES, :].T
            rows(xt_ref, ot_ref)

        @pl.when(i > 0)
        def _():
            if has_kv:
                k, v = project_kv(xm_ref, cm_ref, sm_ref, km_ref, vm_ref)
                kl_ref[...] = _rows_split_to_heads(k[tm - WINDOW:, :].T)
                vl_ref[...] = v[tm - WINDOW:, :].T
            if has_roll:
                t = i - 1
                lane = lax.broadcasted_iota(jnp.int32, (1, WINDOW), 1)
                is_new = lane >= WINDOW - DEC_SEQ
                first_lane = (t * cache_seqs * DEC_SEQ) % LANES
                for new_ref, c_ref, n_ref in ((newk_ref, ck_ref, nk_ref), (newv_ref, cv_ref, nv_ref)):
                    new_t = new_ref[(t * cache_seqs * DEC_SEQ) // LANES]
                    for b in range(cache_seqs):
                        shifted = pltpu.roll(c_ref[b].reshape(KV_W, WINDOW), WINDOW - DEC_SEQ, 1)
                        amount = (WINDOW - DEC_SEQ - b * DEC_SEQ - first_lane) & (LANES - 1)
                        placed = pltpu.roll(new_t, amount, 1)
                        n_ref[b] = jnp.where(is_new, placed, shifted).reshape(N_KV, HEAD_DIM, WINDOW)
            rows(xm_ref, om_ref)

    return body


def _cast_specs(casts, first_main_step=1):
    in_specs, out_specs, out_shape, args = [], [], [], []
    for arr, lead, steps, src_block in casts:
        n_rows, n_cols = arr.shape[-2:]
        rows = n_rows // steps
        assert rows * steps == n_rows and rows % BF16_ROWS == 0
        blk = lambda i, steps=steps: jnp.clip(i - first_main_step, 0, steps - 1)
        src = blk if src_block is None else (lambda i, blk=blk, f=src_block: f(blk(i)))
        in_specs.append(pl.BlockSpec((None,) * len(lead) + (rows, n_cols),
                                     lambda i, lead=lead, src=src: lead + (src(i), 0)))
        out_specs.append(pl.BlockSpec((rows, n_cols), lambda i, blk=blk: (blk(i), 0)))
        out_shape.append(jax.ShapeDtypeStruct((n_rows, n_cols), BF16))
        args.append(arr)
    return in_specs, out_specs, out_shape, args


def _ffn_call(xm, xt, g, w_in, w_out, *, name, final_g=None, kv=None, roll=None, casts=(), tm=TM):
    new_t_shape = (N_SAMPLE // LANES, KV_W, LANES)
    n_tiles = N_MAIN // tm
    assert all(c[2] <= n_tiles for c in casts)
    main = pl.BlockSpec((tm, D_MODEL), lambda i: (_main_tile(i), 0))
    in_specs = [main, _resident((N_TAIL, D_MODEL)), _resident((1, D_MODEL)),
                _resident((D_MODEL, 2 * D_FF)), _resident((D_FF, D_MODEL))]
    args = [xm, xt, g.reshape(1, D_MODEL), w_in, w_out]
    out_shape = [jax.ShapeDtypeStruct((N_MAIN, D_MODEL), F32),
                 jax.ShapeDtypeStruct((N_TAIL, D_MODEL), F32)]
    out_specs = [main, _resident((N_TAIL, D_MODEL))]
    if final_g is not None:
        in_specs.append(_resident((1, D_MODEL)))
        args.append(final_g.reshape(1, D_MODEL))
    scratch = [pltpu.VMEM((max(N_TAIL, tm), D_FF), BF16)]
    if kv is not None:
        g_kv, w_kv, cos_m, sin_m, cos_t, sin_t = kv
        pos = pl.BlockSpec((tm, LANES), lambda i: (_main_tile(i) % (SEQ // tm), 0))
        kv_main = pl.BlockSpec((tm, KV_W), lambda i: (_main_tile(i), 0))
        in_specs += [_resident((1, D_MODEL)), _resident((D_MODEL, 2 * KV_W)), pos, pos,
                     _resident((N_TAIL, LANES)), _resident((N_TAIL, LANES))]
        args += [g_kv.reshape(1, D_MODEL), w_kv, cos_m, sin_m, cos_t, sin_t]
        out_shape += [jax.ShapeDtypeStruct((N_MAIN, KV_W), BF16)] * 2
        out_shape += [jax.ShapeDtypeStruct((N_TAIL, KV_W), F32)] * 2
        out_shape += [jax.ShapeDtypeStruct((BATCH, KV_W, WINDOW), F32)] * 2
        out_shape += [jax.ShapeDtypeStruct(new_t_shape, F32)] * 2
        last = pl.BlockSpec((None, KV_W, WINDOW), lambda i: (_main_tile(i) // (SEQ // tm), 0, 0))
        out_specs += [kv_main, kv_main, _resident((N_TAIL, KV_W)), _resident((N_TAIL, KV_W)),
                      last, last, _resident(new_t_shape), _resident(new_t_shape)]
    if roll is not None:
        cached = pl.BlockSpec((DEC_BATCH // n_tiles, N_KV, HEAD_DIM, WINDOW),
                              lambda i: (_main_tile(i), 0, 0, 0))
        in_specs += [_resident(new_t_shape), _resident(new_t_shape), cached, cached]
        args += list(roll)
        out_shape += [jax.ShapeDtypeStruct((DEC_BATCH, N_KV, HEAD_DIM, WINDOW), F32)] * 2
        out_specs += [cached, cached]
    c_in, c_out, c_shape, c_args = _cast_specs(casts)
    return pl.pallas_call(
        _make_ffn_kernel(final_g is not None, kv is not None, roll is not None, len(casts), tm),
        grid=(1 + n_tiles,),
        in_specs=in_specs + c_in,
        out_specs=out_specs + c_out,
        out_shape=out_shape + c_shape,
        scratch_shapes=scratch,
        input_output_aliases={0: 0},
        compiler_params=_params(),
        name=name,
    )(*args, *c_args)


CARRY = 8
CONV_TM = 1024


def _conv_rows(x_ref, o_ref, g, w_in_ref, kern_ref, w_out_ref, ubuf_ref, mbuf_ref, fix):
    x = x_ref[...]
    m = x.shape[0]
    hn = _rms(x, g).astype(BF16)
    for c in range(D_MODEL // OUT_CHUNK):
        cols = slice(c * OUT_CHUNK, (c + 1) * OUT_CHUNK)
        part = lambda j: _dot(hn, w_in_ref[:, j * D_MODEL + c * OUT_CHUNK:j * D_MODEL + (c + 1) * OUT_CHUNK])
        u = part(1) * part(2)
        ubuf_ref[CARRY:CARRY + m, cols] = u
        prev1 = ubuf_ref[CARRY - 1:CARRY - 1 + m, cols]
        prev2 = ubuf_ref[CARRY - 2:CARRY - 2 + m, cols]
        if fix is not None:
            prev1, prev2 = fix(prev1, prev2, cols)
        conv = kern_ref[0:1, cols] * prev2 + kern_ref[1:2, cols] * prev1 + kern_ref[2:3, cols] * u
        mbuf_ref[0:m, cols] = (part(0) * conv).astype(BF16)
    for c in range(D_MODEL // OUT_CHUNK):
        cols = slice(c * OUT_CHUNK, (c + 1) * OUT_CHUNK)
        o_ref[:, cols] = x[:, cols] + _dot(mbuf_ref[0:m, :], w_out_ref[:, cols])


def _conv_kernel(xm_ref, xt_ref, g_ref, w_in_ref, kern_ref, w_out_ref, s1_ref, s2_ref,
                 om_ref, ot_ref, ut_ref, sp_ref, ubuf_ref, meta_ref, mbuf_ref):
    i = pl.program_id(0)

    @pl.when(i == 0)
    def _():
        ubuf_ref[0:CARRY, :] = jnp.zeros((CARRY, D_MODEL), F32)

        def fix(prev1, prev2, cols):
            step = lax.broadcasted_iota(jnp.int32, (N_SAMPLE, 1), 0) & (DEC_SEQ - 1)
            meta = lax.broadcasted_iota(jnp.int32, (N_META, 1), 0)
            p1 = [jnp.where(step >= 1, prev1[:N_SAMPLE], s1_ref[:, cols]),
                  jnp.where(meta >= 1, prev1[N_SAMPLE:], 0.0)]
            p2 = [jnp.where(step >= 2, prev2[:N_SAMPLE], s2_ref[:, cols]),
                  jnp.where(meta >= 2, prev2[N_SAMPLE:], 0.0)]
            return jnp.concatenate(p1, axis=0), jnp.concatenate(p2, axis=0)

        _conv_rows(xt_ref, ot_ref, g_ref[...], w_in_ref, kern_ref, w_out_ref, ubuf_ref, mbuf_ref, fix)
        ut_ref[...] = ubuf_ref[CARRY:CARRY + N_TAIL, :]
        meta_ref[...] = ubuf_ref[N_TAIL:N_TAIL + CARRY, :]

    @pl.when(i > 0)
    def _():
        t = i - 1
        per_seq = SEQ // CONV_TM

        @pl.when(t % per_seq == 0)
        def _():
            ubuf_ref[0:CARRY, :] = meta_ref[...]

        _conv_rows(xm_ref, om_ref, g_ref[...], w_in_ref, kern_ref, w_out_ref, ubuf_ref, mbuf_ref, None)
        last = ubuf_ref[CONV_TM:CONV_TM + CARRY, :]
        ubuf_ref[0:CARRY, :] = last

        @pl.when(t % per_seq == per_seq - 1)
        def _():
            sp_ref[t // per_seq] = last


def _conv_call(xm, xt, g, w_in, kern, w_out, s1, s2):
    main = pl.BlockSpec((CONV_TM, D_MODEL), lambda i: (_main_tile(i), 0))
    return pl.pallas_call(
        _conv_kernel,
        grid=(1 + N_MAIN // CONV_TM,),
        in_specs=[main, _resident((N_TAIL, D_MODEL)), _resident((1, D_MODEL)),
                  _resident((D_MODEL, 3 * D_MODEL)), _resident((CONV_W, D_MODEL)),
                  _resident((D_MODEL, D_MODEL)),
                  _resident((N_SAMPLE, D_MODEL)), _resident((N_SAMPLE, D_MODEL))],
        out_specs=[main,
                   pl.BlockSpec((N_TAIL, D_MODEL), lambda i: (0, 0)),
                   pl.BlockSpec((N_TAIL, D_MODEL), lambda i: (0, 0)),
                   pl.BlockSpec((BATCH, CARRY, D_MODEL), lambda i: (0, 0, 0))],
        out_shape=[jax.ShapeDtypeStruct((N_MAIN, D_MODEL), F32),
                   jax.ShapeDtypeStruct((N_TAIL, D_MODEL), F32),
                   jax.ShapeDtypeStruct((N_TAIL, D_MODEL), F32),
                   jax.ShapeDtypeStruct((BATCH, CARRY, D_MODEL), F32)],
        scratch_shapes=[pltpu.VMEM((CARRY + max(N_TAIL, CONV_TM), D_MODEL), F32),
                        pltpu.VMEM((CARRY, D_MODEL), F32),
                        pltpu.VMEM((max(N_TAIL, CONV_TM), D_MODEL), BF16)],
        input_output_aliases={0: 0},
        compiler_params=_params(),
        name="conv_mixer",
    )(xm, xt, g.reshape(1, D_MODEL), w_in, kern, w_out, s1, s2)


def _attend(qp, segs, sinks_ref, heads_per_pass=1):
    mq = qp.shape[0]
    rows = GROUP * mq
    lane = lax.broadcasted_iota(jnp.int32, (1, KV_W), 1)
    qk_head = (lane & (LANES - 1)) // HALF_DIM
    pair_head = lax.broadcasted_iota(jnp.int32, (1, LANES), 1) // HEAD_DIM
    row_head = lax.broadcasted_iota(jnp.int32, (heads_per_pass * rows, 1), 0) // mq
    masks = [jnp.concatenate([seg[2]] * (heads_per_pass * GROUP), axis=0) for seg in segs]
    outs = []
    for k0 in range(0, N_KV, heads_per_pass):
        heads = range(k0, k0 + heads_per_pass)
        qs = jnp.concatenate(
            [jnp.where(qk_head == k, qp[:, g * KV_W:(g + 1) * KV_W], jnp.zeros((), BF16))
             for k in heads for g in range(GROUP)], axis=0)
        sink = jnp.full((heads_per_pass * rows, 1), sinks_ref[k0 * GROUP] * LOG2E, F32)
        for j in range(1, heads_per_pass * GROUP):
            sink = jnp.where(row_head == j, sinks_ref[k0 * GROUP + j] * LOG2E, sink)
        scores = []
        for (kb, _, _, k_transposed), msk in zip(segs, masks):
            s = _dot(qs, kb) if k_transposed else _dot_nt(qs, kb)
            scores.append(jnp.where(msk, s, NEG))
        mx = sink
        for s in scores:
            mx = jnp.maximum(mx, jnp.max(s, axis=-1, keepdims=True))
        denom = jnp.exp2(sink - mx)
        probs = []
        for s in scores:
            p = jnp.exp2(s - mx)
            denom = denom + jnp.sum(p, axis=-1, keepdims=True)
            probs.append(p.astype(BF16))
        inv = 1.0 / denom
        for n, k in enumerate(heads):
            pair = slice((k // 2) * LANES, (k // 2 + 1) * LANES)
            part = slice(n * rows, (n + 1) * rows)
            acc = None
            for p, (_, vb, _, _) in zip(probs, segs):
                pv = _dot(p[part], jnp.where(pair_head == k % 2, vb[:, pair], jnp.zeros((), BF16)))
                acc = pv if acc is None else acc + pv
            outs.append(acc * inv[part])
    out = jnp.concatenate([outs[k] + outs[k + 1] for k in range(0, N_KV, 2)], axis=1)
    return jnp.concatenate([out[g * mq:(g + 1) * mq, :] for g in range(GROUP)], axis=1)


def _queries(x, g, wq_ref, cos, sin):
    q = _dot(_rms(x, g).astype(BF16), wq_ref[...])
    return (_rope(q, cos, sin) * (HEAD_DIM ** -0.5 * LOG2E)).astype(BF16)


def _attn_main_kernel(sinks_ref, xm_ref, g_ref, wq_ref, wo_ref, cos_ref, sin_ref,
                      km_ref, vm_ref, kt_ref, vt_ref,
                      om_ref, q_ref, ao_ref, kbuf_ref, vbuf_ref):
    i = pl.program_id(0)
    start = i % (SEQ // ATT_TM) == 0

    x = xm_ref[...]
    q_ref[...] = _queries(x, g_ref[...], wq_ref, cos_ref[...], sin_ref[...])

    @pl.when(start)
    def _():
        pad = jnp.zeros((BLK - N_META, KV_W), BF16)
        kbuf_ref[0:BLK - N_META, :] = pad
        vbuf_ref[0:BLK - N_META, :] = pad
        kbuf_ref[BLK - N_META:BLK, :] = kt_ref[N_SAMPLE:N_TAIL, :].astype(BF16)
        vbuf_ref[BLK - N_META:BLK, :] = vt_ref[N_SAMPLE:N_TAIL, :].astype(BF16)

    @pl.when(jnp.logical_not(start))
    def _():
        kbuf_ref[0:BLK, :] = kbuf_ref[ATT_TM:ATT_TM + BLK, :]
        vbuf_ref[0:BLK, :] = vbuf_ref[ATT_TM:ATT_TM + BLK, :]

    kbuf_ref[BLK:BLK + ATT_TM, :] = km_ref[...]
    vbuf_ref[BLK:BLK + ATT_TM, :] = vm_ref[...]

    qi = lax.broadcasted_iota(jnp.int32, (BLK, 2 * BLK), 0)
    kj = lax.broadcasted_iota(jnp.int32, (BLK, 2 * BLK), 1)
    band = (kj >= qi) & (kj <= qi + WINDOW)
    first_key = jnp.where(start, BLK - N_META, 0)
    for blk in range(ATT_TM // BLK):
        lo = blk * BLK
        mask = band & (kj >= first_key) if blk == 0 else band
        att = _attend(q_ref[lo:lo + BLK, :],
                      [(kbuf_ref[lo:lo + 2 * BLK, :], vbuf_ref[lo:lo + 2 * BLK, :], mask, False)],
                      sinks_ref)
        ao_ref[lo:lo + BLK, :] = att.astype(BF16)
    om_ref[...] = x + _dot(ao_ref[...], wo_ref[...])


def _attn_main_call(sinks, xm, g, wq, wo, cos_m, sin_m, km, vm, kt, vt):
    tile = lambda w: pl.BlockSpec((ATT_TM, w), lambda i: (i, 0))
    pos = pl.BlockSpec((ATT_TM, LANES), lambda i: (i % (SEQ // ATT_TM), 0))
    return pl.pallas_call(
        _attn_main_kernel,
        grid=(N_MAIN // ATT_TM,),
        in_specs=[pl.BlockSpec(memory_space=pltpu.SMEM),
                  tile(D_MODEL), _resident((1, D_MODEL)),
                  _resident((D_MODEL, D_MODEL)), _resident((D_MODEL, D_MODEL)),
                  pos, pos, tile(KV_W), tile(KV_W),
                  _resident((N_TAIL, KV_W)), _resident((N_TAIL, KV_W))],
        out_specs=tile(D_MODEL),
        out_shape=jax.ShapeDtypeStruct((N_MAIN, D_MODEL), F32),
        scratch_shapes=[pltpu.VMEM((ATT_TM, D_MODEL), BF16), pltpu.VMEM((ATT_TM, D_MODEL), BF16),
                        pltpu.VMEM((BLK + ATT_TM, KV_W), BF16),
                        pltpu.VMEM((BLK + ATT_TM, KV_W), BF16)],
        input_output_aliases={1: 0},
        compiler_params=_params(),
        name="attn_main",
    )(sinks, xm, g.reshape(1, D_MODEL), wq, wo, cos_m, sin_m, km, vm, kt, vt)


def _attn_tail_kernel(sinks_ref, xt_ref, g_ref, wq_ref, wo_ref, cos_ref, sin_ref,
                      kt_ref, vt_ref, ck_ref, cv_ref, ot_ref, q_ref, ao_ref):
    s = pl.program_id(0)

    @pl.when(s == 0)
    def _():
        q_ref[...] = _queries(xt_ref[...], g_ref[...], wq_ref, cos_ref[...], sin_ref[...])
        qi = lax.broadcasted_iota(jnp.int32, (N_META, N_META), 0)
        kj = lax.broadcasted_iota(jnp.int32, (N_META, N_META), 1)
        km = kt_ref[N_SAMPLE:N_TAIL, :].astype(BF16)
        vm = vt_ref[N_SAMPLE:N_TAIL, :].astype(BF16)
        att = _attend(q_ref[N_SAMPLE:N_TAIL, :], [(km, vm, kj <= qi, False)], sinks_ref, N_KV)
        ao_ref[N_SAMPLE:N_TAIL, :] = att.astype(BF16)

    n_cached = SAMPLE_GROUP * WINDOW
    qr = lax.broadcasted_iota(jnp.int32, (GROUP_ROWS, n_cached), 0)
    kc = lax.broadcasted_iota(jnp.int32, (GROUP_ROWS, n_cached), 1)
    mask_cached = ((kc // WINDOW) == (qr // DEC_SEQ)) & ((kc % WINDOW) >= (qr % DEC_SEQ))
    qn = lax.broadcasted_iota(jnp.int32, (GROUP_ROWS, GROUP_ROWS), 0)
    kn = lax.broadcasted_iota(jnp.int32, (GROUP_ROWS, GROUP_ROWS), 1)
    mask_new = ((kn // DEC_SEQ) == (qn // DEC_SEQ)) & ((kn % DEC_SEQ) <= (qn % DEC_SEQ))
    for grp in range(GROUPS_PER_STEP):
        first = (s * GROUPS_PER_STEP + grp) * GROUP_ROWS
        rows = pl.ds(pl.multiple_of(first, GROUP_ROWS), GROUP_ROWS)
        k_new = kt_ref[rows, :]
        v_new = vt_ref[rows, :]
        seqs = range(grp * SAMPLE_GROUP, (grp + 1) * SAMPLE_GROUP)
        k_cached_t = jnp.concatenate(
            [_rows_heads_to_split(ck_ref[b].reshape(KV_W, WINDOW)) for b in seqs], axis=1)
        v_cached = jnp.concatenate([cv_ref[b].reshape(KV_W, WINDOW).T for b in seqs], axis=0)
        att = _attend(q_ref[rows, :],
                      [(k_cached_t.astype(BF16), v_cached.astype(BF16), mask_cached, True),
                       (k_new.astype(BF16), v_new.astype(BF16), mask_new, False)],
                      sinks_ref, N_KV)
        ao_ref[rows, :] = att.astype(BF16)

    @pl.when(s == N_GROUPS // GROUPS_PER_STEP - 1)
    def _():
        ot_ref[...] = xt_ref[...] + _dot(ao_ref[...], wo_ref[...])


def _attn_tail_call(sinks, xt, g, wq, wo, cos_t, sin_t, kt, vt, ck, cv):
    cached = pl.BlockSpec((GROUPS_PER_STEP * SAMPLE_GROUP, N_KV, HEAD_DIM, WINDOW),
                          lambda s: (s, 0, 0, 0))
    return pl.pallas_call(
        _attn_tail_kernel,
        grid=(N_GROUPS // GROUPS_PER_STEP,),
        in_specs=[pl.BlockSpec(memory_space=pltpu.SMEM),
                  _resident((N_TAIL, D_MODEL)), _resident((1, D_MODEL)),
                  _resident((D_MODEL, D_MODEL)), _resident((D_MODEL, D_MODEL)),
                  _resident((N_TAIL, LANES)), _resident((N_TAIL, LANES)),
                  _resident((N_TAIL, KV_W)), _resident((N_TAIL, KV_W)),
                  cached, cached],
        out_specs=pl.BlockSpec((N_TAIL, D_MODEL), lambda s: (0, 0)),
        out_shape=jax.ShapeDtypeStruct((N_TAIL, D_MODEL), F32),
        scratch_shapes=[pltpu.VMEM((N_TAIL, D_MODEL), BF16), pltpu.VMEM((N_TAIL, D_MODEL), BF16)],
        compiler_params=_params(),
        name="attn_tail",
    )(sinks, xt, g.reshape(1, D_MODEL), wq, wo, cos_t, sin_t, kt, vt, ck, cv)


def _rope_tables(pos):
    inv = 1.0 / (ROPE_THETA ** (jnp.arange(0, HEAD_DIM, 2, dtype=F32) / HEAD_DIM))
    ang = pos.astype(F32)[:, None] * inv[None, :]
    cos, sin = jnp.cos(ang), jnp.sin(ang)
    return jnp.tile(cos, (1, N_KV)), jnp.tile(sin, (1, N_KV))


def kernel(x_prompt, x_sample, state_conv, cache_k, cache_v, meta_tokens, norm_g, ffn_w_in,
           ffn_w_out, conv_w_in, conv_kernel, conv_w_out, kv_norm_g, w_kv, w_q, w_o, sinks,
           final_norm_g):
    assert x_prompt.shape == (BATCH, SEQ, D_MODEL) and x_sample.shape == (DEC_BATCH, DEC_SEQ, D_MODEL)
    assert cache_k.shape == (DEC_BATCH, WINDOW, N_KV, HEAD_DIM)
    assert norm_g.shape[0] == 2 and conv_w_in.shape[0] == 1 and w_q.shape[0] == 1

    xm = x_prompt.reshape(N_MAIN, D_MODEL)
    xt = jnp.concatenate([x_sample.reshape(N_SAMPLE, D_MODEL), meta_tokens], axis=0)

    wq = w_q[0].reshape(D_MODEL, N_KV, GROUP, 2, HALF_DIM).transpose(0, 2, 3, 1, 4)
    wq = wq.reshape(D_MODEL, D_MODEL).astype(BF16)
    wk =w_kv[:, :KV_W].reshape(D_MODEL, N_KV, 2, HALF_DIM).transpose(0, 2, 1, 3)
    wkv = jnp.concatenate([wk.reshape(D_MODEL, KV_W), w_kv[:, KV_W:]], axis=1).astype(BF16)

    pos_tail = jnp.concatenate([
        jnp.tile(PAST_LEN + jnp.arange(DEC_SEQ, dtype=jnp.int32), DEC_BATCH),
        jnp.arange(N_META, dtype=jnp.int32)])
    cos_t, sin_t = _rope_tables(pos_tail)
    cos_row, sin_row = _rope_tables(jnp.arange(TM, dtype=jnp.int32))
    cos_tile, sin_tile = _rope_tables(N_META + TM * jnp.arange(TILES_PER_SEQ, dtype=jnp.int32))
    rope_seeds = (cos_row, sin_row, cos_tile[:, None, :], sin_tile[:, None, :])

    sc = state_conv[0]
    grow = lambda a: jnp.pad(a, ((0, 0), (0, DEC_SEQ - a.shape[1]), (0, 0))).reshape(N_SAMPLE, D_MODEL)
    s1 = grow(sc[:, 1:])
    s2 = grow(sc)

    ffn_casts = lambda l, j, steps=N_TILES: [(ffn_w_in, (l, j), steps, None),
                                             (ffn_w_out, (l, j), steps // 2, None)]
    wo_cast = (w_o, (0,), N_HEADS, lambda b: (b % N_KV) * GROUP + b // N_KV)
    slot_minor = lambda a: a.transpose(0, 2, 3, 1)
    ck, cv = slot_minor(cache_k), slot_minor(cache_v)

    xm, xt, cos_m, sin_m, cw_in, cw_out, w_in, w_out, w_in_10, w_out_10 = _ffn_first_call(
        xm, xt, norm_g[0, 0], ffn_w_in, ffn_w_out, (0, 0),
        [(conv_w_in, (0,), N_TILES, None), (conv_w_out, (0,), N_TILES, None)]
        + ffn_casts(0, 1) + ffn_casts(1, 0), rope_seeds, name="ffn_l0a")
    xm, xt, u_tail, sp = _conv_call(xm, xt, norm_g[0, 1], cw_in, conv_kernel[0], cw_out, s1, s2)
    xm, xt, wo, w_in_11, w_out_11 = _ffn_call(
        xm, xt, norm_g[0, 2], w_in, w_out, name="ffn_l0b", tm=WIDE_TM,
        casts=[wo_cast] + ffn_casts(1, 1, N_MAIN // WIDE_TM))

    xm, xt, km, vm, kt, vt, k_last, v_last, new_k_t, new_v_t = _ffn_call(
        xm, xt, norm_g[1, 0], w_in_10, w_out_10, name="ffn_l1a_kv", tm=WIDE_TM,
        kv=(kv_norm_g, wkv, cos_m, sin_m, cos_t, sin_t))
    xt = _attn_tail_call(sinks[0], xt, norm_g[1, 1], wq, wo, cos_t, sin_t, kt, vt, ck, cv)
    xm = _attn_main_call(sinks[0], xm, norm_g[1, 1], wq, wo, cos_m, sin_m, km, vm, kt, vt)
    ym, yt, nk, nv = _ffn_call(xm, xt, norm_g[1, 2], w_in_11, w_out_11, name="ffn_l1b_final",
                               final_g=final_norm_g, roll=(new_k_t, new_v_t, ck, cv), tm=WIDE_TM)

    y_prompt = ym.reshape(BATCH, SEQ, D_MODEL)
    y_sample = yt[:N_SAMPLE].reshape(DEC_BATCH, DEC_SEQ, D_MODEL)
    new_state_conv_p = sp[:, CARRY - (CONV_W - 1):][None]
    new_state_conv_s = u_tail[:N_SAMPLE].reshape(DEC_BATCH, DEC_SEQ, D_MODEL)[:, DEC_SEQ - (CONV_W - 1):][None]
    slot_major = lambda a: a.transpose(0, 3, 1, 2)
    new_k_p = slot_major(k_last.reshape(BATCH, N_KV, HEAD_DIM, WINDOW))
    new_v_p = slot_major(v_last.reshape(BATCH, N_KV, HEAD_DIM, WINDOW))
    return (y_prompt, y_sample, new_state_conv_p, new_state_conv_s,
            new_k_p, new_v_p, slot_major(nk), slot_major(nv))
```

```python
import jax
import jax.numpy as jnp
from jax import lax
from jax.experimental import pallas as pl
from jax.experimental.pallas import tpu as pltpu

D_MODEL = 1024
BATCH = 2
SEQ = 8192
DEC_BATCH = 128
DEC_SEQ = 4
PAST_LEN = 8192
N_META = 16
D_FF = 2816
CONV_W = 3
HEAD_DIM = 64
N_HEADS = 16
N_KV = 4
GROUP = 4
WINDOW = 128
ROPE_THETA = 10000.0
EPS = 1e-6
NEG = -1e30
LOG2E = 1.4426950408889634

KV_W = N_KV * HEAD_DIM
HALF_DIM = HEAD_DIM // 2
N_MAIN = BATCH * SEQ
N_SAMPLE = DEC_BATCH * DEC_SEQ
N_TAIL = N_SAMPLE + N_META
TM = 512
N_TILES = N_MAIN // TM
TILES_PER_SEQ = SEQ // TM
WIDE_TM = 1024
FF_CHUNK = 256
OUT_CHUNK = 256
BLK = WINDOW
ATT_TM = 1024
SAMPLE_GROUP = 8
N_GROUPS = DEC_BATCH // SAMPLE_GROUP
GROUPS_PER_STEP = 4
GROUP_ROWS = SAMPLE_GROUP * DEC_SEQ
LANES = 128
BF16_ROWS = 16
V7X_VMEM_LIMIT = 56 * 1024 * 1024

F32 = jnp.float32
BF16 = jnp.bfloat16


def _rms(x, g):
    return x * lax.rsqrt(jnp.mean(x * x, axis=-1, keepdims=True) + EPS) * g


def _rope(x, cos, sin):
    outs = []
    for c in range(x.shape[1] // KV_W):
        x1 = x[:, c * KV_W:c * KV_W + LANES]
        x2 = x[:, c * KV_W + LANES:(c + 1) * KV_W]
        outs += [x1 * cos - x2 * sin, x2 * cos + x1 * sin]
    return jnp.concatenate(outs, axis=1)


def _rows_split_to_heads(t):
    return jnp.concatenate([t[half * LANES + h * HALF_DIM:half * LANES + (h + 1) * HALF_DIM]
                            for h in range(N_KV) for half in range(2)], axis=0)


def _rows_heads_to_split(t):
    return jnp.concatenate([t[h * HEAD_DIM + half * HALF_DIM:h * HEAD_DIM + (half + 1) * HALF_DIM]
                            for half in range(2) for h in range(N_KV)], axis=0)


def _dot(a, b):
    return jnp.dot(a, b, preferred_element_type=F32)


def _dot_nt(a, b):
    return lax.dot_general(a, b, (((1,), (1,)), ((), ())), preferred_element_type=F32)


def _resident(shape):
    nd = len(shape)
    return pl.BlockSpec(shape, lambda i: (0,) * nd, pipeline_mode=pl.Buffered(1))


def _main_tile(i):
    return jnp.maximum(i - 1, 0)


def _params():
    return pltpu.CompilerParams(dimension_semantics=("arbitrary",),
                                vmem_limit_bytes=V7X_VMEM_LIMIT)


def _ffn_rows(x_ref, o_ref, g, w_in_ref, w_out_ref, a_ref, final_g=None):
    x = x_ref[...]
    m = x.shape[0]
    hn = _rms(x, g).astype(BF16)
    for c in range(D_FF // FF_CHUNK):
        lo = c * FF_CHUNK
        a_ref[0:m, lo:lo + FF_CHUNK] = _swiglu_chunk(
            hn, w_in_ref[:, lo:lo + FF_CHUNK], w_in_ref[:, D_FF + lo:D_FF + lo + FF_CHUNK])
    _ffn_out(x, o_ref, a_ref, w_out_ref, final_g)


def _swiglu_chunk(hn, w_gate, w_up):
    gate = _dot(hn, w_gate)
    return (gate * jax.nn.sigmoid(gate) * _dot(hn, w_up)).astype(BF16)


def _ffn_out(x, o_ref, a_ref, w_out_ref, final_g):
    m = x.shape[0]
    sq = None
    for c in range(D_MODEL // OUT_CHUNK):
        cols = slice(c * OUT_CHUNK, (c + 1) * OUT_CHUNK)
        y = x[:, cols] + 0.5 * _dot(a_ref[0:m, :], w_out_ref[:, cols])
        o_ref[:, cols] = y
        if final_g is not None:
            part = jnp.sum(y * y, axis=-1, keepdims=True)
            sq = part if sq is None else sq + part
    if final_g is not None:
        o_ref[...] = o_ref[...] * lax.rsqrt(sq * (1.0 / D_MODEL) + EPS) * final_g


N_LOAD = 8


def _make_ffn_first_kernel(n_cast):
    in_rows, out_rows = D_MODEL // N_LOAD, D_FF // N_LOAD

    def body(*refs):
        xm_ref, xt_ref, g_ref, wg_ref, wu_ref, wo_ref = refs[:6]
        cast_src = refs[6:6 + n_cast]
        om_ref, ot_ref = refs[6 + n_cast:8 + n_cast]
        cast_dst = refs[8 + n_cast:8 + 2 * n_cast]
        a_ref, w_in_ref, w_out_ref = refs[8 + 2 * n_cast:]
        i = pl.program_id(0)

        for src, dst in zip(cast_src, cast_dst):
            dst[...] = src[...].astype(BF16)

        @pl.when(i < N_LOAD)
        def _():
            rows_in = pl.ds(pl.multiple_of(i * in_rows, in_rows), in_rows)
            w_in_ref[rows_in, 0:D_FF] = wg_ref[...].astype(BF16)
            w_in_ref[rows_in, D_FF:2 * D_FF] = wu_ref[...].astype(BF16)
            w_out_ref[pl.ds(pl.multiple_of(i * out_rows, out_rows), out_rows), :] = wo_ref[...].astype(BF16)

        @pl.when(i == N_LOAD)
        def _():
            _ffn_rows(xt_ref, ot_ref, g_ref[...], w_in_ref, w_out_ref, a_ref)

        @pl.when(i > N_LOAD)
        def _():
            _ffn_rows(xm_ref, om_ref, g_ref[...], w_in_ref, w_out_ref, a_ref)

    return body


def _ffn_first_call(xm, xt, g, ffn_w_in, ffn_w_out, lead, casts, *, name):
    slab = lambda i: jnp.minimum(i, N_LOAD - 1)
    nl = (None,) * len(lead)
    first_main = N_LOAD + 1
    main = pl.BlockSpec((TM, D_MODEL), lambda i: (jnp.maximum(i - first_main, 0), 0))
    c_in, c_out, c_shape, c_args = _cast_specs(casts, first_main)
    assert D_MODEL % (N_LOAD * BF16_ROWS) == 0 and D_FF % (N_LOAD * BF16_ROWS) == 0
    return pl.pallas_call(
        _make_ffn_first_kernel(len(casts)),
        grid=(first_main + N_TILES,),
        in_specs=[main, _resident((N_TAIL, D_MODEL)), _resident((1, D_MODEL)),
                  pl.BlockSpec(nl + (D_MODEL // N_LOAD, D_FF), lambda i: lead + (slab(i), 0)),
                  pl.BlockSpec(nl + (D_MODEL // N_LOAD, D_FF), lambda i: lead + (slab(i), 1)),
                  pl.BlockSpec(nl + (D_FF // N_LOAD, D_MODEL), lambda i: lead + (slab(i), 0))] + c_in,
        out_specs=[main, _resident((N_TAIL, D_MODEL))] + c_out,
        out_shape=[jax.ShapeDtypeStruct((N_MAIN, D_MODEL), F32),
                   jax.ShapeDtypeStruct((N_TAIL, D_MODEL), F32)] + c_shape,
        scratch_shapes=[pltpu.VMEM((N_TAIL, D_FF), BF16),
                        pltpu.VMEM((D_MODEL, 2 * D_FF), BF16), pltpu.VMEM((D_FF, D_MODEL), BF16)],
        compiler_params=_params(),
        name=name,
    )(xm, xt, g.reshape(1, D_MODEL), ffn_w_in, ffn_w_in, ffn_w_out, *c_args)


def _make_ffn_kernel(has_final, has_kv, has_roll, n_cast, tm):
    cache_seqs = DEC_BATCH * tm // N_MAIN

    def body(*refs):
        refs = list(refs)
        xm_ref, xt_ref, g_ref, w_in_ref, w_out_ref = refs[:5]
        pos = 5
        if has_final:
            gf_ref = refs[pos]
            pos += 1
        if has_kv:
            gkv_ref, wkv_ref, cm_ref, sm_ref, ct_ref, st_ref = refs[pos:pos + 6]
            pos += 6
        if has_roll:
            newk_ref, newv_ref, ck_ref, cv_ref = refs[pos:pos + 4]
            pos += 4
        cast_src = refs[pos:pos + n_cast]
        pos += n_cast
        om_ref, ot_ref = refs[pos:pos + 2]
        pos += 2
        if has_kv:
            km_ref, vm_ref, kt_ref, vt_ref, kl_ref, vl_ref, newk_ref, newv_ref = refs[pos:pos + 8]
            pos += 8
        if has_roll:
            nk_ref, nv_ref = refs[pos:pos + 2]
            pos += 2
        cast_dst = refs[pos:pos + n_cast]
        pos += n_cast
        a_ref = refs[pos]
        i = pl.program_id(0)

        for src, dst in zip(cast_src, cast_dst):
            dst[...] = src[...].astype(BF16)

        def rows(x_ref, o_ref):
            _ffn_rows(x_ref, o_ref, g_ref[...], w_in_ref, w_out_ref, a_ref,
                      gf_ref[...] if has_final else None)

        def project_kv(x_ref, cos_ref, sin_ref, k_ref, v_ref):
            kv = _dot(_rms(x_ref[...], gkv_ref[...]).astype(BF16), wkv_ref[...])
            k = _rope(kv[:, :KV_W], cos_ref[...], sin_ref[...])
            v = kv[:, KV_W:]
            k_ref[...] = k.astype(k_ref.dtype)
            v_ref[...] = v.astype(v_ref.dtype)
            return k, v

        @pl.when(i == 0)
        def _():
            if has_kv:
                k, v = project_kv(xt_ref, ct_ref, st_ref, kt_ref, vt_ref)
                for c in range(N_SAMPLE // LANES):
                    newk_ref[c] = _rows_split_to_heads(k[c * LANES:(c + 1) * LANES, :].T)
                    newv_ref[c] = v[c * LANES:(c + 1) * LANES, :].T
            rows(xt_ref, ot_ref)

        @pl.when(i > 0)
        def _():
            if has_kv:
                k, v = project_kv(xm_ref, cm_ref, sm_ref, km_ref, vm_ref)
                kl_ref[...] = _rows_split_to_heads(k[tm - WINDOW:, :].T)
                vl_ref[...] = v[tm - WINDOW:, :].T
            if has_roll:
                t = i - 1
                lane = lax.broadcasted_iota(jnp.int32, (1, WINDOW), 1)
                is_new = lane >= WINDOW - DEC_SEQ
                first_lane = (t * cache_seqs * DEC_SEQ) % LANES
                for new_ref, c_ref, n_ref in ((newk_ref, ck_ref, nk_ref), (newv_ref, cv_ref, nv_ref)):
                    new_t = new_ref[(t * cache_seqs * DEC_SEQ) // LANES]
                    for b in range(cache_seqs):
                        shifted = pltpu.roll(c_ref[b].reshape(KV_W, WINDOW), WINDOW - DEC_SEQ, 1)
                        amount = (WINDOW - DEC_SEQ - b * DEC_SEQ - first_lane) & (LANES - 1)
                        placed = pltpu.roll(new_t, amount, 1)
                        n_ref[b] = jnp.where(is_new, placed, shifted).reshape(N_KV, HEAD_DIM, WINDOW)
            rows(xm_ref, om_ref)

    return body


def _cast_specs(casts, first_main_step=1):
    in_specs, out_specs, out_shape, args = [], [], [], []
    for arr, lead, steps, src_block in casts:
        n_rows, n_cols = arr.shape[-2:]
        rows = n_rows // steps
        assert rows * steps == n_rows and rows % BF16_ROWS == 0
        blk = lambda i, steps=steps: jnp.clip(i - first_main_step, 0, steps - 1)
        src = blk if src_block is None else (lambda i, blk=blk, f=src_block: f(blk(i)))
        in_specs.append(pl.BlockSpec((None,) * len(lead) + (rows, n_cols),
                                     lambda i, lead=lead, src=src: lead + (src(i), 0)))
        out_specs.append(pl.BlockSpec((rows, n_cols), lambda i, blk=blk: (blk(i), 0)))
        out_shape.append(jax.ShapeDtypeStruct((n_rows, n_cols), BF16))
        args.append(arr)
    return in_specs, out_specs, out_shape, args


def _ffn_call(xm, xt, g, w_in, w_out, *, name, final_g=None, kv=None, roll=None, casts=(), tm=TM):
    new_t_shape = (N_SAMPLE // LANES, KV_W, LANES)
    n_tiles = N_MAIN // tm
    assert all(c[2] <= n_tiles for c in casts)
    main = pl.BlockSpec((tm, D_MODEL), lambda i: (_main_tile(i), 0))
    in_specs = [main, _resident((N_TAIL, D_MODEL)), _resident((1, D_MODEL)),
                _resident((D_MODEL, 2 * D_FF)), _resident((D_FF, D_MODEL))]
    args = [xm, xt, g.reshape(1, D_MODEL), w_in, w_out]
    out_shape = [jax.ShapeDtypeStruct((N_MAIN, D_MODEL), F32),
                 jax.ShapeDtypeStruct((N_TAIL, D_MODEL), F32)]
    out_specs = [main, _resident((N_TAIL, D_MODEL))]
    if final_g is not None:
        in_specs.append(_resident((1, D_MODEL)))
        args.append(final_g.reshape(1, D_MODEL))
    scratch = [pltpu.VMEM((max(N_TAIL, tm), D_FF), BF16)]
    if kv is not None:
        g_kv, w_kv, cos_m, sin_m, cos_t, sin_t = kv
        pos = pl.BlockSpec((tm, LANES), lambda i: (_main_tile(i) % (SEQ // tm), 0))
        kv_main = pl.BlockSpec((tm, KV_W), lambda i: (_main_tile(i), 0))
        in_specs += [_resident((1, D_MODEL)), _resident((D_MODEL, 2 * KV_W)), pos, pos,
                     _resident((N_TAIL, LANES)), _resident((N_TAIL, LANES))]
        args += [g_kv.reshape(1, D_MODEL), w_kv, cos_m, sin_m, cos_t, sin_t]
        out_shape += [jax.ShapeDtypeStruct((N_MAIN, KV_W), BF16)] * 2
        out_shape += [jax.ShapeDtypeStruct((N_TAIL, KV_W), F32)] * 2
        out_shape += [jax.ShapeDtypeStruct((BATCH, KV_W, WINDOW), F32)] * 2
        out_shape += [jax.ShapeDtypeStruct(new_t_shape, F32)] * 2
        last = pl.BlockSpec((None, KV_W, WINDOW), lambda i: (_main_tile(i) // (SEQ // tm), 0, 0))
        out_specs += [kv_main, kv_main, _resident((N_TAIL, KV_W)), _resident((N_TAIL, KV_W)),
                      last, last, _resident(new_t_shape), _resident(new_t_shape)]
    if roll is not None:
        cached = pl.BlockSpec((DEC_BATCH // n_tiles, N_KV, HEAD_DIM, WINDOW),
                              lambda i: (_main_tile(i), 0, 0, 0))
        in_specs += [_resident(new_t_shape), _resident(new_t_shape), cached, cached]
        args += list(roll)
        out_shape += [jax.ShapeDtypeStruct((DEC_BATCH, N_KV, HEAD_DIM, WINDOW), F32)] * 2
        out_specs += [cached, cached]
    c_in, c_out, c_shape, c_args = _cast_specs(casts)
    return pl.pallas_call(
        _make_ffn_kernel(final_g is not None, kv is not None, roll is not None, len(casts), tm),
        grid=(1 + n_tiles,),
        in_specs=in_specs + c_in,
        out_specs=out_specs + c_out,
        out_shape=out_shape + c_shape,
        scratch_shapes=scratch,
        input_output_aliases={0: 0},
        compiler_params=_params(),
        name=name,
    )(*args, *c_args)


CARRY = 8
CONV_TM = 1024


def _conv_rows(x_ref, o_ref, g, w_in_ref, kern_ref, w_out_ref, ubuf_ref, mbuf_ref, fix):
    x = x_ref[...]
    m = x.shape[0]
    hn = _rms(x, g).astype(BF16)
    for c in range(D_MODEL // OUT_CHUNK):
        cols = slice(c * OUT_CHUNK, (c + 1) * OUT_CHUNK)
        part = lambda j: _dot(hn, w_in_ref[:, j * D_MODEL + c * OUT_CHUNK:j * D_MODEL + (c + 1) * OUT_CHUNK])
        u = part(1) * part(2)
        ubuf_ref[CARRY:CARRY + m, cols] = u
        prev1 = ubuf_ref[CARRY - 1:CARRY - 1 + m, cols]
        prev2 = ubuf_ref[CARRY - 2:CARRY - 2 + m, cols]
        if fix is not None:
            prev1, prev2 = fix(prev1, prev2, cols)
        conv = kern_ref[0:1, cols] * prev2 + kern_ref[1:2, cols] * prev1 + kern_ref[2:3, cols] * u
        mbuf_ref[0:m, cols] = (part(0) * conv).astype(BF16)
    for c in range(D_MODEL // OUT_CHUNK):
        cols = slice(c * OUT_CHUNK, (c + 1) * OUT_CHUNK)
        o_ref[:, cols] = x[:, cols] + _dot(mbuf_ref[0:m, :], w_out_ref[:, cols])


def _conv_kernel(xm_ref, xt_ref, g_ref, w_in_ref, kern_ref, w_out_ref, s1_ref, s2_ref,
                 om_ref, ot_ref, ut_ref, sp_ref, ubuf_ref, meta_ref, mbuf_ref):
    i = pl.program_id(0)

    @pl.when(i == 0)
    def _():
        ubuf_ref[0:CARRY, :] = jnp.zeros((CARRY, D_MODEL), F32)

        def fix(prev1, prev2, cols):
            step = lax.broadcasted_iota(jnp.int32, (N_SAMPLE, 1), 0) & (DEC_SEQ - 1)
            meta = lax.broadcasted_iota(jnp.int32, (N_META, 1), 0)
            p1 = [jnp.where(step >= 1, prev1[:N_SAMPLE], s1_ref[:, cols]),
                  jnp.where(meta >= 1, prev1[N_SAMPLE:], 0.0)]
            p2 = [jnp.where(step >= 2, prev2[:N_SAMPLE], s2_ref[:, cols]),
                  jnp.where(meta >= 2, prev2[N_SAMPLE:], 0.0)]
            return jnp.concatenate(p1, axis=0), jnp.concatenate(p2, axis=0)

        _conv_rows(xt_ref, ot_ref, g_ref[...], w_in_ref, kern_ref, w_out_ref, ubuf_ref, mbuf_ref, fix)
        ut_ref[...] = ubuf_ref[CARRY:CARRY + N_TAIL, :]
        meta_ref[...] = ubuf_ref[N_TAIL:N_TAIL + CARRY, :]

    @pl.when(i > 0)
    def _():
        t = i - 1
        per_seq = SEQ // CONV_TM

        @pl.when(t % per_seq == 0)
        def _():
            ubuf_ref[0:CARRY, :] = meta_ref[...]

        _conv_rows(xm_ref, om_ref, g_ref[...], w_in_ref, kern_ref, w_out_ref, ubuf_ref, mbuf_ref, None)
        last = ubuf_ref[CONV_TM:CONV_TM + CARRY, :]
        ubuf_ref[0:CARRY, :] = last

        @pl.when(t % per_seq == per_seq - 1)
        def _():
            sp_ref[t // per_seq] = last


def _conv_call(xm, xt, g, w_in, kern, w_out, s1, s2):
    main = pl.BlockSpec((CONV_TM, D_MODEL), lambda i: (_main_tile(i), 0))
    return pl.pallas_call(
        _conv_kernel,
        grid=(1 + N_MAIN // CONV_TM,),
        in_specs=[main, _resident((N_TAIL, D_MODEL)), _resident((1, D_MODEL)),
                  _resident((D_MODEL, 3 * D_MODEL)), _resident((CONV_W, D_MODEL)),
                  _resident((D_MODEL, D_MODEL)),
                  _resident((N_SAMPLE, D_MODEL)), _resident((N_SAMPLE, D_MODEL))],
        out_specs=[main,
                   pl.BlockSpec((N_TAIL, D_MODEL), lambda i: (0, 0)),
                   pl.BlockSpec((N_TAIL, D_MODEL), lambda i: (0, 0)),
                   pl.BlockSpec((BATCH, CARRY, D_MODEL), lambda i: (0, 0, 0))],
        out_shape=[jax.ShapeDtypeStruct((N_MAIN, D_MODEL), F32),
                   jax.ShapeDtypeStruct((N_TAIL, D_MODEL), F32),
                   jax.ShapeDtypeStruct((N_TAIL, D_MODEL), F32),
                   jax.ShapeDtypeStruct((BATCH, CARRY, D_MODEL), F32)],
        scratch_shapes=[pltpu.VMEM((CARRY + max(N_TAIL, CONV_TM), D_MODEL), F32),
                        pltpu.VMEM((CARRY, D_MODEL), F32),
                        pltpu.VMEM((max(N_TAIL, CONV_TM), D_MODEL), BF16)],
        input_output_aliases={0: 0},
        compiler_params=_params(),
        name="conv_mixer",
    )(xm, xt, g.reshape(1, D_MODEL), w_in, kern, w_out, s1, s2)


def _attend(qp, segs, sinks_ref, heads_per_pass=1):
    mq = qp.shape[0]
    rows = GROUP * mq
    lane = lax.broadcasted_iota(jnp.int32, (1, KV_W), 1)
    qk_head = (lane & (LANES - 1)) // HALF_DIM
    pair_head = lax.broadcasted_iota(jnp.int32, (1, LANES), 1) // HEAD_DIM
    row_head = lax.broadcasted_iota(jnp.int32, (heads_per_pass * rows, 1), 0) // mq
    masks = [jnp.concatenate([seg[2]] * (heads_per_pass * GROUP), axis=0) for seg in segs]
    outs = []
    for k0 in range(0, N_KV, heads_per_pass):
        heads = range(k0, k0 + heads_per_pass)
        qs = jnp.concatenate(
            [jnp.where(qk_head == k, qp[:, g * KV_W:(g + 1) * KV_W], jnp.zeros((), BF16))
             for k in heads for g in range(GROUP)], axis=0)
        sink = jnp.full((heads_per_pass * rows, 1), sinks_ref[k0 * GROUP] * LOG2E, F32)
        for j in range(1, heads_per_pass * GROUP):
            sink = jnp.where(row_head == j, sinks_ref[k0 * GROUP + j] * LOG2E, sink)
        scores = []
        for (kb, _, _, k_transposed), msk in zip(segs, masks):
            s = _dot(qs, kb) if k_transposed else _dot_nt(qs, kb)
            scores.append(jnp.where(msk, s, NEG))
        mx = sink
        for s in scores:
            mx = jnp.maximum(mx, jnp.max(s, axis=-1, keepdims=True))
        denom = jnp.exp2(sink - mx)
        probs = []
        for s in scores:
            p = jnp.exp2(s - mx)
            denom = denom + jnp.sum(p, axis=-1, keepdims=True)
            probs.append(p.astype(BF16))
        inv = 1.0 / denom
        for n, k in enumerate(heads):
            pair = slice((k // 2) * LANES, (k // 2 + 1) * LANES)
            part = slice(n * rows, (n + 1) * rows)
            acc = None
            for p, (_, vb, _, _) in zip(probs, segs):
                pv = _dot(p[part], jnp.where(pair_head == k % 2, vb[:, pair], jnp.zeros((), BF16)))
                acc = pv if acc is None else acc + pv
            outs.append(acc * inv[part])
    out = jnp.concatenate([outs[k] + outs[k + 1] for k in range(0, N_KV, 2)], axis=1)
    return jnp.concatenate([out[g * mq:(g + 1) * mq, :] for g in range(GROUP)], axis=1)


def _queries(x, g, wq_ref, cos, sin):
    q = _dot(_rms(x, g).astype(BF16), wq_ref[...])
    return (_rope(q, cos, sin) * (HEAD_DIM ** -0.5 * LOG2E)).astype(BF16)


def _attn_main_kernel(sinks_ref, xm_ref, g_ref, wq_ref, wo_ref, cos_ref, sin_ref,
                      km_ref, vm_ref, kt_ref, vt_ref,
                      om_ref, q_ref, ao_ref, kbuf_ref, vbuf_ref):
    i = pl.program_id(0)
    start = i % (SEQ // ATT_TM) == 0

    x = xm_ref[...]
    q_ref[...] = _queries(x, g_ref[...], wq_ref, cos_ref[...], sin_ref[...])

    @pl.when(start)
    def _():
        pad = jnp.zeros((BLK - N_META, KV_W), BF16)
        kbuf_ref[0:BLK - N_META, :] = pad
        vbuf_ref[0:BLK - N_META, :] = pad
        kbuf_ref[BLK - N_META:BLK, :] = kt_ref[N_SAMPLE:N_TAIL, :].astype(BF16)
        vbuf_ref[BLK - N_META:BLK, :] = vt_ref[N_SAMPLE:N_TAIL, :].astype(BF16)

    @pl.when(jnp.logical_not(start))
    def _():
        kbuf_ref[0:BLK, :] = kbuf_ref[ATT_TM:ATT_TM + BLK, :]
        vbuf_ref[0:BLK, :] = vbuf_ref[ATT_TM:ATT_TM + BLK, :]

    kbuf_ref[BLK:BLK + ATT_TM, :] = km_ref[...]
    vbuf_ref[BLK:BLK + ATT_TM, :] = vm_ref[...]

    qi = lax.broadcasted_iota(jnp.int32, (BLK, 2 * BLK), 0)
    kj = lax.broadcasted_iota(jnp.int32, (BLK, 2 * BLK), 1)
    band = (kj >= qi) & (kj <= qi + WINDOW)
    first_key = jnp.where(start, BLK - N_META, 0)
    for blk in range(ATT_TM // BLK):
        lo = blk * BLK
        mask = band & (kj >= first_key) if blk == 0 else band
        att = _attend(q_ref[lo:lo + BLK, :],
                      [(kbuf_ref[lo:lo + 2 * BLK, :], vbuf_ref[lo:lo + 2 * BLK, :], mask, False)],
                      sinks_ref)
        ao_ref[lo:lo + BLK, :] = att.astype(BF16)
    om_ref[...] = x + _dot(ao_ref[...], wo_ref[...])


def _attn_main_call(sinks, xm, g, wq, wo, cos_m, sin_m, km, vm, kt, vt):
    tile = lambda w: pl.BlockSpec((ATT_TM, w), lambda i: (i, 0))
    pos = pl.BlockSpec((ATT_TM, LANES), lambda i: (i % (SEQ // ATT_TM), 0))
    return pl.pallas_call(
        _attn_main_kernel,
        grid=(N_MAIN // ATT_TM,),
        in_specs=[pl.BlockSpec(memory_space=pltpu.SMEM),
                  tile(D_MODEL), _resident((1, D_MODEL)),
                  _resident((D_MODEL, D_MODEL)), _resident((D_MODEL, D_MODEL)),
                  pos, pos, tile(KV_W), tile(KV_W),
                  _resident((N_TAIL, KV_W)), _resident((N_TAIL, KV_W))],
        out_specs=tile(D_MODEL),
        out_shape=jax.ShapeDtypeStruct((N_MAIN, D_MODEL), F32),
        scratch_shapes=[pltpu.VMEM((ATT_TM, D_MODEL), BF16), pltpu.VMEM((ATT_TM, D_MODEL), BF16),
                        pltpu.VMEM((BLK + ATT_TM, KV_W), BF16),
                        pltpu.VMEM((BLK + ATT_TM, KV_W), BF16)],
        input_output_aliases={1: 0},
        compiler_params=_params(),
        name="attn_main",
    )(sinks, xm, g.reshape(1, D_MODEL), wq, wo, cos_m, sin_m, km, vm, kt, vt)


def _attn_tail_kernel(sinks_ref, xt_ref, g_ref, wq_ref, wo_ref, cos_ref, sin_ref,
                      kt_ref, vt_ref, ck_ref, cv_ref, ot_ref, q_ref, ao_ref):
    s = pl.program_id(0)

    @pl.when(s == 0)
    def _():
        q_ref[...] = _queries(xt_ref[...], g_ref[...], wq_ref, cos_ref[...], sin_ref[...])
        qi = lax.broadcasted_iota(jnp.int32, (N_META, N_META), 0)
        kj = lax.broadcasted_iota(jnp.int32, (N_META, N_META), 1)
        km = kt_ref[N_SAMPLE:N_TAIL, :].astype(BF16)
        vm = vt_ref[N_SAMPLE:N_TAIL, :].astype(BF16)
        att = _attend(q_ref[N_SAMPLE:N_TAIL, :], [(km, vm, kj <= qi, False)], sinks_ref, N_KV)
        ao_ref[N_SAMPLE:N_TAIL, :] = att.astype(BF16)

    n_cached = SAMPLE_GROUP * WINDOW
    qr = lax.broadcasted_iota(jnp.int32, (GROUP_ROWS, n_cached), 0)
    kc = lax.broadcasted_iota(jnp.int32, (GROUP_ROWS, n_cached), 1)
    mask_cached = ((kc // WINDOW) == (qr // DEC_SEQ)) & ((kc % WINDOW) >= (qr % DEC_SEQ))
    qn = lax.broadcasted_iota(jnp.int32, (GROUP_ROWS, GROUP_ROWS), 0)
    kn = lax.broadcasted_iota(jnp.int32, (GROUP_ROWS, GROUP_ROWS), 1)
    mask_new = ((kn // DEC_SEQ) == (qn // DEC_SEQ)) & ((kn % DEC_SEQ) <= (qn % DEC_SEQ))
    for grp in range(GROUPS_PER_STEP):
        first = (s * GROUPS_PER_STEP + grp) * GROUP_ROWS
        rows = pl.ds(pl.multiple_of(first, GROUP_ROWS), GROUP_ROWS)
        k_new = kt_ref[rows, :]
        v_new = vt_ref[rows, :]
        seqs = range(grp * SAMPLE_GROUP, (grp + 1) * SAMPLE_GROUP)
        k_cached_t = jnp.concatenate(
            [_rows_heads_to_split(ck_ref[b].reshape(KV_W, WINDOW)) for b in seqs], axis=1)
        v_cached = jnp.concatenate([cv_ref[b].reshape(KV_W, WINDOW).T for b in seqs], axis=0)
        att = _attend(q_ref[rows, :],
                      [(k_cached_t.astype(BF16), v_cached.astype(BF16), mask_cached, True),
                       (k_new.astype(BF16), v_new.astype(BF16), mask_new, False)],
                      sinks_ref, N_KV)
        ao_ref[rows, :] = att.astype(BF16)

    @pl.when(s == N_GROUPS // GROUPS_PER_STEP - 1)
    def _():
        ot_ref[...] = xt_ref[...] + _dot(ao_ref[...], wo_ref[...])


def _attn_tail_call(sinks, xt, g, wq, wo, cos_t, sin_t, kt, vt, ck, cv):
    cached = pl.BlockSpec((GROUPS_PER_STEP * SAMPLE_GROUP, N_KV, HEAD_DIM, WINDOW),
                          lambda s: (s, 0, 0, 0))
    return pl.pallas_call(
        _attn_tail_kernel,
        grid=(N_GROUPS // GROUPS_PER_STEP,),
        in_specs=[pl.BlockSpec(memory_space=pltpu.SMEM),
                  _resident((N_TAIL, D_MODEL)), _resident((1, D_MODEL)),
                  _resident((D_MODEL, D_MODEL)), _resident((D_MODEL, D_MODEL)),
                  _resident((N_TAIL, LANES)), _resident((N_TAIL, LANES)),
                  _resident((N_TAIL, KV_W)), _resident((N_TAIL, KV_W)),
                  cached, cached],
        out_specs=pl.BlockSpec((N_TAIL, D_MODEL), lambda s: (0, 0)),
        out_shape=jax.ShapeDtypeStruct((N_TAIL, D_MODEL), F32),
        scratch_shapes=[pltpu.VMEM((N_TAIL, D_MODEL), BF16), pltpu.VMEM((N_TAIL, D_MODEL), BF16)],
        compiler_params=_params(),
        name="attn_tail",
    )(sinks, xt, g.reshape(1, D_MODEL), wq, wo, cos_t, sin_t, kt, vt, ck, cv)


def _rope_tables(pos):
    inv = 1.0 / (ROPE_THETA ** (jnp.arange(0, HEAD_DIM, 2, dtype=F32) / HEAD_DIM))
    ang = pos.astype(F32)[:, None] * inv[None, :]
    cos, sin = jnp.cos(ang), jnp.sin(ang)
    return jnp.tile(cos, (1, N_KV)), jnp.tile(sin, (1, N_KV))


def kernel(x_prompt, x_sample, state_conv, cache_k, cache_v, meta_tokens, norm_g, ffn_w_in,
           ffn_w_out, conv_w_in, conv_kernel, conv_w_out, kv_norm_g, w_kv, w_q, w_o, sinks,
           final_norm_g):
    assert x_prompt.shape == (BATCH, SEQ, D_MODEL) and x_sample.shape == (DEC_BATCH, DEC_SEQ, D_MODEL)
    assert cache_k.shape == (DEC_BATCH, WINDOW, N_KV, HEAD_DIM)
    assert norm_g.shape[0] == 2 and conv_w_in.shape[0] == 1 and w_q.shape[0] == 1

    xm = x_prompt.reshape(N_MAIN, D_MODEL)
    xt = jnp.concatenate([x_sample.reshape(N_SAMPLE, D_MODEL), meta_tokens], axis=0)

    wq = w_q[0].reshape(D_MODEL, N_KV, GROUP, 2, HALF_DIM).transpose(0, 2, 3, 1, 4)
    wq = wq.reshape(D_MODEL, D_MODEL).astype(BF16)
    wk =w_kv[:, :KV_W].reshape(D_MODEL, N_KV, 2, HALF_DIM).transpose(0, 2, 1, 3)
    wkv = jnp.concatenate([wk.reshape(D_MODEL, KV_W), w_kv[:, KV_W:]], axis=1).astype(BF16)

    pos_main = N_META + jnp.arange(SEQ, dtype=jnp.int32)
    pos_tail = jnp.concatenate([
        jnp.tile(PAST_LEN + jnp.arange(DEC_SEQ, dtype=jnp.int32), DEC_BATCH),
        jnp.arange(N_META, dtype=jnp.int32)])
    cos_m, sin_m = _rope_tables(pos_main)
    cos_t, sin_t = _rope_tables(pos_tail)

    sc = state_conv[0]
    grow = lambda a: jnp.pad(a, ((0, 0), (0, DEC_SEQ - a.shape[1]), (0, 0))).reshape(N_SAMPLE, D_MODEL)
    s1 = grow(sc[:, 1:])
    s2 = grow(sc)

    ffn_casts = lambda l, j, steps=N_TILES: [(ffn_w_in, (l, j), steps, None),
                                             (ffn_w_out, (l, j), steps // 2, None)]
    wo_cast = (w_o, (0,), N_HEADS, lambda b: (b % N_KV) * GROUP + b // N_KV)
    slot_minor = lambda a: a.transpose(0, 2, 3, 1)
    ck, cv = slot_minor(cache_k), slot_minor(cache_v)

    xm, xt, cw_in, cw_out, w_in, w_out, w_in_10, w_out_10 = _ffn_first_call(
        xm, xt, norm_g[0, 0], ffn_w_in, ffn_w_out, (0, 0),
        [(conv_w_in, (0,), N_TILES, None), (conv_w_out, (0,), N_TILES, None)]
        + ffn_casts(0, 1) + ffn_casts(1, 0), name="ffn_l0a")
    xm, xt, u_tail, sp = _conv_call(xm, xt, norm_g[0, 1], cw_in, conv_kernel[0], cw_out, s1, s2)
    xm, xt, wo, w_in_11, w_out_11 = _ffn_call(
        xm, xt, norm_g[0, 2], w_in, w_out, name="ffn_l0b", tm=WIDE_TM,
        casts=[wo_cast] + ffn_casts(1, 1, N_MAIN // WIDE_TM))

    xm, xt, km, vm, kt, vt, k_last, v_last, new_k_t, new_v_t = _ffn_call(
        xm, xt, norm_g[1, 0], w_in_10, w_out_10, name="ffn_l1a_kv", tm=WIDE_TM,
        kv=(kv_norm_g, wkv, cos_m, sin_m, cos_t, sin_t))
    xt = _attn_tail_call(sinks[0], xt, norm_g[1, 1], wq, wo, cos_t, sin_t, kt, vt, ck, cv)
    xm = _attn_main_call(sinks[0], xm, norm_g[1, 1], wq, wo, cos_m, sin_m, km, vm, kt, vt)
    ym, yt, nk, nv = _ffn_call(xm, xt, norm_g[1, 2], w_in_11, w_out_11, name="ffn_l1b_final",
                               final_g=final_norm_g, roll=(new_k_t, new_v_t, ck, cv), tm=WIDE_TM)

    y_prompt = ym.reshape(BATCH, SEQ, D_MODEL)
    y_sample = yt[:N_SAMPLE].reshape(DEC_BATCH, DEC_SEQ, D_MODEL)
    new_state_conv_p = sp[:, CARRY - (CONV_W - 1):][None]
    new_state_conv_s = u_tail[:N_SAMPLE].reshape(DEC_BATCH, DEC_SEQ, D_MODEL)[:, DEC_SEQ - (CONV_W - 1):][None]
    slot_major = lambda a: a.transpose(0, 3, 1, 2)
    new_k_p = slot_major(k_last.reshape(BATCH, N_KV, HEAD_DIM, WINDOW))
    new_v_p = slot_major(v_last.reshape(BATCH, N_KV, HEAD_DIM, WINDOW))
    return (y_prompt, y_sample, new_state_conv_p, new_state_conv_s,
            new_k_p, new_v_p, slot_major(nk), slot_major(nv))
```

```python
import jax
import jax.numpy as jnp
from jax import lax
from jax.experimental import pallas as pl
from jax.experimental.pallas import tpu as pltpu

D_MODEL = 1024
BATCH = 2
SEQ = 8192
DEC_BATCH = 128
DEC_SEQ = 4
PAST_LEN = 8192
N_META = 16
D_FF = 2816
CONV_W = 3
HEAD_DIM = 64
N_HEADS = 16
N_KV = 4
GROUP = 4
WINDOW = 128
ROPE_THETA = 10000.0
EPS = 1e-6
NEG = -1e30
LOG2E = 1.4426950408889634

KV_W = N_KV * HEAD_DIM
HALF_DIM = HEAD_DIM // 2
N_MAIN = BATCH * SEQ
N_SAMPLE = DEC_BATCH * DEC_SEQ
N_TAIL = N_SAMPLE + N_META
TM = 512
N_TILES = N_MAIN // TM
WIDE_TM = 1024
CONV_TM = 1024
FF_CHUNK = 256
OUT_CHUNK = 256
BLK = WINDOW
ATT_TM = 1024
SAMPLE_GROUP = 8
N_GROUPS = DEC_BATCH // SAMPLE_GROUP
GROUPS_PER_STEP = 2
GROUP_ROWS = SAMPLE_GROUP * DEC_SEQ
LANES = 128
BF16_ROWS = 16
V7X_VMEM_LIMIT = 56 * 1024 * 1024

F32 = jnp.float32
BF16 = jnp.bfloat16


def _rms(x, g):
    return x * lax.rsqrt(jnp.mean(x * x, axis=-1, keepdims=True) + EPS) * g


def _rope(x, cos, sin):
    outs = []
    for c in range(x.shape[1] // KV_W):
        x1 = x[:, c * KV_W:c * KV_W + LANES]
        x2 = x[:, c * KV_W + LANES:(c + 1) * KV_W]
        outs += [x1 * cos - x2 * sin, x2 * cos + x1 * sin]
    return jnp.concatenate(outs, axis=1)


def _rows_split_to_heads(t):
    return jnp.concatenate([t[half * LANES + h * HALF_DIM:half * LANES + (h + 1) * HALF_DIM]
                            for h in range(N_KV) for half in range(2)], axis=0)


def _rows_heads_to_split(t):
    return jnp.concatenate([t[h * HEAD_DIM + half * HALF_DIM:h * HEAD_DIM + (half + 1) * HALF_DIM]
                            for half in range(2) for h in range(N_KV)], axis=0)


def _dot(a, b):
    return jnp.dot(a, b, preferred_element_type=F32)


def _dot_nt(a, b):
    return lax.dot_general(a, b, (((1,), (1,)), ((), ())), preferred_element_type=F32)


def _resident(shape):
    nd = len(shape)
    return pl.BlockSpec(shape, lambda i: (0,) * nd, pipeline_mode=pl.Buffered(1))


def _main_tile(i):
    return jnp.maximum(i - 1, 0)


def _params():
    return pltpu.CompilerParams(dimension_semantics=("arbitrary",),
                                vmem_limit_bytes=V7X_VMEM_LIMIT)


def _ffn_rows(x_ref, o_ref, g, w_in_ref, w_out_ref, a_ref, final_g=None):
    x = x_ref[...]
    m = x.shape[0]
    hn = _rms(x, g).astype(BF16)
    for c in range(D_FF // FF_CHUNK):
        lo = c * FF_CHUNK
        a_ref[0:m, lo:lo + FF_CHUNK] = _swiglu_chunk(
            hn, w_in_ref[:, lo:lo + FF_CHUNK], w_in_ref[:, D_FF + lo:D_FF + lo + FF_CHUNK])
    _ffn_out(x, o_ref, a_ref, w_out_ref, final_g)


def _swiglu_chunk(hn, w_gate, w_up):
    gate = _dot(hn, w_gate)
    return (gate * jax.nn.sigmoid(gate) * _dot(hn, w_up)).astype(BF16)


def _ffn_out(x, o_ref, a_ref, w_out_ref, final_g):
    m = x.shape[0]
    sq = None
    for c in range(D_MODEL // OUT_CHUNK):
        cols = slice(c * OUT_CHUNK, (c + 1) * OUT_CHUNK)
        y = x[:, cols] + 0.5 * _dot(a_ref[0:m, :], w_out_ref[:, cols])
        o_ref[:, cols] = y
        if final_g is not None:
            part = jnp.sum(y * y, axis=-1, keepdims=True)
            sq = part if sq is None else sq + part
    if final_g is not None:
        o_ref[...] = o_ref[...] * lax.rsqrt(sq * (1.0 / D_MODEL) + EPS) * final_g


N_LOAD = 8


def _make_ffn_first_kernel(n_cast):
    in_rows, out_rows = D_MODEL // N_LOAD, D_FF // N_LOAD

    def body(*refs):
        xm_ref, xt_ref, g_ref, wg_ref, wu_ref, wo_ref = refs[:6]
        cast_src = refs[6:6 + n_cast]
        om_ref, ot_ref = refs[6 + n_cast:8 + n_cast]
        cast_dst = refs[8 + n_cast:8 + 2 * n_cast]
        a_ref, w_in_ref, w_out_ref = refs[8 + 2 * n_cast:]
        i = pl.program_id(0)

        for src, dst in zip(cast_src, cast_dst):
            dst[...] = src[...].astype(BF16)

        @pl.when(i < N_LOAD)
        def _():
            rows_in = pl.ds(pl.multiple_of(i * in_rows, in_rows), in_rows)
            w_in_ref[rows_in, 0:D_FF] = wg_ref[...].astype(BF16)
            w_in_ref[rows_in, D_FF:2 * D_FF] = wu_ref[...].astype(BF16)
            w_out_ref[pl.ds(pl.multiple_of(i * out_rows, out_rows), out_rows), :] = wo_ref[...].astype(BF16)

        @pl.when(i == N_LOAD)
        def _():
            _ffn_rows(xt_ref, ot_ref, g_ref[...], w_in_ref, w_out_ref, a_ref)

        @pl.when(i > N_LOAD)
        def _():
            _ffn_rows(xm_ref, om_ref, g_ref[...], w_in_ref, w_out_ref, a_ref)

    return body


def _ffn_first_call(xm, xt, g, ffn_w_in, ffn_w_out, lead, casts, *, name):
    slab = lambda i: jnp.minimum(i, N_LOAD - 1)
    nl = (None,) * len(lead)
    first_main = N_LOAD + 1
    main = pl.BlockSpec((TM, D_MODEL), lambda i: (jnp.maximum(i - first_main, 0), 0))
    c_in, c_out, c_shape, c_args = _cast_specs(casts, first_main)
    assert D_MODEL % (N_LOAD * BF16_ROWS) == 0 and D_FF % (N_LOAD * BF16_ROWS) == 0
    return pl.pallas_call(
        _make_ffn_first_kernel(len(casts)),
        grid=(first_main + N_TILES,),
        in_specs=[main, _resident((N_TAIL, D_MODEL)), _resident((1, D_MODEL)),
                  pl.BlockSpec(nl + (D_MODEL // N_LOAD, D_FF), lambda i: lead + (slab(i), 0)),
                  pl.BlockSpec(nl + (D_MODEL // N_LOAD, D_FF), lambda i: lead + (slab(i), 1)),
                  pl.BlockSpec(nl + (D_FF // N_LOAD, D_MODEL), lambda i: lead + (slab(i), 0))] + c_in,
        out_specs=[main, _resident((N_TAIL, D_MODEL))] + c_out,
        out_shape=[jax.ShapeDtypeStruct((N_MAIN, D_MODEL), F32),
                   jax.ShapeDtypeStruct((N_TAIL, D_MODEL), F32)] + c_shape,
        scratch_shapes=[pltpu.VMEM((N_TAIL, D_FF), BF16),
                        pltpu.VMEM((D_MODEL, 2 * D_FF), BF16), pltpu.VMEM((D_FF, D_MODEL), BF16)],
        compiler_params=_params(),
        name=name,
    )(xm, xt, g.reshape(1, D_MODEL), ffn_w_in, ffn_w_in, ffn_w_out, *c_args)


def _make_ffn_kernel(has_final, has_kv, has_roll, n_cast, tm):
    assert not (has_kv and has_roll)
    cache_seqs = DEC_BATCH * tm // N_MAIN

    def body(*refs):
        refs = list(refs)
        xm_ref, xt_ref, g_ref, w_in_ref, w_out_ref = refs[:5]
        pos = 5
        if has_final:
            gf_ref = refs[pos]
            pos += 1
        if has_kv:
            gkv_ref, wkv_ref, cm_ref, sm_ref, ct_ref, st_ref = refs[pos:pos + 6]
            pos += 6
        if has_roll:
            newk_ref, newv_ref, ck_ref, cv_ref = refs[pos:pos + 4]
            pos += 4
        cast_src = refs[pos:pos + n_cast]
        pos += n_cast
        om_ref, ot_ref = refs[pos:pos + 2]
        pos += 2
        if has_kv:
            km_ref, vm_ref, kt_ref, vt_ref, kl_ref, vl_ref, newk_ref, newv_ref = refs[pos:pos + 8]
            pos += 8
        if has_roll:
            nk_ref, nv_ref = refs[pos:pos + 2]
            pos += 2
        cast_dst = refs[pos:pos + n_cast]
        pos += n_cast
        a_ref = refs[pos]
        i = pl.program_id(0)

        for src, dst in zip(cast_src, cast_dst):
            dst[...] = src[...].astype(BF16)

        def rows(x_ref, o_ref):
            _ffn_rows(x_ref, o_ref, g_ref[...], w_in_ref, w_out_ref, a_ref,
                      gf_ref[...] if has_final else None)

        def project_kv(x_ref, cos_ref, sin_ref, k_ref, v_ref):
            kv = _dot(_rms(x_ref[...], gkv_ref[...]).astype(BF16), wkv_ref[...])
            k = _rope(kv[:, :KV_W], cos_ref[...], sin_ref[...])
            v = kv[:, KV_W:]
            k_ref[...] = k.astype(k_ref.dtype)
            v_ref[...] = v.astype(v_ref.dtype)
            return k, v

        @pl.when(i == 0)
        def _():
            if has_kv:
                k, v = project_kv(xt_ref, ct_ref, st_ref, kt_ref, vt_ref)
                for c in range(N_SAMPLE // LANES):
                    newk_ref[c] = _rows_split_to_heads(k[c * LANES:(c + 1) * LANES, :].T)
                    newv_ref[c] = v[c * LANES:(c + 1) * LANES, :].T
            rows(xt_ref, ot_ref)

        @pl.when(i > 0)
        def _():
            if has_kv:
                k, v = project_kv(xm_ref, cm_ref, sm_ref, km_ref, vm_ref)
                kl_ref[...] = _rows_split_to_heads(k[tm - WINDOW:, :].T)
                vl_ref[...] = v[tm - WINDOW:, :].T
            if has_roll:
                t = i - 1
                lane = lax.broadcasted_iota(jnp.int32, (1, WINDOW), 1)
                is_new = lane >= WINDOW - DEC_SEQ
                first_lane = (t * cache_seqs * DEC_SEQ) % LANES
                for new_ref, c_ref, n_ref in ((newk_ref, ck_ref, nk_ref), (newv_ref, cv_ref, nv_ref)):
                    new_t = new_ref[(t * cache_seqs * DEC_SEQ) // LANES]
                    for b in range(cache_seqs):
                        shifted = pltpu.roll(c_ref[b].reshape(KV_W, WINDOW), WINDOW - DEC_SEQ, 1)
                        amount = (WINDOW - DEC_SEQ - b * DEC_SEQ - first_lane) & (LANES - 1)
                        placed = pltpu.roll(new_t, amount, 1)
                        n_ref[b] = jnp.where(is_new, placed, shifted).reshape(N_KV, HEAD_DIM, WINDOW)
            rows(xm_ref, om_ref)

    return body


def _cast_specs(casts, first_main_step=1):
    in_specs, out_specs, out_shape, args = [], [], [], []
    for arr, lead, steps, src_block in casts:
        n_rows, n_cols = arr.shape[-2:]
        rows = n_rows // steps
        assert rows * steps == n_rows and rows % BF16_ROWS == 0
        blk = lambda i, steps=steps: jnp.clip(i - first_main_step, 0, steps - 1)
        src = blk if src_block is None else (lambda i, blk=blk, f=src_block: f(blk(i)))
        in_specs.append(pl.BlockSpec((None,) * len(lead) + (rows, n_cols),
                                     lambda i, lead=lead, src=src: lead + (src(i), 0)))
        out_specs.append(pl.BlockSpec((rows, n_cols), lambda i, blk=blk: (blk(i), 0)))
        out_shape.append(jax.ShapeDtypeStruct((n_rows, n_cols), BF16))
        args.append(arr)
    return in_specs, out_specs, out_shape, args


def _ffn_call(xm, xt, g, w_in, w_out, *, name, final_g=None, kv=None, roll=None, casts=(), tm=TM):
    new_t_shape = (N_SAMPLE // LANES, KV_W, LANES)
    n_tiles = N_MAIN // tm
    assert all(c[2] <= n_tiles for c in casts)
    main = pl.BlockSpec((tm, D_MODEL), lambda i: (_main_tile(i), 0))
    in_specs = [main, _resident((N_TAIL, D_MODEL)), _resident((1, D_MODEL)),
                _resident((D_MODEL, 2 * D_FF)), _resident((D_FF, D_MODEL))]
    args = [xm, xt, g.reshape(1, D_MODEL), w_in, w_out]
    out_shape = [jax.ShapeDtypeStruct((N_MAIN, D_MODEL), F32),
                 jax.ShapeDtypeStruct((N_TAIL, D_MODEL), F32)]
    out_specs = [main, _resident((N_TAIL, D_MODEL))]
    if final_g is not None:
        in_specs.append(_resident((1, D_MODEL)))
        args.append(final_g.reshape(1, D_MODEL))
    scratch = [pltpu.VMEM((max(N_TAIL, tm), D_FF), BF16)]
    if kv is not None:
        g_kv, w_kv, cos_m, sin_m, cos_t, sin_t = kv
        pos = pl.BlockSpec((tm, LANES), lambda i: (_main_tile(i) % (SEQ // tm), 0))
        kv_main = pl.BlockSpec((tm, KV_W), lambda i: (_main_tile(i), 0))
        in_specs += [_resident((1, D_MODEL)), _resident((D_MODEL, 2 * KV_W)), pos, pos,
                     _resident((N_TAIL, LANES)), _resident((N_TAIL, LANES))]
        args += [g_kv.reshape(1, D_MODEL), w_kv, cos_m, sin_m, cos_t, sin_t]
        out_shape += [jax.ShapeDtypeStruct((N_MAIN, KV_W), BF16)] * 2
        out_shape += [jax.ShapeDtypeStruct((N_TAIL, KV_W), F32)] * 2
        out_shape += [jax.ShapeDtypeStruct((BATCH, KV_W, WINDOW), F32)] * 2
        out_shape += [jax.ShapeDtypeStruct(new_t_shape, F32)] * 2
        last = pl.BlockSpec((None, KV_W, WINDOW), lambda i: (_main_tile(i) // (SEQ // tm), 0, 0))
        out_specs += [kv_main, kv_main, _resident((N_TAIL, KV_W)), _resident((N_TAIL, KV_W)),
                      last, last, _resident(new_t_shape), _resident(new_t_shape)]
    if roll is not None:
        cached = pl.BlockSpec((DEC_BATCH // n_tiles, N_KV, HEAD_DIM, WINDOW),
                              lambda i: (_main_tile(i), 0, 0, 0))
        in_specs += [_resident(new_t_shape), _resident(new_t_shape), cached, cached]
        args += list(roll)
        out_shape += [jax.ShapeDtypeStruct((DEC_BATCH, N_KV, HEAD_DIM, WINDOW), F32)] * 2
        out_specs += [cached, cached]
    c_in, c_out, c_shape, c_args = _cast_specs(casts)
    return pl.pallas_call(
        _make_ffn_kernel(final_g is not None, kv is not None, roll is not None, len(casts), tm),
        grid=(1 + n_tiles,),
        in_specs=in_specs + c_in,
        out_specs=out_specs + c_out,
        out_shape=out_shape + c_shape,
        scratch_shapes=scratch,
        input_output_aliases={0: 0},
        compiler_params=_params(),
        name=name,
    )(*args, *c_args)


CARRY = 8


def _conv_rows(x_ref, o_ref, g, w_in_ref, kern_ref, w_out_ref, ubuf_ref, mbuf_ref, fix):
    x = x_ref[...]
    m = x.shape[0]
    hn = _rms(x, g).astype(BF16)
    for c in range(D_MODEL // OUT_CHUNK):
        cols = slice(c * OUT_CHUNK, (c + 1) * OUT_CHUNK)
        part = lambda j: _dot(hn, w_in_ref[:, j * D_MODEL + c * OUT_CHUNK:j * D_MODEL + (c + 1) * OUT_CHUNK])
        u = part(1) * part(2)
        ubuf_ref[CARRY:CARRY + m, cols] = u
        prev1 = ubuf_ref[CARRY - 1:CARRY - 1 + m, cols]
        prev2 = ubuf_ref[CARRY - 2:CARRY - 2 + m, cols]
        if fix is not None:
            prev1, prev2 = fix(prev1, prev2, cols)
        conv = kern_ref[0:1, cols] * prev2 + kern_ref[1:2, cols] * prev1 + kern_ref[2:3, cols] * u
        mbuf_ref[0:m, cols] = (part(0) * conv).astype(BF16)
    for c in range(D_MODEL // OUT_CHUNK):
        cols = slice(c * OUT_CHUNK, (c + 1) * OUT_CHUNK)
        o_ref[:, cols] = x[:, cols] + _dot(mbuf_ref[0:m, :], w_out_ref[:, cols])


def _conv_kernel(xm_ref, xt_ref, g_ref, w_in_ref, kern_ref, w_out_ref, s1_ref, s2_ref,
                 om_ref, ot_ref, ut_ref, sp_ref, ubuf_ref, meta_ref, mbuf_ref):
    i = pl.program_id(0)

    @pl.when(i == 0)
    def _():
        ubuf_ref[0:CARRY, :] = jnp.zeros((CARRY, D_MODEL), F32)

        def fix(prev1, prev2, cols):
            step = lax.broadcasted_iota(jnp.int32, (N_SAMPLE, 1), 0) & (DEC_SEQ - 1)
            meta = lax.broadcasted_iota(jnp.int32, (N_META, 1), 0)
            p1 = [jnp.where(step >= 1, prev1[:N_SAMPLE], s1_ref[:, cols]),
                  jnp.where(meta >= 1, prev1[N_SAMPLE:], 0.0)]
            p2 = [jnp.where(step >= 2, prev2[:N_SAMPLE], s2_ref[:, cols]),
                  jnp.where(meta >= 2, prev2[N_SAMPLE:], 0.0)]
            return jnp.concatenate(p1, axis=0), jnp.concatenate(p2, axis=0)

        _conv_rows(xt_ref, ot_ref, g_ref[...], w_in_ref, kern_ref, w_out_ref, ubuf_ref, mbuf_ref, fix)
        ut_ref[...] = ubuf_ref[CARRY:CARRY + N_TAIL, :]
        meta_ref[...] = ubuf_ref[N_TAIL:N_TAIL + CARRY, :]

    @pl.when(i > 0)
    def _():
        t = i - 1
        per_seq = SEQ // CONV_TM

        @pl.when(t % per_seq == 0)
        def _():
            ubuf_ref[0:CARRY, :] = meta_ref[...]

        _conv_rows(xm_ref, om_ref, g_ref[...], w_in_ref, kern_ref, w_out_ref, ubuf_ref, mbuf_ref, None)
        last = ubuf_ref[CONV_TM:CONV_TM + CARRY, :]
        ubuf_ref[0:CARRY, :] = last

        @pl.when(t % per_seq == per_seq - 1)
        def _():
            sp_ref[t // per_seq] = last


def _conv_call(xm, xt, g, w_in, kern, w_out, s1, s2):
    main = pl.BlockSpec((CONV_TM, D_MODEL), lambda i: (_main_tile(i), 0))
    return pl.pallas_call(
        _conv_kernel,
        grid=(1 + N_MAIN // CONV_TM,),
        in_specs=[main, _resident((N_TAIL, D_MODEL)), _resident((1, D_MODEL)),
                  _resident((D_MODEL, 3 * D_MODEL)), _resident((CONV_W, D_MODEL)),
                  _resident((D_MODEL, D_MODEL)),
                  _resident((N_SAMPLE, D_MODEL)), _resident((N_SAMPLE, D_MODEL))],
        out_specs=[main,
                   pl.BlockSpec((N_TAIL, D_MODEL), lambda i: (0, 0)),
                   pl.BlockSpec((N_TAIL, D_MODEL), lambda i: (0, 0)),
                   pl.BlockSpec((BATCH, CARRY, D_MODEL), lambda i: (0, 0, 0))],
        out_shape=[jax.ShapeDtypeStruct((N_MAIN, D_MODEL), F32),
                   jax.ShapeDtypeStruct((N_TAIL, D_MODEL), F32),
                   jax.ShapeDtypeStruct((N_TAIL, D_MODEL), F32),
                   jax.ShapeDtypeStruct((BATCH, CARRY, D_MODEL), F32)],
        scratch_shapes=[pltpu.VMEM((CARRY + max(N_TAIL, CONV_TM), D_MODEL), F32),
                        pltpu.VMEM((CARRY, D_MODEL), F32),
                        pltpu.VMEM((max(N_TAIL, CONV_TM), D_MODEL), BF16)],
        input_output_aliases={0: 0},
        compiler_params=_params(),
        name="conv_mixer",
    )(xm, xt, g.reshape(1, D_MODEL), w_in, kern, w_out, s1, s2)


def _attend(qp, segs, sinks_ref, heads_per_pass=1):
    mq = qp.shape[0]
    rows = GROUP * mq
    lane = lax.broadcasted_iota(jnp.int32, (1, KV_W), 1)
    qk_head = (lane & (LANES - 1)) // HALF_DIM
    pair_head = lax.broadcasted_iota(jnp.int32, (1, LANES), 1) // HEAD_DIM
    row_head = lax.broadcasted_iota(jnp.int32, (heads_per_pass * rows, 1), 0) // mq
    masks = [jnp.concatenate([seg[2]] * (heads_per_pass * GROUP), axis=0) for seg in segs]
    outs = []
    for k0 in range(0, N_KV, heads_per_pass):
        heads = range(k0, k0 + heads_per_pass)
        qs = jnp.concatenate(
            [jnp.where(qk_head == k, qp[:, g * KV_W:(g + 1) * KV_W], jnp.zeros((), BF16))
             for k in heads for g in range(GROUP)], axis=0)
        sink = jnp.full((heads_per_pass * rows, 1), sinks_ref[k0 * GROUP] * LOG2E, F32)
        for j in range(1, heads_per_pass * GROUP):
            sink = jnp.where(row_head == j, sinks_ref[k0 * GROUP + j] * LOG2E, sink)
        scores = []
        for (kb, _, _, k_transposed), msk in zip(segs, masks):
            s = _dot(qs, kb) if k_transposed else _dot_nt(qs, kb)
            scores.append(jnp.where(msk, s, NEG))
        mx = sink
        for s in scores:
            mx = jnp.maximum(mx, jnp.max(s, axis=-1, keepdims=True))
        denom = jnp.exp2(sink - mx)
        probs = []
        for s in scores:
            p = jnp.exp2(s - mx)
            denom = denom + jnp.sum(p, axis=-1, keepdims=True)
            probs.append(p.astype(BF16))
        inv = 1.0 / denom
        for n, k in enumerate(heads):
            pair = slice((k // 2) * LANES, (k // 2 + 1) * LANES)
            part = slice(n * rows, (n + 1) * rows)
            acc = None
            for p, (_, vb, _, _) in zip(probs, segs):
                pv = _dot(p[part], jnp.where(pair_head == k % 2, vb[:, pair], jnp.zeros((), BF16)))
                acc = pv if acc is None else acc + pv
            outs.append(acc * inv[part])
    out = jnp.concatenate([outs[k] + outs[k + 1] for k in range(0, N_KV, 2)], axis=1)
    return jnp.concatenate([out[g * mq:(g + 1) * mq, :] for g in range(GROUP)], axis=1)


def _queries(x, g, wq_ref, cos, sin):
    q = _dot(_rms(x, g).astype(BF16), wq_ref[...])
    return (_rope(q, cos, sin) * (HEAD_DIM ** -0.5 * LOG2E)).astype(BF16)


def _attn_main_kernel(sinks_ref, xm_ref, g_ref, wq_ref, wo_ref, cos_ref, sin_ref,
                      km_ref, vm_ref, kt_ref, vt_ref,
                      om_ref, q_ref, ao_ref, kbuf_ref, vbuf_ref):
    i = pl.program_id(0)
    start = i % (SEQ // ATT_TM) == 0

    x = xm_ref[...]
    q_ref[...] = _queries(x, g_ref[...], wq_ref, cos_ref[...], sin_ref[...])

    @pl.when(start)
    def _():
        pad = jnp.zeros((BLK - N_META, KV_W), BF16)
        kbuf_ref[0:BLK - N_META, :] = pad
        vbuf_ref[0:BLK - N_META, :] = pad
        kbuf_ref[BLK - N_META:BLK, :] = kt_ref[N_SAMPLE:N_TAIL, :].astype(BF16)
        vbuf_ref[BLK - N_META:BLK, :] = vt_ref[N_SAMPLE:N_TAIL, :].astype(BF16)

    @pl.when(jnp.logical_not(start))
    def _():
        kbuf_ref[0:BLK, :] = kbuf_ref[ATT_TM:ATT_TM + BLK, :]
        vbuf_ref[0:BLK, :] = vbuf_ref[ATT_TM:ATT_TM + BLK, :]

    kbuf_ref[BLK:BLK + ATT_TM, :] = km_ref[...]
    vbuf_ref[BLK:BLK + ATT_TM, :] = vm_ref[...]

    qi = lax.broadcasted_iota(jnp.int32, (BLK, 2 * BLK), 0)
    kj = lax.broadcasted_iota(jnp.int32, (BLK, 2 * BLK), 1)
    band = (kj >= qi) & (kj <= qi + WINDOW)
    first_key = jnp.where(start, BLK - N_META, 0)
    for blk in range(ATT_TM // BLK):
        lo = blk * BLK
        mask = band & (kj >= first_key) if blk == 0 else band
        att = _attend(q_ref[lo:lo + BLK, :],
                      [(kbuf_ref[lo:lo + 2 * BLK, :], vbuf_ref[lo:lo + 2 * BLK, :], mask, False)],
                      sinks_ref)
        ao_ref[lo:lo + BLK, :] = att.astype(BF16)
    om_ref[...] = x + _dot(ao_ref[...], wo_ref[...])


def _attn_main_call(sinks, xm, g, wq, wo, cos_m, sin_m, km, vm, kt, vt):
    tile = lambda w: pl.BlockSpec((ATT_TM, w), lambda i: (i, 0))
    pos = pl.BlockSpec((ATT_TM, LANES), lambda i: (i % (SEQ // ATT_TM), 0))
    return pl.pallas_call(
        _attn_main_kernel,
        grid=(N_MAIN // ATT_TM,),
        in_specs=[pl.BlockSpec(memory_space=pltpu.SMEM),
                  tile(D_MODEL), _resident((1, D_MODEL)),
                  _resident((D_MODEL, D_MODEL)), _resident((D_MODEL, D_MODEL)),
                  pos, pos, tile(KV_W), tile(KV_W),
                  _resident((N_TAIL, KV_W)), _resident((N_TAIL, KV_W))],
        out_specs=tile(D_MODEL),
        out_shape=jax.ShapeDtypeStruct((N_MAIN, D_MODEL), F32),
        scratch_shapes=[pltpu.VMEM((ATT_TM, D_MODEL), BF16), pltpu.VMEM((ATT_TM, D_MODEL), BF16),
                        pltpu.VMEM((BLK + ATT_TM, KV_W), BF16),
                        pltpu.VMEM((BLK + ATT_TM, KV_W), BF16)],
        input_output_aliases={1: 0},
        compiler_params=_params(),
        name="attn_main",
    )(sinks, xm, g.reshape(1, D_MODEL), wq, wo, cos_m, sin_m, km, vm, kt, vt)


def _attn_tail_kernel(sinks_ref, xt_ref, g_ref, wq_ref, wo_ref, cos_ref, sin_ref,
                      kt_ref, vt_ref, ck_ref, cv_ref, ot_ref, q_ref, ao_ref):
    s = pl.program_id(0)

    @pl.when(s == 0)
    def _():
        q_ref[...] = _queries(xt_ref[...], g_ref[...], wq_ref, cos_ref[...], sin_ref[...])
        qi = lax.broadcasted_iota(jnp.int32, (N_META, N_META), 0)
        kj = lax.broadcasted_iota(jnp.int32, (N_META, N_META), 1)
        km = kt_ref[N_SAMPLE:N_TAIL, :].astype(BF16)
        vm = vt_ref[N_SAMPLE:N_TAIL, :].astype(BF16)
        att = _attend(q_ref[N_SAMPLE:N_TAIL, :], [(km, vm, kj <= qi, False)], sinks_ref, N_KV)
        ao_ref[N_SAMPLE:N_TAIL, :] = att.astype(BF16)

    n_cached = SAMPLE_GROUP * WINDOW
    qr = lax.broadcasted_iota(jnp.int32, (GROUP_ROWS, n_cached), 0)
    kc = lax.broadcasted_iota(jnp.int32, (GROUP_ROWS, n_cached), 1)
    mask_cached = ((kc // WINDOW) == (qr // DEC_SEQ)) & ((kc % WINDOW) >= (qr % DEC_SEQ))
    qn = lax.broadcasted_iota(jnp.int32, (GROUP_ROWS, GROUP_ROWS), 0)
    kn = lax.broadcasted_iota(jnp.int32, (GROUP_ROWS, GROUP_ROWS), 1)
    mask_new = ((kn // DEC_SEQ) == (qn // DEC_SEQ)) & ((kn % DEC_SEQ) <= (qn % DEC_SEQ))
    for grp in range(GROUPS_PER_STEP):
        first = (s * GROUPS_PER_STEP + grp) * GROUP_ROWS
        rows = pl.ds(pl.multiple_of(first, GROUP_ROWS), GROUP_ROWS)
        k_new = kt_ref[rows, :]
        v_new = vt_ref[rows, :]
        seqs = range(grp * SAMPLE_GROUP, (grp + 1) * SAMPLE_GROUP)
        k_cached_t = jnp.concatenate(
            [_rows_heads_to_split(ck_ref[b].reshape(KV_W, WINDOW)) for b in seqs], axis=1)
        v_cached = jnp.concatenate([cv_ref[b].reshape(KV_W, WINDOW).T for b in seqs], axis=0)
        att = _attend(q_ref[rows, :],
                      [(k_cached_t.astype(BF16), v_cached.astype(BF16), mask_cached, True),
                       (k_new.astype(BF16), v_new.astype(BF16), mask_new, False)],
                      sinks_ref, N_KV)
        ao_ref[rows, :] = att.astype(BF16)

    @pl.when(s == N_GROUPS // GROUPS_PER_STEP - 1)
    def _():
        ot_ref[...] = xt_ref[...] + _dot(ao_ref[...], wo_ref[...])


def _attn_tail_call(sinks, xt, g, wq, wo, cos_t, sin_t, kt, vt, ck, cv):
    cached = pl.BlockSpec((GROUPS_PER_STEP * SAMPLE_GROUP, N_KV, HEAD_DIM, WINDOW),
                          lambda s: (s, 0, 0, 0))
    return pl.pallas_call(
        _attn_tail_kernel,
        grid=(N_GROUPS // GROUPS_PER_STEP,),
        in_specs=[pl.BlockSpec(memory_space=pltpu.SMEM),
                  _resident((N_TAIL, D_MODEL)), _resident((1, D_MODEL)),
                  _resident((D_MODEL, D_MODEL)), _resident((D_MODEL, D_MODEL)),
                  _resident((N_TAIL, LANES)), _resident((N_TAIL, LANES)),
                  _resident((N_TAIL, KV_W)), _resident((N_TAIL, KV_W)),
                  cached, cached],
        out_specs=pl.BlockSpec((N_TAIL, D_MODEL), lambda s: (0, 0)),
        out_shape=jax.ShapeDtypeStruct((N_TAIL, D_MODEL), F32),
        scratch_shapes=[pltpu.VMEM((N_TAIL, D_MODEL), BF16), pltpu.VMEM((N_TAIL, D_MODEL), BF16)],
        compiler_params=_params(),
        name="attn_tail",
    )(sinks, xt, g.reshape(1, D_MODEL), wq, wo, cos_t, sin_t, kt, vt, ck, cv)


def _rope_tables(pos):
    inv = 1.0 / (ROPE_THETA ** (jnp.arange(0, HEAD_DIM, 2, dtype=F32) / HEAD_DIM))
    ang = pos.astype(F32)[:, None] * inv[None, :]
    cos, sin = jnp.cos(ang), jnp.sin(ang)
    return jnp.tile(cos, (1, N_KV)), jnp.tile(sin, (1, N_KV))


def kernel(x_prompt, x_sample, state_conv, cache_k, cache_v, meta_tokens, norm_g, ffn_w_in,
           ffn_w_out, conv_w_in, conv_kernel, conv_w_out, kv_norm_g, w_kv, w_q, w_o, sinks,
           final_norm_g):
    assert x_prompt.shape == (BATCH, SEQ, D_MODEL) and x_sample.shape == (DEC_BATCH, DEC_SEQ, D_MODEL)
    assert cache_k.shape == (DEC_BATCH, WINDOW, N_KV, HEAD_DIM)
    assert norm_g.shape[0] == 2 and conv_w_in.shape[0] == 1 and w_q.shape[0] == 1

    xm = x_prompt.reshape(N_MAIN, D_MODEL)
    xt = jnp.concatenate([x_sample.reshape(N_SAMPLE, D_MODEL), meta_tokens], axis=0)

    wq = w_q[0].reshape(D_MODEL, N_KV, GROUP, 2, HALF_DIM).transpose(0, 2, 3, 1, 4)
    wq = wq.reshape(D_MODEL, D_MODEL).astype(BF16)
    wk = w_kv[:, :KV_W].reshape(D_MODEL, N_KV, 2, HALF_DIM).transpose(0, 2, 1, 3)
    wkv = jnp.concatenate([wk.reshape(D_MODEL, KV_W), w_kv[:, KV_W:]], axis=1).astype(BF16)

    pos_main = N_META + jnp.arange(SEQ, dtype=jnp.int32)
    pos_tail = jnp.concatenate([
        jnp.tile(PAST_LEN + jnp.arange(DEC_SEQ, dtype=jnp.int32), DEC_BATCH),
        jnp.arange(N_META, dtype=jnp.int32)])
    cos_m, sin_m = _rope_tables(pos_main)
    cos_t, sin_t = _rope_tables(pos_tail)

    sc = state_conv[0]
    grow = lambda a: jnp.pad(a, ((0, 0), (0, DEC_SEQ - a.shape[1]), (0, 0))).reshape(N_SAMPLE, D_MODEL)
    s1 = grow(sc[:, 1:])
    s2 = grow(sc)

    ffn_casts = lambda l, j, steps=N_TILES: [(ffn_w_in, (l, j), steps, None),
                                             (ffn_w_out, (l, j), steps // 2, None)]
    wo_cast = (w_o, (0,), N_HEADS, lambda b: (b % N_KV) * GROUP + b // N_KV)
    slot_minor = lambda a: a.transpose(0, 2, 3, 1)
    ck, cv = slot_minor(cache_k), slot_minor(cache_v)

    xm, xt, cw_in, cw_out, w_in, w_out, w_in_10, w_out_10 = _ffn_first_call(
        xm, xt, norm_g[0, 0], ffn_w_in, ffn_w_out, (0, 0),
        [(conv_w_in, (0,), N_TILES, None), (conv_w_out, (0,), N_TILES, None)]
        + ffn_casts(0, 1) + ffn_casts(1, 0), name="ffn_l0a")
    xm, xt, u_tail, sp = _conv_call(xm, xt, norm_g[0, 1], cw_in, conv_kernel[0], cw_out, s1, s2)
    xm, xt, wo, w_in_11, w_out_11 = _ffn_call(
        xm, xt, norm_g[0, 2], w_in, w_out, name="ffn_l0b", tm=WIDE_TM,
        casts=[wo_cast] + ffn_casts(1, 1, N_MAIN // WIDE_TM))

    xm, xt, km, vm, kt, vt, k_last, v_last, new_k_t, new_v_t = _ffn_call(
        xm, xt, norm_g[1, 0], w_in_10, w_out_10, name="ffn_l1a_kv", tm=WIDE_TM,
        kv=(kv_norm_g, wkv, cos_m, sin_m, cos_t, sin_t))
    xt = _attn_tail_call(sinks[0], xt, norm_g[1, 1], wq, wo, cos_t, sin_t, kt, vt, ck, cv)
    xm = _attn_main_call(sinks[0], xm, norm_g[1, 1], wq, wo, cos_m, sin_m, km, vm, kt, vt)
    ym, yt, nk, nv = _ffn_call(xm, xt, norm_g[1, 2], w_in_11, w_out_11, name="ffn_l1b_final",
                               final_g=final_norm_g, roll=(new_k_t, new_v_t, ck, cv), tm=WIDE_TM)

    y_prompt = ym.reshape(BATCH, SEQ, D_MODEL)
    y_sample = yt[:N_SAMPLE].reshape(DEC_BATCH, DEC_SEQ, D_MODEL)
    new_state_conv_p = sp[:, CARRY - (CONV_W - 1):][None]
    new_state_conv_s = u_tail[:N_SAMPLE].reshape(DEC_BATCH, DEC_SEQ, D_MODEL)[:, DEC_SEQ - (CONV_W - 1):][None]
    slot_major = lambda a: a.transpose(0, 3, 1, 2)
    new_k_p = slot_major(k_last.reshape(BATCH, N_KV, HEAD_DIM, WINDOW))
    new_v_p = slot_major(v_last.reshape(BATCH, N_KV, HEAD_DIM, WINDOW))
    return (y_prompt, y_sample, new_state_conv_p, new_state_conv_s,
            new_k_p, new_v_p, slot_major(nk), slot_major(nv))
```
